```python
import math
import jax, jax.numpy as jnp
from jax import lax
import numpy as np

D_MODEL = 2048
BATCH = 32
SEQ = 256
DEPTH = 2
DEC_BATCH = 2
DEC_SEQ = 1024
PAST_LEN = 512

GRID_W = 64
DH = 128
H_A = 8
W_A = H_A * DH
WIN_H = 8
WIN_W = 16
H_B = 8
W_B = H_B * DH
ML_CHUNK = 128
H_C = 8
DQK = 64
W_CQK = H_C * 2 * DQK
W_CV = H_C * DH
ROPE_THETA = 10000.0
G_D = 8
SG_CHUNK = 128
W_D = 1024
SG_W = W_D // G_D
D_MIX = W_A + W_B
D_FF = 5632
IN_EVEN = 3 * W_A + 4 * W_B + 4 * H_B
IN_ODD = 2 * W_CQK + W_CV + 2 * W_D
N_EVEN = (DEPTH + 1) // 2
N_ODD = DEPTH // 2
Q_BLOCK = 128
EPS = 1e-6

kernel_name = 'hybrid_natten_mlstm_diffattn_sgu_dit_step'


def _rms(x, g):
    xf = x.astype(jnp.float32)
    y = xf * lax.rsqrt(jnp.mean(xf * xf, axis=-1, keepdims=True) + EPS)
    return (y * g.astype(jnp.float32)).astype(x.dtype)


def _modulation(cond, w_mod, b_mod):
    m = jax.nn.silu(cond) @ w_mod + b_mod
    return jnp.split(m[:, None, :], 6, axis=-1)


def _lambda_init(layer):
    return 0.8 - 0.6 * math.exp(-0.3 * layer)


def _axial_rope(x):
    S = x.shape[1]
    t = jnp.arange(S)
    pos = jnp.stack([t // GRID_W, t % GRID_W], axis=-1).astype(jnp.float32)
    half = x.shape[-1] // 2
    inv = ROPE_THETA ** (-jnp.arange(0, half, 2, dtype=jnp.float32) / half)
    ang = pos[:, :, None] * inv
    shape = (1, S) + (1,) * (x.ndim - 3) + (2, half // 2)
    cos = jnp.cos(ang).reshape(shape)
    sin = jnp.sin(ang).reshape(shape)
    xf = x.astype(jnp.float32).reshape(x.shape[:-1] + (2, half))
    x1, x2 = xf[..., :half // 2], xf[..., half // 2:]
    out = jnp.concatenate([x1 * cos - x2 * sin, x2 * cos + x1 * sin], axis=-1)
    return out.reshape(x.shape).astype(x.dtype)


def _sweep(fn, q):
    B, H, Sq = q.shape[:3]
    nb = Sq // Q_BLOCK
    qb = jnp.moveaxis(q.reshape((B, H, nb, Q_BLOCK) + q.shape[3:]), 2, 0)
    out = lax.map(fn, qb)
    return jnp.moveaxis(out, 0, 2).reshape((B, H, Sq) + out.shape[4:])


def _softmax_attend(q, k, v):
    scale = q.shape[-1] ** -0.5
    def blk(qb):
        s = jnp.einsum('bhqd,bhkd->bhqk', qb, k).astype(jnp.float32) * scale
        p = jax.nn.softmax(s, axis=-1).astype(v.dtype)
        return jnp.einsum('bhqk,bhkd->bhqd', p, v)
    return _sweep(blk, q)


def _diff_attend(q, k, v, lam):
    scale = q.shape[-1] ** -0.5
    def blk(qb):
        s = jnp.einsum('bhqmd,bhkmd->bhmqk', qb, k).astype(jnp.float32) * scale
        p = jax.nn.softmax(s, axis=-1)
        a = (p[:, :, 0] - lam * p[:, :, 1]).astype(v.dtype)
        return jnp.einsum('bhqk,bhkd->bhqd', a, v)
    return _sweep(blk, q)


def _neighbourhood_attend(q, k, v, k_ctx, v_ctx, rpb):
    B, H, S, dh = q.shape
    rows = S // GRID_W
    kh = min(WIN_H, rows)
    r = np.arange(rows)
    row0 = np.clip(r - kh // 2, 0, rows - kh)
    band = row0[:, None] + np.arange(kh)[None, :]
    qc = np.arange(GRID_W)
    col0 = np.clip(qc - WIN_W // 2, 0, GRID_W - WIN_W)
    kc = np.arange(GRID_W)
    col_ok = (kc[None, :] >= col0[:, None]) & (kc[None, :] < col0[:, None] + WIN_W)
    dr_idx = band - r[:, None] + WIN_H - 1
    dc_idx = np.clip(kc[None, :] - qc[:, None] + WIN_W - 1, 0, 2 * WIN_W - 2)
    bias = rpb[:, dr_idx[:, None, :, None], dc_idx[None, :, None, :]]
    bias = jnp.where(col_ok[None, None, :, None, :], bias.astype(jnp.float32), -jnp.inf)
    qg = q.reshape(B, H, rows, GRID_W, dh)
    kb = k.reshape(B, H, rows, GRID_W, dh)[:, :, band]
    vb = v.reshape(B, H, rows, GRID_W, dh)[:, :, band]
    scale = dh ** -0.5
    n_loc = kh * GRID_W
    s_loc = jnp.einsum('bhrqd,bhrikd->bhrqik', qg, kb).astype(jnp.float32) * scale + bias[None]
    s_ctx = jnp.einsum('bhrqd,bhld->bhrql', qg, k_ctx).astype(jnp.float32) * scale
    s = jnp.concatenate([s_loc.reshape(B, H, rows, GRID_W, n_loc), s_ctx], axis=-1)
    p = jax.nn.softmax(s, axis=-1).astype(v.dtype)
    p_loc = p[..., :n_loc].reshape(B, H, rows, GRID_W, kh, GRID_W)
    out = (jnp.einsum('bhrqik,bhrikd->bhrqd', p_loc, vb)
           + jnp.einsum('bhrql,bhld->bhrqd', p[..., n_loc:], v_ctx))
    return out.reshape(B, H, S, dh)


def _mlstm_scan(q, k, v, li, lf, C0, n0, m0):
    B, S, H, dh = q.shape
    nc = S // ML_CHUNK
    def chunks(a):
        a = jnp.moveaxis(a.reshape((B, nc, ML_CHUNK) + a.shape[2:]), 1, 0)
        return jnp.swapaxes(a, 2, 3)
    tril = jnp.tril(jnp.ones((ML_CHUNK, ML_CHUNK), dtype=bool))
    def step(carry, xs):
        C, n, m = carry
        qc, kc, vc, ic, fc = xs
        b = jnp.cumsum(fc, axis=-1)
        dmat = jnp.where(tril, b[..., :, None] - b[..., None, :] + ic[..., None, :], -jnp.inf)
        inter = b + m[..., None]
        mj = jnp.maximum(inter, jnp.max(dmat, axis=-1))
        w_in = jnp.exp(inter - mj)
        w = jnp.exp(dmat - mj[..., None]) * jnp.einsum('bhjd,bhsd->bhjs', qc, kc)
        num = w_in[..., None] * jnp.einsum('bhjd,bhde->bhje', qc, C) + jnp.einsum('bhjs,bhse->bhje', w, vc)
        den = w_in * jnp.einsum('bhjd,bhd->bhj', qc, n) + jnp.sum(w, axis=-1)
        h = num / jnp.maximum(jnp.abs(den), jnp.exp(-mj))[..., None]
        bl = b[..., -1]
        ds = bl[..., None] - b + ic
        m_new = jnp.maximum(bl + m, jnp.max(ds, axis=-1))
        a_prev = jnp.exp(bl + m - m_new)
        ws = jnp.exp(ds - m_new[..., None])
        C = a_prev[..., None, None] * C + jnp.einsum('bhs,bhsd,bhse->bhde', ws, kc, vc)
        n = a_prev[..., None] * n + jnp.einsum('bhs,bhsd->bhd', ws, kc)
        return (C, n, m_new), h
    xs = (chunks(q), chunks(k), chunks(v), chunks(li), chunks(lf))
    (C, n, m), h = lax.scan(step, (C0, n0, m0), xs)
    h = jnp.moveaxis(jnp.swapaxes(h, 2, 3), 0, 1).reshape(B, S, H, dh)
    return h, C, n, m


def _mlstm_bidir(q, k, v, gates, C0, n0, m0):
    f32 = lambda a: a.astype(jnp.float32)
    q, k, v, gates = f32(q), f32(k), f32(v), f32(gates)
    li_f, lf_f = gates[:, :, 0], jax.nn.log_sigmoid(gates[:, :, 1])
    li_b, lf_b = gates[:, :, 2], jax.nn.log_sigmoid(gates[:, :, 3])
    h_f, Cf, nf, mf = _mlstm_scan(q, k, v, li_f, lf_f, f32(C0[:, 0]), f32(n0[:, 0]), f32(m0[:, 0]))
    rev = lambda a: jnp.flip(a, axis=1)
    h_b, Cb, nb, mb = _mlstm_scan(rev(q), rev(k), rev(v), rev(li_b), rev(lf_b),
                                  f32(C0[:, 1]), f32(n0[:, 1]), f32(m0[:, 1]))
    return (h_f + rev(h_b), jnp.stack([Cf, Cb], axis=1), jnp.stack([nf, nb], axis=1), jnp.stack([mf, mb], axis=1))


def _even_mixer(h, w_in, na_gq, na_gk, na_rpb, ml_b, ml_g, cache=None):
    B, S, _ = h.shape
    cuts = [W_A, 2 * W_A, 3 * W_A, 3 * W_A + W_B, 3 * W_A + 2 * W_B, 3 * W_A + 3 * W_B, 3 * W_A + 4 * W_B]
    qa, ka, va, qb, kb, vb, ob, gates = jnp.split(h @ w_in, cuts, axis=-1)
    qa = _rms(qa.reshape(B, S, H_A, DH), na_gq).transpose(0, 2, 1, 3)
    ka = _rms(ka.reshape(B, S, H_A, DH), na_gk).transpose(0, 2, 1, 3)
    va = va.reshape(B, S, H_A, DH).transpose(0, 2, 1, 3)
    qb = qb.reshape(B, S, H_B, DH)
    kb = kb.reshape(B, S, H_B, DH) * (DH ** -0.5)
    vb = vb.reshape(B, S, H_B, DH)
    gates = gates.reshape(B, S, 4, H_B) + ml_b.reshape(4, H_B)
    if cache is None:
        out_a = _softmax_attend(qa, ka, va)
        C0 = jnp.zeros((B, 2, H_B, DH, DH), jnp.float32)
        n0 = jnp.zeros((B, 2, H_B, DH), jnp.float32)
        m0 = jnp.zeros((B, 2, H_B), jnp.float32)
    else:
        k_ctx, v_ctx, C0, n0, m0 = cache
        out_a = _neighbourhood_attend(qa, ka, va, k_ctx, v_ctx, na_rpb)
    hm, C1, n1, m1 = _mlstm_bidir(qb, kb, vb, gates, C0, n0, m0)
    out_b = jax.nn.sigmoid(ob) * _rms(hm.astype(h.dtype), ml_g.reshape(H_B, DH)).reshape(B, S, W_B)
    out = jnp.concatenate([out_a.transpose(0, 2, 1, 3).reshape(B, S, W_A), out_b], axis=-1)
    ctx = (ka, va, C1.astype(h.dtype), n1.astype(h.dtype), m1.astype(h.dtype)) if cache is None else ()
    return out, ctx


def _odd_mixer(h, w_in, gq, gk, lam_vec, g_out, sg_g, sg_w, sg_b, lam_init, cache=None):
    B, S, _ = h.shape
    cuts = [W_CQK, 2 * W_CQK, 2 * W_CQK + W_CV, 2 * W_CQK + W_CV + W_D]
    q, k, v, u, vd = jnp.split(h @ w_in, cuts, axis=-1)
    q = _rms(q.reshape(B, S, H_C, 2, DQK), gq)
    k = _rms(k.reshape(B, S, H_C, 2, DQK), gk)
    v = v.reshape(B, S, H_C, DH).transpose(0, 2, 1, 3)
    lam = (jnp.exp(jnp.sum(lam_vec[0] * lam_vec[1])) - jnp.exp(jnp.sum(lam_vec[2] * lam_vec[3])) + lam_init).astype(jnp.float32)
    if cache is None:
        q = q.transpose(0, 2, 1, 3, 4)
        k = k.transpose(0, 2, 1, 3, 4)
        out = _diff_attend(q, k, v, lam)
        ctx = (k, v)
    else:
        k_ctx, v_ctx = cache
        q = _axial_rope(q).transpose(0, 2, 1, 3, 4)
        k = _axial_rope(k).transpose(0, 2, 1, 3, 4)
        out = _diff_attend(q, jnp.concatenate([k, k_ctx], axis=2), jnp.concatenate([v, v_ctx], axis=2), lam)
        ctx = ()
    out_c = (_rms(out.transpose(0, 2, 1, 3), g_out.reshape(H_C, DH)) * (1.0 - lam_init)).reshape(B, S, W_CV)
    u = jax.nn.gelu(u)
    vd = _rms(jax.nn.gelu(vd), sg_g).reshape(B, S // SG_CHUNK, SG_CHUNK, G_D, SG_W)
    gate = jnp.einsum('gpt,bntgc->bnpgc', sg_w, vd) + sg_b.T[:, :, None]
    out_d = u * gate.reshape(B, S, W_D)
    return jnp.concatenate([out_c, out_d], axis=-1), ctx


def _conv_ffn(h, w_up, conv_w, conv_b, w_down):
    a = h @ w_up
    ap = jnp.pad(a, ((0, 0), (1, 1), (0, 0)))
    a = ap[:, :-2] * conv_w[0] + ap[:, 1:-1] * conv_w[1] + ap[:, 2:] * conv_w[2] + conv_b
    g, val = jnp.split(a, 2, axis=-1)
    return (jax.nn.silu(g) * val) @ w_down


def _run_trunk(x, cond, p, caches=None):
    even_out, odd_out = [], []
    for l in range(DEPTH):
        j = l // 2
        sm, cm, gm, sf, cf, gf = _modulation(cond, p['w_mod'][l], p['b_mod'][l])
        h = _rms(x, p['g_mix'][l]) * (1 + cm) + sm
        if l % 2 == 0:
            cache = None if caches is None else tuple(a[:, j] for a in caches[0])
            mix, ctx = _even_mixer(h, p['w_in_even'][j], p['na_gq'][j], p['na_gk'][j], p['na_rpb'][j],
                                   p['ml_b_gates'][j], p['ml_g_out'][j], cache)
            even_out.append(ctx)
        else:
            cache = None if caches is None else tuple(a[:, j] for a in caches[1])
            mix, ctx = _odd_mixer(h, p['w_in_odd'][j], p['diff_gq'][j], p['diff_gk'][j], p['diff_lam'][j],
                                  p['diff_g_out'][j], p['sg_g_v'][j], p['sg_w'][j], p['sg_b'][j],
                                  _lambda_init(l), cache)
            odd_out.append(ctx)
        x = x + gm * (mix @ p['w_out'][l])
        h = _rms(x, p['g_ffn'][l]) * (1 + cf) + sf
        x = x + gf * _conv_ffn(h, p['ffn_w_up'][l], p['ffn_conv_w'][l], p['ffn_conv_b'][l], p['ffn_w_down'][l])
    stack = lambda outs: [jnp.stack(t, axis=1) for t in zip(*outs)]
    return x, stack(even_out), stack(odd_out)


def setup_inputs(seed: int = 0) -> dict:
    key = jax.random.key(seed)
    keys = iter(jax.random.split(key, 64))
    def nrm(shape, scale):
        return jax.random.normal(next(keys), shape, jnp.float32) * scale
    def gain(shape):
        return 1.0 + nrm(shape, 0.02)
    f_bias = jnp.linspace(3.0, 6.0, H_B, dtype=jnp.float32)
    ml_b_gates = jnp.concatenate([nrm((N_EVEN, H_B), 0.1), f_bias + nrm((N_EVEN, H_B), 0.1),
                                  nrm((N_EVEN, H_B), 0.1), f_bias + nrm((N_EVEN, H_B), 0.1)], axis=-1)
    return {
        'x_prompt': nrm((BATCH, SEQ, D_MODEL), 1.0),
        'x_sample': nrm((DEC_BATCH, DEC_SEQ, D_MODEL), 1.0),
        'cache_na_k': nrm((DEC_BATCH, N_EVEN, H_A, PAST_LEN, DH), 1.0),
        'cache_na_v': nrm((DEC_BATCH, N_EVEN, H_A, PAST_LEN, DH), 1.0),
        'state_mlstm_C': nrm((DEC_BATCH, N_EVEN, 2, H_B, DH, DH), 0.1),
        'state_mlstm_n': nrm((DEC_BATCH, N_EVEN, 2, H_B, DH), 0.1),
        'state_mlstm_m': nrm((DEC_BATCH, N_EVEN, 2, H_B), 1.0),
        'cache_diff_k': nrm((DEC_BATCH, N_ODD, H_C, PAST_LEN, 2, DQK), 1.0),
        'cache_diff_v': nrm((DEC_BATCH, N_ODD, H_C, PAST_LEN, DH), 1.0),
        'c': nrm((DEC_BATCH, D_MODEL), 1.0),
        'c_ctx': nrm((D_MODEL,), 1.0),
        'w_mod': nrm((DEPTH, D_MODEL, 6 * D_MODEL), 0.3 * D_MODEL ** -0.5),
        'b_mod': nrm((DEPTH, 6 * D_MODEL), 0.02),
        'g_mix': gain((DEPTH, D_MODEL)),
        'g_ffn': gain((DEPTH, D_MODEL)),
        'w_out': nrm((DEPTH, D_MIX, D_MODEL), D_MIX ** -0.5),
        'w_in_even': nrm((N_EVEN, D_MODEL, IN_EVEN), D_MODEL ** -0.5),
        'na_gq': gain((N_EVEN, DH)),
        'na_gk': gain((N_EVEN, DH)),
        'na_rpb': nrm((N_EVEN, H_A, 2 * WIN_H - 1, 2 * WIN_W - 1), 0.5),
        'ml_b_gates': ml_b_gates,
        'ml_g_out': gain((N_EVEN, W_B)),
        'w_in_odd': nrm((N_ODD, D_MODEL, IN_ODD), D_MODEL ** -0.5),
        'diff_gq': gain((N_ODD, DQK)),
        'diff_gk': gain((N_ODD, DQK)),
        'diff_lam': nrm((N_ODD, 4, DQK), 0.1),
        'diff_g_out': gain((N_ODD, W_CV)),
        'sg_g_v': gain((N_ODD, W_D)),
        'sg_w': nrm((N_ODD, G_D, SG_CHUNK, SG_CHUNK), SG_CHUNK ** -0.5),
        'sg_b': 1.0 + nrm((N_ODD, G_D, SG_CHUNK), 0.02),
        'ffn_w_up': nrm((DEPTH, D_MODEL, 2 * D_FF), D_MODEL ** -0.5),
        'ffn_conv_w': nrm((DEPTH, 3, 2 * D_FF), 3 ** -0.5),
        'ffn_conv_b': nrm((DEPTH, 2 * D_FF), 0.02),
        'ffn_w_down': nrm((DEPTH, D_FF, D_MODEL), D_FF ** -0.5),
    }


def reference(x_prompt, x_sample, cache_na_k, cache_na_v, state_mlstm_C, state_mlstm_n, state_mlstm_m,
              cache_diff_k, cache_diff_v, c, c_ctx, w_mod, b_mod, g_mix, g_ffn, w_out, w_in_even,
              na_gq, na_gk, na_rpb, ml_b_gates, ml_g_out, w_in_odd, diff_gq, diff_gk, diff_lam,
              diff_g_out, sg_g_v, sg_w, sg_b, ffn_w_up, ffn_conv_w, ffn_conv_b, ffn_w_down):
    p = dict(w_mod=w_mod, b_mod=b_mod, g_mix=g_mix, g_ffn=g_ffn, w_out=w_out, w_in_even=w_in_even,
             na_gq=na_gq, na_gk=na_gk, na_rpb=na_rpb, ml_b_gates=ml_b_gates, ml_g_out=ml_g_out,
             w_in_odd=w_in_odd, diff_gq=diff_gq, diff_gk=diff_gk, diff_lam=diff_lam, diff_g_out=diff_g_out,
             sg_g_v=sg_g_v, sg_w=sg_w, sg_b=sg_b, ffn_w_up=ffn_w_up, ffn_conv_w=ffn_conv_w,
             ffn_conv_b=ffn_conv_b, ffn_w_down=ffn_w_down)
    y_prompt, even_ctx, odd_ctx = _run_trunk(x_prompt, c_ctx[None, :], p)
    na_k, na_v, mlstm_C, mlstm_n, mlstm_m = even_ctx
    diff_k, diff_v = odd_ctx
    caches = ((cache_na_k, cache_na_v, state_mlstm_C, state_mlstm_n, state_mlstm_m), (cache_diff_k, cache_diff_v))
    y_sample, _, _ = _run_trunk(x_sample, c, p, caches)
    return (y_prompt, y_sample, na_k, na_v, mlstm_C, mlstm_n, mlstm_m, diff_k, diff_v)
```

```python
import functools
import math

import numpy as np
import jax
import jax.numpy as jnp
from jax import lax
from jax.experimental import pallas as pl
from jax.experimental.pallas import tpu as pltpu

F32 = jnp.float32
BF16 = jnp.bfloat16

D_MODEL = 2048
DEPTH = 2
GRID_W = 64
DH = 128
H_A = 8
W_A = H_A * DH
WIN_H = 8
WIN_W = 16
H_B = 8
W_B = H_B * DH
ML_CHUNK = 128
H_C = 8
DQK = 64
W_CQK = H_C * 2 * DQK
W_CV = H_C * DH
ROPE_THETA = 10000.0
G_D = 8
SG_CHUNK = 128
W_D = 1024
D_FF = 5632
Q_BLOCK = 128
EPS = 1e-6
N_GATES = 4 * H_B

LANES = 128
V7X_VMEM_BYTES = 64 * 1024 * 1024
VMEM_LIMIT = V7X_VMEM_BYTES - 8 * 1024 * 1024

NEG_INF = float("-inf")


def _cparams(*sem):
    return pltpu.CompilerParams(dimension_semantics=sem, vmem_limit_bytes=VMEM_LIMIT)


def _dot(a, b):
    return jnp.dot(a, b, preferred_element_type=F32)


def _dot_nt(a, b):
    return lax.dot_general(a, b, (((1,), (1,)), ((), ())), preferred_element_type=F32)


def _rms_lanes(x, g):
    ms = jnp.mean(x * x, axis=-1, keepdims=True)
    return x * lax.rsqrt(ms + EPS) * g


def _norm_mod(x, g, scale, shift):
    return _rms_lanes(x, g) * (1.0 + scale) + shift


def _sigmoid(x):
    return 1.0 / (1.0 + jnp.exp(-x))


def _gelu_tanh(x):
    c = math.sqrt(2.0 / math.pi)
    return 0.5 * x * (1.0 + jnp.tanh(c * (x + 0.044715 * (x * x * x))))


MOD_TN = 1024


def _mod_kernel(c_ref, w_ref, b_ref, o_ref):
    c = c_ref[...]
    s = c * _sigmoid(c)
    o_ref[...] = _dot(s.astype(BF16), w_ref[...].astype(BF16)) + b_ref[...]


def _modulation(cond8, w_mod, b_mod):
    n = 6 * D_MODEL
    return pl.pallas_call(
        _mod_kernel,
        grid=(DEPTH, n // MOD_TN),
        in_specs=[
            pl.BlockSpec((8, D_MODEL), lambda l, j: (0, 0)),
            pl.BlockSpec((None, D_MODEL, MOD_TN), lambda l, j: (l, 0, j)),
            pl.BlockSpec((None, 1, MOD_TN), lambda l, j: (l, 0, j)),
        ],
        out_specs=pl.BlockSpec((None, 8, MOD_TN), lambda l, j: (l, 0, j)),
        out_shape=jax.ShapeDtypeStruct((DEPTH, 8, n), F32),
        compiler_params=_cparams("parallel", "arbitrary"),
        name="modulation",
    )(cond8, w_mod, b_mod.reshape(DEPTH, 1, n))


DENSE_TM = 1024
NORM_ROWS = 256


def _fill_normed(x_ref, g_ref, sc_ref, sh_ref, hb_ref):
    tm = x_ref.shape[0]

    def body(c, carry):
        r = pl.ds(pl.multiple_of(c * NORM_ROWS, NORM_ROWS), NORM_ROWS)
        h = _norm_mod(x_ref[r, :], g_ref[...], sc_ref[...], sh_ref[...])
        hb_ref[r, :] = h.astype(BF16)
        return carry

    lax.fori_loop(0, tm // NORM_ROWS, body, 0)


def _inproj_kernel(x_ref, g_ref, sc_ref, sh_ref, w_ref, o_ref, hb_ref):
    @pl.when(pl.program_id(1) == 0)
    def _():
        _fill_normed(x_ref, g_ref, sc_ref, sh_ref, hb_ref)

    o_ref[...] = _dot(hb_ref[...], w_ref[...]).astype(o_ref.dtype)


def _inproj_gates_kernel(x_ref, g_ref, sc_ref, sh_ref, w_ref, wg_ref, o_ref, og_ref, hb_ref):
    @pl.when(pl.program_id(1) == 0)
    def _():
        _fill_normed(x_ref, g_ref, sc_ref, sh_ref, hb_ref)
        og_ref[...] = _dot(hb_ref[...], wg_ref[...])

    o_ref[...] = _dot(hb_ref[...], w_ref[...]).astype(o_ref.dtype)


def _mod_spec(idx, tm, rows_per_mod):
    return pl.BlockSpec((None, None, 1, D_MODEL), lambda i, n: ((i * tm) // rows_per_mod, idx, 0, 0))


def _inproj(x, g, mod, scale_idx, shift_idx, w, w_gates=None, tn=1024):
    t = x.shape[0]
    tm = DENSE_TM
    n = w.shape[1]
    rows_per_mod = t // mod.shape[0]
    in_specs = [
        pl.BlockSpec((tm, D_MODEL), lambda i, j: (i, 0)),
        pl.BlockSpec((1, D_MODEL), lambda i, j: (0, 0)),
        _mod_spec(scale_idx, tm, rows_per_mod),
        _mod_spec(shift_idx, tm, rows_per_mod),
        pl.BlockSpec((D_MODEL, tn), lambda i, j: (0, j)),
    ]
    out_specs = pl.BlockSpec((tm, tn), lambda i, j: (i, j))
    out_shape = jax.ShapeDtypeStruct((t, n), F32)
    args = [x, g.reshape(1, D_MODEL), mod, mod, w]
    kern = _inproj_kernel
    if w_gates is not None:
        in_specs.append(pl.BlockSpec((D_MODEL, LANES), lambda i, j: (0, 0)))
        out_specs = [out_specs, pl.BlockSpec((tm, LANES), lambda i, j: (i, 0))]
        out_shape = [out_shape, jax.ShapeDtypeStruct((t, LANES), F32)]
        args.append(w_gates)
        kern = _inproj_gates_kernel
    return pl.pallas_call(
        kern,
        grid=(t // tm, n // tn),
        in_specs=in_specs,
        out_specs=out_specs,
        out_shape=out_shape,
        scratch_shapes=[pltpu.VMEM((tm, D_MODEL), BF16)],
        compiler_params=_cparams("parallel", "arbitrary"),
        name="inproj",
    )(*args)


OUT_TN = 512


def _outproj_kernel(a_ref, b_ref, wa_ref, wb_ref, x_ref, gate_ref, o_ref):
    acc = _dot(a_ref[...], wa_ref[...]) + _dot(b_ref[...], wb_ref[...])
    o_ref[...] = x_ref[...] + gate_ref[...] * acc


def _outproj(mix_a, mix_b, w, x, mod, gate_idx):
    t = x.shape[0]
    tm, tn = DENSE_TM, OUT_TN
    half = mix_a.shape[1]
    rows_per_mod = t // mod.shape[0]
    return pl.pallas_call(
        _outproj_kernel,
        grid=(t // tm, D_MODEL // tn),
        in_specs=[
            pl.BlockSpec((tm, half), lambda i, j: (i, 0)),
            pl.BlockSpec((tm, half), lambda i, j: (i, 0)),
            pl.BlockSpec((half, tn), lambda i, j: (0, j)),
            pl.BlockSpec((half, tn), lambda i, j: (1, j)),
            pl.BlockSpec((tm, tn), lambda i, j: (i, j)),
            pl.BlockSpec((None, None, 1, tn), lambda i, j: ((i * tm) // rows_per_mod, gate_idx, 0, j)),
        ],
        out_specs=pl.BlockSpec((tm, tn), lambda i, j: (i, j)),
        out_shape=jax.ShapeDtypeStruct((t, D_MODEL), F32),
        compiler_params=_cparams("parallel", "arbitrary"),
        name="outproj",
    )(mix_a, mix_b, w, w, x, mod)


FFN_TF = 512


def _ffn_kernel(x_ref, g_ref, sc_ref, sh_ref, gate_ref, wg_ref, wv_ref, cwg_ref, cwv_ref, cbg_ref, cbv_ref,
                wd_ref, o_ref, hb_ref, *, seq_len):
    f = pl.program_id(1)
    tm = x_ref.shape[0]

    @pl.when(f == 0)
    def _():
        _fill_normed(x_ref, g_ref, sc_ref, sh_ref, hb_ref)
        o_ref[...] = jnp.zeros_like(o_ref)

    pos = lax.broadcasted_iota(jnp.int32, (tm, 1), 0) & (seq_len - 1)
    first = pos == 0
    last = pos == seq_len - 1

    def conv(a, cw_ref, cb_ref):
        prev = jnp.where(first, 0.0, pltpu.roll(a, 1, 0))
        nxt = jnp.where(last, 0.0, pltpu.roll(a, tm - 1, 0))
        return prev * cw_ref[0:1, :] + a * cw_ref[1:2, :] + nxt * cw_ref[2:3, :] + cb_ref[...]

    hb = hb_ref[...]
    cg = conv(_dot(hb, wg_ref[...]), cwg_ref, cbg_ref)
    cv = conv(_dot(hb, wv_ref[...]), cwv_ref, cbv_ref)
    act = (cg * _sigmoid(cg) * cv).astype(BF16)
    o_ref[...] += _dot(act, wd_ref[...])

    @pl.when(f == pl.num_programs(1) - 1)
    def _():
        o_ref[...] = x_ref[...] + gate_ref[...] * o_ref[...]


def _ffn(x, g, mod, w_up, conv_w, conv_b, w_down, seq_len):
    t = x.shape[0]
    tm, tf = DENSE_TM, FFN_TF
    nf = D_FF // tf
    rows_per_mod = t // mod.shape[0]
    conv_b = conv_b.reshape(1, 2 * D_FF)
    assert seq_len & (seq_len - 1) == 0 and tm % seq_len == 0
    return pl.pallas_call(
        functools.partial(_ffn_kernel, seq_len=seq_len),
        grid=(t // tm, nf),
        in_specs=[
            pl.BlockSpec((tm, D_MODEL), lambda i, f: (i, 0), pipeline_mode=pl.Buffered(1)),
            pl.BlockSpec((1, D_MODEL), lambda i, f: (0, 0)),
            _mod_spec(4, tm, rows_per_mod),
            _mod_spec(3, tm, rows_per_mod),
            _mod_spec(5, tm, rows_per_mod),
            pl.BlockSpec((D_MODEL, tf), lambda i, f: (0, f)),
            pl.BlockSpec((D_MODEL, tf), lambda i, f: (0, nf + f)),
            pl.BlockSpec((3, tf), lambda i, f: (0, f)),
            pl.BlockSpec((3, tf), lambda i, f: (0, nf + f)),
            pl.BlockSpec((1, tf), lambda i, f: (0, f)),
            pl.BlockSpec((1, tf), lambda i, f: (0, nf + f)),
            pl.BlockSpec((tf, D_MODEL), lambda i, f: (f, 0)),
        ],
        out_specs=pl.BlockSpec((tm, D_MODEL), lambda i, f: (i, 0)),
        out_shape=jax.ShapeDtypeStruct((t, D_MODEL), F32),
        scratch_shapes=[pltpu.VMEM((tm, D_MODEL), BF16)],
        compiler_params=_cparams("parallel", "arbitrary"),
        name="convffn",
    )(x, g.reshape(1, D_MODEL), mod, mod, mod, w_up, w_up, conv_w, conv_w, conv_b, conv_b, w_down)


def _attn_ctx_kernel(q_ref, k_ref, v_ref, gq_ref, gk_ref, o_ref, ko_ref, vo_ref):
    scale = DH ** -0.5
    for h in range(H_A):
        sl = slice(h * DH, (h + 1) * DH)
        q = _rms_lanes(q_ref[:, sl], gq_ref[...])
        k = _rms_lanes(k_ref[:, sl], gk_ref[...])
        v = v_ref[:, sl]
        ko_ref[h] = k
        vo_ref[h] = v
        s = _dot_nt(q.astype(BF16), k.astype(BF16)) * scale
        p = jnp.exp(s - jnp.max(s, axis=-1, keepdims=True))
        p = p / jnp.sum(p, axis=-1, keepdims=True)
        o_ref[:, sl] = _dot(p.astype(BF16), v.astype(BF16)).astype(o_ref.dtype)


def _attn_ctx(proj, gq, gk, batch, seq):
    t = proj.shape[0]
    kv_shape = jax.ShapeDtypeStruct((batch, 1, H_A, seq, DH), F32)
    kv_spec = pl.BlockSpec((None, None, H_A, seq, DH), lambda b: (b, 0, 0, 0, 0))
    return pl.pallas_call(
        _attn_ctx_kernel,
        grid=(batch,),
        in_specs=[
            pl.BlockSpec((seq, W_A), lambda b: (b, 0)),
            pl.BlockSpec((seq, W_A), lambda b: (b, 1)),
            pl.BlockSpec((seq, W_A), lambda b: (b, 2)),
            pl.BlockSpec((1, DH), lambda b: (0, 0)),
            pl.BlockSpec((1, DH), lambda b: (0, 0)),
        ],
        out_specs=[pl.BlockSpec((seq, W_A), lambda b: (b, 0)), kv_spec, kv_spec],
        out_shape=[jax.ShapeDtypeStruct((t, W_A), BF16), kv_shape, kv_shape],
        compiler_params=_cparams("parallel"),
        name="attn_ctx",
    )(proj, proj, proj, gq.reshape(1, DH), gk.reshape(1, DH))


NA_ROWS = 16
NA_PAIRS = 2 * WIN_H - 2
RPB_H = 2 * WIN_H - 1
RPB_W = 2 * WIN_W - 1


def _natten_kernel(rpb_ref, q_ref, k_ref, v_ref, kc_ref, vc_ref, gq_ref, gk_ref, o_ref,
                   qn_s, kn_s, v_s, kc_s, vc_s, bias_s):
    h = pl.program_id(1)
    scale = DH ** -0.5
    qn_s[...] = _rms_lanes(q_ref[...], gq_ref[...]).astype(BF16)
    kn_s[...] = _rms_lanes(k_ref[...], gk_ref[...]).astype(BF16)
    v_s[...] = v_ref[...].astype(BF16)
    kc_s[...] = kc_ref[...].astype(BF16)
    vc_s[...] = vc_ref[...].astype(BF16)

    lane = lax.broadcasted_iota(jnp.int32, (GRID_W, LANES), 1)
    qc = lax.broadcasted_iota(jnp.int32, (GRID_W, LANES), 0)
    kcol = lane % GRID_W
    upper = lane >= GRID_W
    dcol = kcol - qc + (WIN_W - 1)
    col0 = jnp.clip(qc - WIN_W // 2, 0, GRID_W - WIN_W)
    col_ok = (kcol >= col0) & (kcol < col0 + WIN_W)
    base = h * (RPB_H * RPB_W)

    def build(p, carry):
        acc = jnp.zeros((GRID_W, LANES), F32)
        for d in range(RPB_W):
            lo = rpb_ref[base + p * RPB_W + d]
            hi = rpb_ref[base + (p + 1) * RPB_W + d]
            acc = jnp.where(dcol == d, jnp.where(upper, hi, lo), acc)
        bias_s[p] = jnp.where(col_ok, acc, NEG_INF)
        return carry

    lax.fori_loop(0, NA_PAIRS, build, 0)

    kh = WIN_H
    n_loc = kh * GRID_W
    for r in range(NA_ROWS):
        row0 = min(max(r - kh // 2, 0), NA_ROWS - kh)
        dr0 = row0 - r + WIN_H - 1
        q = qn_s[r * GRID_W:(r + 1) * GRID_W, :]
        k_loc = kn_s[row0 * GRID_W:row0 * GRID_W + n_loc, :]
        v_loc = v_s[row0 * GRID_W:row0 * GRID_W + n_loc, :]
        bias = jnp.concatenate([bias_s[dr0 + 2 * i] for i in range(kh // 2)], axis=-1)
        s_loc = _dot_nt(q, k_loc) * scale + bias
        s_ctx = _dot_nt(q, kc_s[...]) * scale
        m = jnp.maximum(jnp.max(s_loc, axis=-1, keepdims=True), jnp.max(s_ctx, axis=-1, keepdims=True))
        p_loc = jnp.exp(s_loc - m)
        p_ctx = jnp.exp(s_ctx - m)
        l = jnp.sum(p_loc, axis=-1, keepdims=True) + jnp.sum(p_ctx, axis=-1, keepdims=True)
        out = _dot((p_loc / l).astype(BF16), v_loc) + _dot((p_ctx / l).astype(BF16), vc_s[...])
        o_ref[r * GRID_W:(r + 1) * GRID_W, :] = out.astype(o_ref.dtype)


def _natten(proj, cache_k, cache_v, gq, gk, rpb, batch, seq):
    t = proj.shape[0]
    past = cache_k.shape[3]
    ctx_spec = pl.BlockSpec((None, None, None, past, DH), lambda b, h: (b, 0, h, 0, 0))
    return pl.pallas_call(
        _natten_kernel,
        grid=(batch, H_A),
        in_specs=[
            pl.BlockSpec(memory_space=pltpu.SMEM),
            pl.BlockSpec((seq, DH), lambda b, h: (b, h)),
            pl.BlockSpec((seq, DH), lambda b, h: (b, H_A + h)),
            pl.BlockSpec((seq, DH), lambda b, h: (b, 2 * H_A + h)),
            ctx_spec,
            ctx_spec,
            pl.BlockSpec((1, DH), lambda b, h: (0, 0)),
            pl.BlockSpec((1, DH), lambda b, h: (0, 0)),
        ],
        out_specs=pl.BlockSpec((seq, DH), lambda b, h: (b, h)),
        out_shape=jax.ShapeDtypeStruct((t, W_A), BF16),
        scratch_shapes=[
            pltpu.VMEM((seq, DH), BF16),
            pltpu.VMEM((seq, DH), BF16),
            pltpu.VMEM((seq, DH), BF16),
            pltpu.VMEM((past, DH), BF16),
            pltpu.VMEM((past, DH), BF16),
            pltpu.VMEM((NA_PAIRS, GRID_W, LANES), F32),
        ],
        compiler_params=_cparams("parallel", "arbitrary"),
        name="natten",
    )(rpb.reshape(-1), proj, proj, proj, cache_k, cache_v, gq.reshape(1, DH), gk.reshape(1, DH))


def _split3(x):
    hi = x.astype(BF16)
    r1 = x - hi.astype(F32)
    mid = r1.astype(BF16)
    lo = (r1 - mid.astype(F32)).astype(BF16)
    return hi, mid, lo


def _mlstm_kernel(*refs, has_state, emit_state, seq):
    it = iter(refs)
    q_ref, k_ref, v_ref, ob_ref, gt_ref, mlb_ref, mlg_ref = (next(it) for _ in range(7))
    if has_state:
        c0_ref, n0_ref, m0_ref = (next(it) for _ in range(3))
    o_ref = next(it)
    if emit_state:
        co_ref, no_ref, mo_ref = (next(it) for _ in range(3))
    lf_s, hf_s, hb_s, c_s, n_s, m_s = (next(it) for _ in range(6))

    L = ML_CHUNK
    nc = seq // L
    kscale = DH ** -0.5

    gcol = lax.broadcasted_iota(jnp.int32, (seq, LANES), 1)
    graw = gt_ref[...] + mlb_ref[...]
    logsig = jnp.minimum(graw, 0.0) - jnp.log1p(jnp.exp(-jnp.abs(graw)))
    is_forget = ((gcol >= H_B) & (gcol < 2 * H_B)) | ((gcol >= 3 * H_B) & (gcol < 4 * H_B))
    lf_s[...] = jnp.where(is_forget, logsig, graw)

    if has_state:
        c_s[...] = c0_ref[...]
        n_s[...] = n0_ref[...]
        m_s[...] = m0_ref[...]
    else:
        c_s[...] = jnp.zeros_like(c_s)
        n_s[...] = jnp.zeros_like(n_s)
        m_s[...] = jnp.zeros_like(m_s)

    ri = lax.broadcasted_iota(jnp.int32, (L, L), 0)
    ci = lax.broadcasted_iota(jnp.int32, (L, L), 1)
    causal = (ci <= ri, ci >= ri)
    tri = tuple(m.astype(BF16) for m in causal)
    lane_row = lax.broadcasted_iota(jnp.int32, (L, LANES), 1)

    def chunk_sums(g):
        hi, mid, lo = _split3(g)
        pf = _dot(tri[0], hi) + _dot(tri[0], mid) + _dot(tri[0], lo)
        pb = _dot(tri[1], hi) + _dot(tri[1], mid) + _dot(tri[1], lo)
        return pf, pb

    def step(c, carry):
        for d in range(2):
            cc = c if d == 0 else nc - 1 - c
            rows = pl.ds(pl.multiple_of(cc * L, L), L)
            g = lf_s[rows, :]
            pf, pb = chunk_sums(g)
            bsum = pf if d == 0 else pb
            bsum_t = bsum.T
            g_t = g.T
            h_dst = hf_s if d == 0 else hb_s
            for h in range(H_B):
                ci_col = h + 2 * H_B * d
                cb_col = ci_col + H_B
                sl = slice(h * DH, (h + 1) * DH)
                b_col = bsum[:, cb_col:cb_col + 1]
                i_col = g[:, ci_col:ci_col + 1]
                b_row = bsum_t[cb_col:cb_col + 1, :]
                i_row = g_t[ci_col:ci_col + 1, :]
                bl = b_row[:, L - 1:L] if d == 0 else b_row[:, 0:1]
                m_prev = m_s[d, h:h + 1, 0:1]
                c_prev = c_s[d, h]
                n_prev = n_s[d, h:h + 1, :]

                q = q_ref[rows, sl]
                k = k_ref[rows, sl] * kscale
                v = v_ref[rows, sl]
                qb, kb, vb = q.astype(BF16), k.astype(BF16), v.astype(BF16)

                dmat = jnp.where(causal[d], b_col - b_row + i_row, NEG_INF)
                inter = b_col + m_prev
                mj = jnp.maximum(inter, jnp.max(dmat, axis=-1, keepdims=True))
                w_in = jnp.exp(inter - mj)
                w = jnp.exp(dmat - mj) * _dot_nt(qb, kb)
                num = w_in * _dot(qb, c_prev.astype(BF16)) + _dot(w.astype(BF16), vb)
                qn = jnp.sum(q * n_prev, axis=-1, keepdims=True)
                den = w_in * qn + jnp.sum(w, axis=-1, keepdims=True)
                hval = num / jnp.maximum(jnp.abs(den), jnp.exp(-mj))
                h_dst[rows, sl] = hval

                ds_row = bl - b_row + i_row
                ds_col = bl - b_col + i_col
                m_new = jnp.maximum(bl + m_prev, jnp.max(ds_row, axis=-1, keepdims=True))
                a_prev = jnp.exp(bl + m_prev - m_new)
                kw = k * jnp.exp(ds_col - m_new)
                c_s[d, h] = a_prev * c_prev + _dot(kw.T.astype(BF16), vb)
                n_s[d, h:h + 1, :] = a_prev * n_prev + jnp.sum(kw, axis=0, keepdims=True)
                m_s[d, h:h + 1, :] = jnp.broadcast_to(m_new, (1, LANES))
        return carry

    lax.fori_loop(0, nc, step, 0)

    if emit_state:
        co_ref[...] = c_s[...]
        no_ref[...] = n_s[...]
        mo_ref[...] = m_s[...]

    rows_ep = 256
    for r in range(seq // rows_ep):
        rr = slice(r * rows_ep, (r + 1) * rows_ep)
        for h in range(H_B):
            sl = slice(h * DH, (h + 1) * DH)
            hm = hf_s[rr, sl] + hb_s[rr, sl]
            o_ref[rr, sl] = (_sigmoid(ob_ref[rr, sl]) * _rms_lanes(hm, mlg_ref[:, sl])).astype(o_ref.dtype)


def _mlstm(proj, gates, ml_b, ml_g, batch, seq, state=None, emit_state=False):
    t = proj.shape[0]
    col0 = 3 * W_A // W_B
    big = lambda j: pl.BlockSpec((seq, W_B), lambda b: (b, col0 + j))
    in_specs = [big(0), big(1), big(2), big(3),
                pl.BlockSpec((seq, LANES), lambda b: (b, 0)),
                pl.BlockSpec((1, LANES), lambda b: (0, 0)),
                pl.BlockSpec((1, W_B), lambda b: (0, 0))]
    args = [proj, proj, proj, proj, gates, ml_b, ml_g.reshape(1, W_B)]
    c_spec = pl.BlockSpec((None, None, 2, H_B, DH, DH), lambda b: (b, 0, 0, 0, 0, 0))
    n_spec = pl.BlockSpec((None, None, 2, H_B, DH), lambda b: (b, 0, 0, 0, 0))
    if state is not None:
        in_specs += [c_spec, n_spec, n_spec]
        args += list(state)
    out_specs = [pl.BlockSpec((seq, W_B), lambda b: (b, 0))]
    out_shape = [jax.ShapeDtypeStruct((t, W_B), BF16)]
    if emit_state:
        out_specs += [c_spec, n_spec, n_spec]
        out_shape += [jax.ShapeDtypeStruct((batch, 1, 2, H_B, DH, DH), F32),
                      jax.ShapeDtypeStruct((batch, 1, 2, H_B, DH), F32),
                      jax.ShapeDtypeStruct((batch, 1, 2, H_B, LANES), F32)]
    return pl.pallas_call(
        functools.partial(_mlstm_kernel, has_state=state is not None, emit_state=emit_state, seq=seq),
        grid=(batch,),
        in_specs=in_specs,
        out_specs=out_specs,
        out_shape=out_shape,
        scratch_shapes=[
            pltpu.VMEM((seq, LANES), F32),
            pltpu.VMEM((seq, W_B), F32),
            pltpu.VMEM((seq, W_B), F32),
            pltpu.VMEM((2, H_B, DH, DH), F32),
            pltpu.VMEM((2, H_B, DH), F32),
            pltpu.VMEM((2, H_B, LANES), F32),
        ],
        compiler_params=_cparams("parallel"),
        name="mlstm",
    )(*args)


def _rms_halves(x, g2):
    lane = lax.broadcasted_iota(jnp.int32, x.shape, 1)
    low = lane < DQK
    y = x * x
    s_lo = jnp.sum(jnp.where(low, y, 0.0), axis=-1, keepdims=True)
    s_hi = jnp.sum(jnp.where(low, 0.0, y), axis=-1, keepdims=True)
    ms = jnp.where(low, s_lo, s_hi) * (1.0 / DQK)
    return x * lax.rsqrt(ms + EPS) * g2


def _diff_kernel(*refs, heads, seq, past, rope, emit_ctx, lam_init):
    it = iter(refs)
    q_ref, k_ref, v_ref, gq_ref, gk_ref, go_ref, lam_ref = (next(it) for _ in range(7))
    if past:
        kc_ref, vc_ref = next(it), next(it)
    if rope:
        cos_ref, s1_ref, s2_ref = next(it), next(it), next(it)
    o_ref = next(it)
    if emit_ctx:
        ko_ref, vo_ref = next(it), next(it)
    q0_s, q1_s, ka_s, va_s = (next(it) for _ in range(4))

    scale = DQK ** -0.5
    lv = lam_ref[...]
    lam = (jnp.exp(jnp.sum(lv[0:1, :] * lv[1:2, :], axis=-1, keepdims=True))
           - jnp.exp(jnp.sum(lv[2:3, :] * lv[3:4, :], axis=-1, keepdims=True)) + lam_init)
    low = lax.broadcasted_iota(jnp.int32, (seq, LANES), 1) < DQK

    def apply_rope(x):
        return (x * cos_ref[...] + pltpu.roll(x, LANES - DQK // 4, 1) * s1_ref[...]
                + pltpu.roll(x, DQK // 4, 1) * s2_ref[...])

    nqb = seq // Q_BLOCK
    for hh in range(heads):
        sl = slice(hh * DH, (hh + 1) * DH)
        qn = _rms_halves(q_ref[:, sl], gq_ref[...])
        kn = _rms_halves(k_ref[:, sl], gk_ref[...])
        v = v_ref[:, sl]
        if emit_ctx:
            ko_ref[hh] = kn
            vo_ref[hh] = v
        if rope:
            qn = apply_rope(qn)
            kn = apply_rope(kn)
        q0_s[...] = jnp.where(low, qn, 0.0).astype(BF16)
        q1_s[...] = jnp.where(low, 0.0, qn).astype(BF16)
        ka_s[0:seq, :] = kn.astype(BF16)
        va_s[0:seq, :] = v.astype(BF16)
        if past:
            ka_s[seq:seq + past, :] = kc_ref[hh].astype(BF16)
            va_s[seq:seq + past, :] = vc_ref[hh].astype(BF16)

        def qblock(i, carry):
            rows = pl.ds(pl.multiple_of(i * Q_BLOCK, Q_BLOCK), Q_BLOCK)
            ka = ka_s[...]

            def probs(qm):
                s = _dot_nt(qm, ka) * scale
                p = jnp.exp(s - jnp.max(s, axis=-1, keepdims=True))
                return p / jnp.sum(p, axis=-1, keepdims=True)

            a = probs(q0_s[rows, :]) - lam * probs(q1_s[rows, :])
            out = _dot(a.astype(BF16), va_s[...])
            out = _rms_lanes(out, go_ref[:, sl]) * (1.0 - lam_init)
            o_ref[rows, sl] = out.astype(o_ref.dtype)
            return carry

        lax.fori_loop(0, nqb, qblock, 0)


def _diff_attn(proj, gq, gk, g_out, lam_vec, lam_init, batch, seq, heads, cache=None, rope=None,
               emit_ctx=False):
    t = proj.shape[0]
    nh = H_C // heads
    w = heads * DH
    kblk = W_CQK // w
    past = 0 if cache is None else cache[0].shape[3]
    in_specs = [
        pl.BlockSpec((seq, w), lambda b, j: (b, j)),
        pl.BlockSpec((seq, w), lambda b, j: (b, kblk + j)),
        pl.BlockSpec((seq, w), lambda b, j: (b, 2 * kblk + j)),
        pl.BlockSpec((1, DH), lambda b, j: (0, 0)),
        pl.BlockSpec((1, DH), lambda b, j: (0, 0)),
        pl.BlockSpec((1, w), lambda b, j: (0, j)),
        pl.BlockSpec((4, DQK), lambda b, j: (0, 0)),
    ]
    args = [proj, proj, proj, jnp.tile(gq, 2).reshape(1, DH), jnp.tile(gk, 2).reshape(1, DH),
            g_out.reshape(1, W_CV), lam_vec]
    if cache is not None:
        ctx_spec = pl.BlockSpec((None, None, heads, past, DH), lambda b, j: (b, 0, j, 0, 0))
        in_specs += [ctx_spec, ctx_spec]
        args += [cache[0].reshape(cache[0].shape[:4] + (DH,)), cache[1]]
    if rope is not None:
        tab = pl.BlockSpec((seq, DH), lambda b, j: (0, 0))
        in_specs += [tab, tab, tab]
        args += list(rope)
    out_specs = [pl.BlockSpec((seq, w), lambda b, j: (b, j))]
    out_shape = [jax.ShapeDtypeStruct((t, W_CV), BF16)]
    if emit_ctx:
        kv_spec = pl.BlockSpec((None, None, heads, seq, DH), lambda b, j: (b, 0, j, 0, 0))
        kv_shape = jax.ShapeDtypeStruct((batch, 1, H_C, seq, DH), F32)
        out_specs += [kv_spec, kv_spec]
        out_shape += [kv_shape, kv_shape]
    return pl.pallas_call(
        functools.partial(_diff_kernel, heads=heads, seq=seq, past=past, rope=rope is not None,
                          emit_ctx=emit_ctx, lam_init=lam_init),
        grid=(batch, nh),
        in_specs=in_specs,
        out_specs=out_specs,
        out_shape=out_shape,
        scratch_shapes=[
            pltpu.VMEM((seq, DH), BF16),
            pltpu.VMEM((seq, DH), BF16),
            pltpu.VMEM((seq + past, DH), BF16),
            pltpu.VMEM((seq + past, DH), BF16),
        ],
        compiler_params=_cparams("parallel", "arbitrary"),
        name="diff_attn",
    )(*args)


def _rope_tables(seq):
    tpos = jnp.arange(seq)
    pos = jnp.stack([tpos // GRID_W, tpos % GRID_W], axis=-1).astype(F32)
    half = DQK // 2
    inv = ROPE_THETA ** (-jnp.arange(0, half, 2, dtype=F32) / half)
    ang = pos[:, :, None] * inv
    cos = jnp.cos(ang)
    sin = jnp.sin(ang)
    zero = jnp.zeros_like(sin)
    lay = lambda first, second: jnp.tile(jnp.concatenate([first, second], axis=-1).reshape(seq, DQK), (1, 2))
    return lay(cos, cos), lay(-sin, zero), lay(zero, sin)


SGU_TM = 512


def _sgu_kernel(u_ref, vd_ref, sgg_ref, sgw_ref, sgb_ref, o_ref):
    tm = u_ref.shape[0]
    for c in range(tm // SG_CHUNK):
        rows = slice(c * SG_CHUNK, (c + 1) * SG_CHUNK)
        vn = _rms_lanes(_gelu_tanh(vd_ref[rows, :]), sgg_ref[...]).astype(BF16)
        for g in range(G_D):
            sl = slice(g * DH, (g + 1) * DH)
            gate = _dot(sgw_ref[g].astype(BF16), vn[:, sl]) + sgb_ref[:, g:g + 1]
            o_ref[rows, sl] = (_gelu_tanh(u_ref[rows, sl]) * gate).astype(o_ref.dtype)


def _sgu(proj, sg_g, sg_w, sg_b):
    t = proj.shape[0]
    tm = SGU_TM
    ublk = (2 * W_CQK + W_CV) // W_D
    return pl.pallas_call(
        _sgu_kernel,
        grid=(t // tm,),
        in_specs=[
            pl.BlockSpec((tm, W_D), lambda i: (i, ublk)),
            pl.BlockSpec((tm, W_D), lambda i: (i, ublk + 1)),
            pl.BlockSpec((1, W_D), lambda i: (0, 0)),
            pl.BlockSpec((G_D, SG_CHUNK, SG_CHUNK), lambda i: (0, 0, 0)),
            pl.BlockSpec((SG_CHUNK, G_D), lambda i: (0, 0)),
        ],
        out_specs=pl.BlockSpec((tm, W_D), lambda i: (i, 0)),
        out_shape=jax.ShapeDtypeStruct((t, W_D), BF16),
        compiler_params=_cparams("parallel"),
        name="sgu",
    )(proj, proj, sg_g.reshape(1, W_D), sg_w, sg_b.T)


def _lambda_init(layer):
    return 0.8 - 0.6 * math.exp(-0.3 * layer)


def _trunk(x, mods, p, batch, seq, caches):
    ctx_out = {}
    for l in range(DEPTH):
        mod = mods[l]
        if l % 2 == 0:
            proj, gates = _inproj(x, p["g_mix"][l], mod, 1, 0, p["w_even"], p["w_even_gates"])
            if caches is None:
                mix_a, ctx_out["na_k"], ctx_out["na_v"] = _attn_ctx(proj, p["na_gq"], p["na_gk"], batch, seq)
                mix_b, c1, n1, m1 = _mlstm(proj, gates, p["ml_b"], p["ml_g"], batch, seq, emit_state=True)
                ctx_out.update(mlstm_C=c1, mlstm_n=n1, mlstm_m=m1[..., 0])
            else:
                mix_a = _natten(proj, caches["na_k"], caches["na_v"], p["na_gq"], p["na_gk"], p["na_rpb"], batch, seq)
                (mix_b,) = _mlstm(proj, gates, p["ml_b"], p["ml_g"], batch, seq,
                                  state=(caches["C"], caches["n"], caches["m"]))
        else:
            proj = _inproj(x, p["g_mix"][l], mod, 1, 0, p["w_odd"])
            lam_init = _lambda_init(l)
            if caches is None:
                mix_a, dk, dv = _diff_attn(proj, p["diff_gq"], p["diff_gk"], p["diff_g_out"], p["diff_lam"],
                                           lam_init, batch, seq, heads=H_C, emit_ctx=True)
                ctx_out["diff_k"] = dk.reshape(batch, 1, H_C, seq, 2, DQK)
                ctx_out["diff_v"] = dv
            else:
                (mix_a,) = _diff_attn(proj, p["diff_gq"], p["diff_gk"], p["diff_g_out"], p["diff_lam"],
                                      lam_init, batch, seq, heads=1,
                                      cache=(caches["diff_k"], caches["diff_v"]), rope=_rope_tables(seq))
            mix_b = _sgu(proj, p["sg_g"], p["sg_w"], p["sg_b"])
        x = _outproj(mix_a, mix_b, p["w_out"][l], x, mod, 2)
        x = _ffn(x, p["g_ffn"][l], mod, p["w_up"][l], p["conv_w"][l], p["conv_b"][l], p["w_down"][l], seq)
    return x, ctx_out


def kernel(x_prompt, x_sample, cache_na_k, cache_na_v, state_mlstm_C, state_mlstm_n, state_mlstm_m,
           cache_diff_k, cache_diff_v, c, c_ctx, w_mod, b_mod, g_mix, g_ffn, w_out, w_in_even,
           na_gq, na_gk, na_rpb, ml_b_gates, ml_g_out, w_in_odd, diff_gq, diff_gk, diff_lam,
           diff_g_out, sg_g_v, sg_w, sg_b, ffn_w_up, ffn_conv_w, ffn_conv_b, ffn_w_down):
    batch, seq, _ = x_prompt.shape
    dbatch, dseq, _ = x_sample.shape
    n_main = 3 * W_A + 4 * W_B

    cond8 = jnp.zeros((8, D_MODEL), F32).at[0].set(c_ctx).at[1:1 + dbatch].set(c)
    mod = _modulation(cond8, w_mod, b_mod)
    mods_ctx = [mod[l, 0:1].reshape(1, 6, 1, D_MODEL) for l in range(DEPTH)]
    mods_lat = [mod[l, 1:1 + dbatch].reshape(dbatch, 6, 1, D_MODEL) for l in range(DEPTH)]

    w_even = w_in_even[0]
    p = dict(
        g_mix=g_mix, g_ffn=g_ffn,
        w_even=w_even[:, :n_main].astype(BF16),
        w_even_gates=jnp.pad(w_even[:, n_main:], ((0, 0), (0, LANES - N_GATES))).astype(BF16),
        w_odd=w_in_odd[0].astype(BF16),
        w_out=w_out.astype(BF16),
        w_up=ffn_w_up.astype(BF16), w_down=ffn_w_down.astype(BF16),
        conv_w=ffn_conv_w, conv_b=ffn_conv_b,
        na_gq=na_gq[0], na_gk=na_gk[0], na_rpb=na_rpb[0],
        ml_b=jnp.pad(ml_b_gates[0], (0, LANES - N_GATES)).reshape(1, LANES), ml_g=ml_g_out[0],
        diff_gq=diff_gq[0], diff_gk=diff_gk[0], diff_lam=diff_lam[0], diff_g_out=diff_g_out[0],
        sg_g=sg_g_v[0], sg_w=sg_w[0], sg_b=sg_b[0],
    )

    y_prompt, ctx = _trunk(x_prompt.reshape(batch * seq, D_MODEL), mods_ctx, p, batch, seq, None)
    caches = dict(
        na_k=cache_na_k, na_v=cache_na_v, C=state_mlstm_C, n=state_mlstm_n,
        m=jnp.broadcast_to(state_mlstm_m[..., None], state_mlstm_m.shape + (LANES,)),
        diff_k=cache_diff_k, diff_v=cache_diff_v,
    )
    y_sample, _ = _trunk(x_sample.reshape(dbatch * dseq, D_MODEL), mods_lat, p, dbatch, dseq, caches)
    return (y_prompt.reshape(batch, seq, D_MODEL), y_sample.reshape(dbatch, dseq, D_MODEL),
            ctx["na_k"], ctx["na_v"], ctx["mlstm_C"], ctx["mlstm_n"], ctx["mlstm_m"],
            ctx["diff_k"], ctx["diff_v"])
```

```python
import functools
import math

import numpy as np
import jax
import jax.numpy as jnp
from jax import lax
from jax.experimental import pallas as pl
from jax.experimental.pallas import tpu as pltpu

F32 = jnp.float32
BF16 = jnp.bfloat16

D_MODEL = 2048
DEPTH = 2
GRID_W = 64
DH = 128
H_A = 8
W_A = H_A * DH
WIN_H = 8
WIN_W = 16
H_B = 8
W_B = H_B * DH
ML_CHUNK = 128
H_C = 8
DQK = 64
W_CQK = H_C * 2 * DQK
W_CV = H_C * DH
ROPE_THETA = 10000.0
G_D = 8
SG_CHUNK = 128
W_D = 1024
D_FF = 5632
Q_BLOCK = 128
EPS = 1e-6
N_GATES = 4 * H_B

LANES = 128
V7X_VMEM_BYTES = 64 * 1024 * 1024
VMEM_LIMIT = V7X_VMEM_BYTES - 8 * 1024 * 1024

NEG_INF = float("-inf")


def _cparams(*sem):
    return pltpu.CompilerParams(dimension_semantics=sem, vmem_limit_bytes=VMEM_LIMIT)


def _dot(a, b):
    return jnp.dot(a, b, preferred_element_type=F32)


def _dot_nt(a, b):
    return lax.dot_general(a, b, (((1,), (1,)), ((), ())), preferred_element_type=F32)


def _rms_lanes(x, g):
    ms = jnp.mean(x * x, axis=-1, keepdims=True)
    return x * lax.rsqrt(ms + EPS) * g


def _norm_mod(x, g, scale, shift):
    return _rms_lanes(x, g) * (1.0 + scale) + shift


def _sigmoid(x):
    return 1.0 / (1.0 + jnp.exp(-x))


def _gelu_tanh(x):
    c = math.sqrt(2.0 / math.pi)
    return 0.5 * x * (1.0 + jnp.tanh(c * (x + 0.044715 * (x * x * x))))


MOD_TN = 1024


def _mod_kernel(c_ref, w_ref, b_ref, o_ref):
    c = c_ref[...]
    s = c * _sigmoid(c)
    o_ref[...] = _dot(s.astype(BF16), w_ref[...].astype(BF16)) + b_ref[...]


def _modulation(cond8, w_mod, b_mod):
    n = 6 * D_MODEL
    return pl.pallas_call(
        _mod_kernel,
        grid=(DEPTH, n // MOD_TN),
        in_specs=[
            pl.BlockSpec((8, D_MODEL), lambda l, j: (0, 0)),
            pl.BlockSpec((None, D_MODEL, MOD_TN), lambda l, j: (l, 0, j)),
            pl.BlockSpec((None, 1, MOD_TN), lambda l, j: (l, 0, j)),
        ],
        out_specs=pl.BlockSpec((None, 8, MOD_TN), lambda l, j: (l, 0, j)),
        out_shape=jax.ShapeDtypeStruct((DEPTH, 8, n), F32),
        compiler_params=_cparams("parallel", "arbitrary"),
        name="modulation",
    )(cond8, w_mod, b_mod.reshape(DEPTH, 1, n))


DENSE_TM = 1024
NORM_ROWS = 256
PROJ_DTYPE = BF16


def _fill_normed(x_ref, g_ref, sc_ref, sh_ref, hb_ref):
    tm = x_ref.shape[0]

    def body(c, carry):
        r = pl.ds(pl.multiple_of(c * NORM_ROWS, NORM_ROWS), NORM_ROWS)
        h = _norm_mod(x_ref[r, :], g_ref[...], sc_ref[...], sh_ref[...])
        hb_ref[r, :] = h.astype(BF16)
        return carry

    lax.fori_loop(0, tm // NORM_ROWS, body, 0)


def _inproj_kernel(x_ref, g_ref, sc_ref, sh_ref, w_ref, o_ref, hb_ref):
    @pl.when(pl.program_id(1) == 0)
    def _():
        _fill_normed(x_ref, g_ref, sc_ref, sh_ref, hb_ref)

    o_ref[...] = _dot(hb_ref[...], w_ref[...]).astype(o_ref.dtype)


def _inproj_gates_kernel(x_ref, g_ref, sc_ref, sh_ref, w_ref, wg_ref, o_ref, og_ref, hb_ref):
    @pl.when(pl.program_id(1) == 0)
    def _():
        _fill_normed(x_ref, g_ref, sc_ref, sh_ref, hb_ref)
        og_ref[...] = _dot(hb_ref[...], wg_ref[...])

    o_ref[...] = _dot(hb_ref[...], w_ref[...]).astype(o_ref.dtype)


def _mod_spec(idx, tm, rows_per_mod):
    return pl.BlockSpec((None, None, 1, D_MODEL), lambda i, n: ((i * tm) // rows_per_mod, idx, 0, 0))


def _inproj(x, g, mod, scale_idx, shift_idx, w, w_gates=None, tn=1024):
    t = x.shape[0]
    tm = DENSE_TM
    n = w.shape[1]
    rows_per_mod = t // mod.shape[0]
    in_specs = [
        pl.BlockSpec((tm, D_MODEL), lambda i, j: (i, 0)),
        pl.BlockSpec((1, D_MODEL), lambda i, j: (0, 0)),
        _mod_spec(scale_idx, tm, rows_per_mod),
        _mod_spec(shift_idx, tm, rows_per_mod),
        pl.BlockSpec((D_MODEL, tn), lambda i, j: (0, j)),
    ]
    out_specs = pl.BlockSpec((tm, tn), lambda i, j: (i, j))
    out_shape = jax.ShapeDtypeStruct((t, n), PROJ_DTYPE)
    args = [x, g.reshape(1, D_MODEL), mod, mod, w]
    kern = _inproj_kernel
    if w_gates is not None:
        ng = w_gates.shape[1]
        in_specs.append(pl.BlockSpec((D_MODEL, ng), lambda i, j: (0, 0)))
        out_specs = [out_specs, pl.BlockSpec((tm, ng), lambda i, j: (i, 0))]
        out_shape = [out_shape, jax.ShapeDtypeStruct((t, ng), F32)]
        args.append(w_gates)
        kern = _inproj_gates_kernel
    return pl.pallas_call(
        kern,
        grid=(t // tm, n // tn),
        in_specs=in_specs,
        out_specs=out_specs,
        out_shape=out_shape,
        scratch_shapes=[pltpu.VMEM((tm, D_MODEL), BF16)],
        compiler_params=_cparams("parallel", "arbitrary"),
        name="inproj",
    )(*args)


OUT_TN = 512


def _outproj_kernel(a_ref, b_ref, wa_ref, wb_ref, x_ref, gate_ref, o_ref):
    acc = _dot(a_ref[...], wa_ref[...]) + _dot(b_ref[...], wb_ref[...])
    o_ref[...] = x_ref[...] + gate_ref[...] * acc


def _outproj(mix_a, mix_b, w, x, mod, gate_idx):
    t = x.shape[0]
    tm, tn = DENSE_TM, OUT_TN
    half = mix_a.shape[1]
    rows_per_mod = t // mod.shape[0]
    return pl.pallas_call(
        _outproj_kernel,
        grid=(t // tm, D_MODEL // tn),
        in_specs=[
            pl.BlockSpec((tm, half), lambda i, j: (i, 0)),
            pl.BlockSpec((tm, half), lambda i, j: (i, 0)),
            pl.BlockSpec((half, tn), lambda i, j: (0, j)),
            pl.BlockSpec((half, tn), lambda i, j: (1, j)),
            pl.BlockSpec((tm, tn), lambda i, j: (i, j)),
            pl.BlockSpec((None, None, 1, tn), lambda i, j: ((i * tm) // rows_per_mod, gate_idx, 0, j)),
        ],
        out_specs=pl.BlockSpec((tm, tn), lambda i, j: (i, j)),
        out_shape=jax.ShapeDtypeStruct((t, D_MODEL), F32),
        compiler_params=_cparams("parallel", "arbitrary"),
        name="outproj",
    )(mix_a, mix_b, w, w, x, mod)


FFN_TF = 512


def _ffn_kernel(x_ref, g_ref, sc_ref, sh_ref, gate_ref, wg_ref, wv_ref, cwg_ref, cwv_ref, cbg_ref, cbv_ref,
                wd_ref, o_ref, hb_ref, *, seq_len):
    f = pl.program_id(1)
    tm = x_ref.shape[0]

    @pl.when(f == 0)
    def _():
        _fill_normed(x_ref, g_ref, sc_ref, sh_ref, hb_ref)
        o_ref[...] = jnp.zeros_like(o_ref)

    pos = lax.broadcasted_iota(jnp.int32, (tm, 1), 0) & (seq_len - 1)
    first = pos == 0
    last = pos == seq_len - 1

    def conv(a, cw_ref, cb_ref):
        prev = jnp.where(first, 0.0, pltpu.roll(a, 1, 0))
        nxt = jnp.where(last, 0.0, pltpu.roll(a, tm - 1, 0))
        return prev * cw_ref[0:1, :] + a * cw_ref[1:2, :] + nxt * cw_ref[2:3, :] + cb_ref[...]

    hb = hb_ref[...]
    cg = conv(_dot(hb, wg_ref[...]), cwg_ref, cbg_ref)
    cv = conv(_dot(hb, wv_ref[...]), cwv_ref, cbv_ref)
    act = (cg * _sigmoid(cg) * cv).astype(BF16)
    o_ref[...] += _dot(act, wd_ref[...])

    @pl.when(f == pl.num_programs(1) - 1)
    def _():
        o_ref[...] = x_ref[...] + gate_ref[...] * o_ref[...]


def _ffn(x, g, mod, w_up, conv_w, conv_b, w_down, seq_len):
    t = x.shape[0]
    tm, tf = DENSE_TM, FFN_TF
    nf = D_FF // tf
    rows_per_mod = t // mod.shape[0]
    conv_b = conv_b.reshape(1, 2 * D_FF)
    assert seq_len & (seq_len - 1) == 0 and tm % seq_len == 0
    return pl.pallas_call(
        functools.partial(_ffn_kernel, seq_len=seq_len),
        grid=(t // tm, nf),
        in_specs=[
            pl.BlockSpec((tm, D_MODEL), lambda i, f: (i, 0), pipeline_mode=pl.Buffered(1)),
            pl.BlockSpec((1, D_MODEL), lambda i, f: (0, 0)),
            _mod_spec(4, tm, rows_per_mod),
            _mod_spec(3, tm, rows_per_mod),
            _mod_spec(5, tm, rows_per_mod),
            pl.BlockSpec((D_MODEL, tf), lambda i, f: (0, f)),
            pl.BlockSpec((D_MODEL, tf), lambda i, f: (0, nf + f)),
            pl.BlockSpec((3, tf), lambda i, f: (0, f)),
            pl.BlockSpec((3, tf), lambda i, f: (0, nf + f)),
            pl.BlockSpec((1, tf), lambda i, f: (0, f)),
            pl.BlockSpec((1, tf), lambda i, f: (0, nf + f)),
            pl.BlockSpec((tf, D_MODEL), lambda i, f: (f, 0)),
        ],
        out_specs=pl.BlockSpec((tm, D_MODEL), lambda i, f: (i, 0)),
        out_shape=jax.ShapeDtypeStruct((t, D_MODEL), F32),
        scratch_shapes=[pltpu.VMEM((tm, D_MODEL), BF16)],
        compiler_params=_cparams("parallel", "arbitrary"),
        name="convffn",
    )(x, g.reshape(1, D_MODEL), mod, mod, mod, w_up, w_up, conv_w, conv_w, conv_b, conv_b, w_down)


def _attn_ctx_kernel(q_ref, k_ref, v_ref, gq_ref, gk_ref, o_ref, ko_ref, vo_ref):
    scale = DH ** -0.5
    for h in range(H_A):
        sl = slice(h * DH, (h + 1) * DH)
        q = _rms_lanes(q_ref[:, sl].astype(F32), gq_ref[...])
        k = _rms_lanes(k_ref[:, sl].astype(F32), gk_ref[...])
        v = v_ref[:, sl].astype(F32)
        ko_ref[h] = k
        vo_ref[h] = v
        s = _dot_nt(q.astype(BF16), k.astype(BF16)) * scale
        p = jnp.exp(s - jnp.max(s, axis=-1, keepdims=True))
        p = p / jnp.sum(p, axis=-1, keepdims=True)
        o_ref[:, sl] = _dot(p.astype(BF16), v.astype(BF16)).astype(o_ref.dtype)


def _attn_ctx(proj, gq, gk, batch, seq):
    t = proj.shape[0]
    kv_shape = jax.ShapeDtypeStruct((batch, 1, H_A, seq, DH), F32)
    kv_spec = pl.BlockSpec((None, None, H_A, seq, DH), lambda b: (b, 0, 0, 0, 0))
    return pl.pallas_call(
        _attn_ctx_kernel,
        grid=(batch,),
        in_specs=[
            pl.BlockSpec((seq, W_A), lambda b: (b, 0)),
            pl.BlockSpec((seq, W_A), lambda b: (b, 1)),
            pl.BlockSpec((seq, W_A), lambda b: (b, 2)),
            pl.BlockSpec((1, DH), lambda b: (0, 0)),
            pl.BlockSpec((1, DH), lambda b: (0, 0)),
        ],
        out_specs=[pl.BlockSpec((seq, W_A), lambda b: (b, 0)), kv_spec, kv_spec],
        out_shape=[jax.ShapeDtypeStruct((t, W_A), BF16), kv_shape, kv_shape],
        compiler_params=_cparams("parallel"),
        name="attn_ctx",
    )(proj, proj, proj, gq.reshape(1, DH), gk.reshape(1, DH))


NA_ROWS = 16
NA_PAIRS = 2 * WIN_H - 2
RPB_H = 2 * WIN_H - 1
RPB_W = 2 * WIN_W - 1


def _natten_kernel(rpb_ref, q_ref, k_ref, v_ref, kc_ref, vc_ref, gq_ref, gk_ref, o_ref,
                   qn_s, kn_s, v_s, kc_s, vc_s, bias_s):
    h = pl.program_id(1)
    scale = DH ** -0.5
    qn_s[...] = _rms_lanes(q_ref[...].astype(F32), gq_ref[...]).astype(BF16)
    kn_s[...] = _rms_lanes(k_ref[...].astype(F32), gk_ref[...]).astype(BF16)
    v_s[...] = v_ref[...].astype(BF16)
    kc_s[...] = kc_ref[...].astype(BF16)
    vc_s[...] = vc_ref[...].astype(BF16)

    lane = lax.broadcasted_iota(jnp.int32, (GRID_W, LANES), 1)
    qc = lax.broadcasted_iota(jnp.int32, (GRID_W, LANES), 0)
    kcol = lane % GRID_W
    upper = lane >= GRID_W
    dcol = kcol - qc + (WIN_W - 1)
    col0 = jnp.clip(qc - WIN_W // 2, 0, GRID_W - WIN_W)
    col_ok = (kcol >= col0) & (kcol < col0 + WIN_W)
    base = h * (RPB_H * RPB_W)

    def build(p, carry):
        acc = jnp.zeros((GRID_W, LANES), F32)
        for d in range(RPB_W):
            lo = rpb_ref[base + p * RPB_W + d]
            hi = rpb_ref[base + (p + 1) * RPB_W + d]
            acc = jnp.where(dcol == d, jnp.where(upper, hi, lo), acc)
        bias_s[p] = jnp.where(col_ok, acc, NEG_INF)
        return carry

    lax.fori_loop(0, NA_PAIRS, build, 0)

    kh = WIN_H
    n_loc = kh * GRID_W
    for r in range(NA_ROWS):
        row0 = min(max(r - kh // 2, 0), NA_ROWS - kh)
        dr0 = row0 - r + WIN_H - 1
        q = qn_s[r * GRID_W:(r + 1) * GRID_W, :]
        k_loc = kn_s[row0 * GRID_W:row0 * GRID_W + n_loc, :]
        v_loc = v_s[row0 * GRID_W:row0 * GRID_W + n_loc, :]
        bias = jnp.concatenate([bias_s[dr0 + 2 * i] for i in range(kh // 2)], axis=-1)
        s_loc = _dot_nt(q, k_loc) * scale + bias
        s_ctx = _dot_nt(q, kc_s[...]) * scale
        m = jnp.maximum(jnp.max(s_loc, axis=-1, keepdims=True), jnp.max(s_ctx, axis=-1, keepdims=True))
        p_loc = jnp.exp(s_loc - m)
        p_ctx = jnp.exp(s_ctx - m)
        l = jnp.sum(p_loc, axis=-1, keepdims=True) + jnp.sum(p_ctx, axis=-1, keepdims=True)
        out = _dot((p_loc / l).astype(BF16), v_loc) + _dot((p_ctx / l).astype(BF16), vc_s[...])
        o_ref[r * GRID_W:(r + 1) * GRID_W, :] = out.astype(o_ref.dtype)


def _natten(proj, cache_k, cache_v, gq, gk, rpb, batch, seq):
    t = proj.shape[0]
    past = cache_k.shape[3]
    ctx_spec = pl.BlockSpec((None, None, None, past, DH), lambda b, h: (b, 0, h, 0, 0))
    return pl.pallas_call(
        _natten_kernel,
        grid=(batch, H_A),
        in_specs=[
            pl.BlockSpec(memory_space=pltpu.SMEM),
            pl.BlockSpec((seq, DH), lambda b, h: (b, h)),
            pl.BlockSpec((seq, DH), lambda b, h: (b, H_A + h)),
            pl.BlockSpec((seq, DH), lambda b, h: (b, 2 * H_A + h)),
            ctx_spec,
            ctx_spec,
            pl.BlockSpec((1, DH), lambda b, h: (0, 0)),
            pl.BlockSpec((1, DH), lambda b, h: (0, 0)),
        ],
        out_specs=pl.BlockSpec((seq, DH), lambda b, h: (b, h)),
        out_shape=jax.ShapeDtypeStruct((t, W_A), BF16),
        scratch_shapes=[
            pltpu.VMEM((seq, DH), BF16),
            pltpu.VMEM((seq, DH), BF16),
            pltpu.VMEM((seq, DH), BF16),
            pltpu.VMEM((past, DH), BF16),
            pltpu.VMEM((past, DH), BF16),
            pltpu.VMEM((NA_PAIRS, GRID_W, LANES), F32),
        ],
        compiler_params=_cparams("parallel", "arbitrary"),
        name="natten",
    )(rpb.reshape(-1), proj, proj, proj, cache_k, cache_v, gq.reshape(1, DH), gk.reshape(1, DH))


def _split3(x):
    hi = x.astype(BF16)
    r1 = x - hi.astype(F32)
    mid = r1.astype(BF16)
    lo = (r1 - mid.astype(F32)).astype(BF16)
    return hi, mid, lo


def _mlstm_kernel(*refs, has_state, emit_state, seq):
    it = iter(refs)
    q_ref, k_ref, v_ref, ob_ref, gt_ref, mlb_ref, mlg_ref = (next(it) for _ in range(7))
    if has_state:
        c0_ref, n0_ref, m0_ref = (next(it) for _ in range(3))
    o_ref = next(it)
    if emit_state:
        co_ref, no_ref, mo_ref = (next(it) for _ in range(3))
    r_s, rt_s, bt_s, hf_s, hb_s, ct_s, n_s, m_s = (next(it) for _ in range(8))

    L = ML_CHUNK
    nc = seq // L
    kscale = DH ** -0.5

    ri = lax.broadcasted_iota(jnp.int32, (L, L), 0)
    ci = lax.broadcasted_iota(jnp.int32, (L, L), 1)
    tri_pre = (ci <= ri).astype(BF16)
    tri_suf = (ci >= ri).astype(BF16)
    mask_t = (ri <= ci, ri >= ci)
    fwd_lane = lax.broadcasted_iota(jnp.int32, (L, LANES), 1) < H_B

    for c in range(nc):
        rows = slice(c * L, (c + 1) * L)
        gi = gt_ref[rows, 0:LANES] + mlb_ref[:, 0:LANES]
        gf = gt_ref[rows, LANES:2 * LANES] + mlb_ref[:, LANES:2 * LANES]
        gf = jnp.minimum(gf, 0.0) - jnp.log1p(jnp.exp(-jnp.abs(gf)))
        hi, mid, lo = _split3(gf)
        pre = _dot(tri_pre, hi) + _dot(tri_pre, mid) + _dot(tri_pre, lo)
        suf = _dot(tri_suf, hi) + _dot(tri_suf, mid) + _dot(tri_suf, lo)
        b = jnp.where(fwd_lane, pre, suf)
        r = gi - b
        r_s[rows, :] = r
        rt_s[c] = r.T
        bt_s[c] = b.T

    for d in range(2):
        for h in range(H_B):
            ct_s[d, h] = c0_ref[d, h].T if has_state else jnp.zeros((DH, DH), F32)
    if has_state:
        n_s[...] = n0_ref[...]
        m_s[...] = m0_ref[...]
    else:
        n_s[...] = jnp.zeros_like(n_s)
        m_s[...] = jnp.zeros_like(m_s)

    def step(c, carry):
        for d in range(2):
            cc = c if d == 0 else nc - 1 - c
            rows = pl.ds(pl.multiple_of(cc * L, L), L)
            r_all = r_s[rows, :]
            rt_all = rt_s[cc]
            bt_all = bt_s[cc]
            h_dst = hf_s if d == 0 else hb_s
            for h in range(H_B):
                col = d * H_B + h
                sl = slice(h * DH, (h + 1) * DH)
                r_row = rt_all[col:col + 1, :]
                b_row = bt_all[col:col + 1, :]
                rm = jnp.where(mask_t[d], jnp.broadcast_to(r_all[:, col:col + 1], (L, L)), NEG_INF)
                a_row = jnp.max(rm, axis=0, keepdims=True)
                r_max = jnp.max(r_row, axis=-1, keepdims=True)
                bl = b_row[:, L - 1:L] if d == 0 else b_row[:, 0:1]
                m_row = m_s[d, h:h + 1, :]
                ct = ct_s[d, h]
                n_row = n_s[d, h:h + 1, :]

                qb = q_ref[rows, sl].astype(BF16)
                kb = (k_ref[rows, sl].astype(F32) * kscale).astype(BF16)
                v_t = v_ref[rows, sl].astype(F32).T

                big_m = jnp.maximum(m_row, a_row)
                w_t = jnp.exp(rm - big_m) * _dot_nt(kb, qb)
                w_in = jnp.exp(m_row - big_m)
                num_t = w_in * _dot_nt(ct.astype(BF16), qb) + _dot(v_t.astype(BF16), w_t.astype(BF16))
                qn = _dot_nt(jnp.broadcast_to(n_row, (8, DH)).astype(BF16), qb)[0:1, :]
                den = w_in * qn + jnp.sum(w_t, axis=0, keepdims=True)
                h_t = num_t / jnp.maximum(jnp.abs(den), jnp.exp(-(b_row + big_m)))
                h_dst[rows, sl] = h_t.T

                m_last = jnp.maximum(m_row, r_max)
                ws = jnp.exp(r_row - m_last)
                a_prev = jnp.exp(m_row - m_last)
                ct_s[d, h] = a_prev * ct + _dot((v_t * ws).astype(BF16), kb)
                n_s[d, h:h + 1, :] = a_prev * n_row + _dot(jnp.broadcast_to(ws, (8, L)).astype(BF16), kb)[0:1, :]
                m_s[d, h:h + 1, :] = bl + m_last
        return carry

    lax.fori_loop(0, nc, step, 0)

    if emit_state:
        for d in range(2):
            for h in range(H_B):
                co_ref[d, h] = ct_s[d, h].T
        no_ref[...] = n_s[...]
        mo_ref[...] = m_s[...]

    rows_ep = 256
    for r in range(seq // rows_ep):
        rr = slice(r * rows_ep, (r + 1) * rows_ep)
        for h in range(H_B):
            sl = slice(h * DH, (h + 1) * DH)
            hm = hf_s[rr, sl] + hb_s[rr, sl]
            ob = ob_ref[rr, sl].astype(F32)
            o_ref[rr, sl] = (_sigmoid(ob) * _rms_lanes(hm, mlg_ref[:, sl])).astype(o_ref.dtype)


def _mlstm(proj, gates, ml_b, ml_g, batch, seq, state=None, emit_state=False):
    t = proj.shape[0]
    col0 = 3 * W_A // W_B
    big = lambda j: pl.BlockSpec((seq, W_B), lambda b: (b, col0 + j))
    in_specs = [big(0), big(1), big(2), big(3),
                pl.BlockSpec((seq, 2 * LANES), lambda b: (b, 0)),
                pl.BlockSpec((1, 2 * LANES), lambda b: (0, 0)),
                pl.BlockSpec((1, W_B), lambda b: (0, 0))]
    args = [proj, proj, proj, proj, gates, ml_b, ml_g.reshape(1, W_B)]
    c_spec = pl.BlockSpec((None, None, 2, H_B, DH, DH), lambda b: (b, 0, 0, 0, 0, 0))
    n_spec = pl.BlockSpec((None, None, 2, H_B, DH), lambda b: (b, 0, 0, 0, 0))
    if state is not None:
        in_specs += [c_spec, n_spec, n_spec]
        args += list(state)
    out_specs = [pl.BlockSpec((seq, W_B), lambda b: (b, 0))]
    out_shape = [jax.ShapeDtypeStruct((t, W_B), BF16)]
    if emit_state:
        out_specs += [c_spec, n_spec, n_spec]
        out_shape += [jax.ShapeDtypeStruct((batch, 1, 2, H_B, DH, DH), F32),
                      jax.ShapeDtypeStruct((batch, 1, 2, H_B, DH), F32),
                      jax.ShapeDtypeStruct((batch, 1, 2, H_B, LANES), F32)]
    nc = seq // ML_CHUNK
    return pl.pallas_call(
        functools.partial(_mlstm_kernel, has_state=state is not None, emit_state=emit_state, seq=seq),
        grid=(batch,),
        in_specs=in_specs,
        out_specs=out_specs,
        out_shape=out_shape,
        scratch_shapes=[
            pltpu.VMEM((seq, LANES), F32),
            pltpu.VMEM((nc, ML_CHUNK, LANES), F32),
            pltpu.VMEM((nc, ML_CHUNK, LANES), F32),
            pltpu.VMEM((seq, W_B), F32),
            pltpu.VMEM((seq, W_B), F32),
            pltpu.VMEM((2, H_B, DH, DH), F32),
            pltpu.VMEM((2, H_B, DH), F32),
            pltpu.VMEM((2, H_B, LANES), F32),
        ],
        compiler_params=_cparams("parallel"),
        name="mlstm",
    )(*args)


DIFF_QB = 256


def _rms_halves(x, g2, halves):
    y = x * x
    hi = y.astype(BF16)
    mid = (y - hi.astype(F32)).astype(BF16)
    ss = _dot(hi, halves) + _dot(mid, halves)
    return x * lax.rsqrt(ss * (1.0 / DQK) + EPS) * g2


def _diff_kernel(*refs, heads, seq, past, rope, emit_ctx, lam_init):
    it = iter(refs)
    q_ref, k_ref, v_ref, gq_ref, gk_ref, go_ref, lam_ref = (next(it) for _ in range(7))
    if past:
        kc_ref, vc_ref = next(it), next(it)
    if rope:
        cos_ref, s1_ref, s2_ref = next(it), next(it), next(it)
    o_ref = next(it)
    if emit_ctx:
        ko_ref, vo_ref = next(it), next(it)

    scale = DQK ** -0.5
    lv = lam_ref[...]
    lam = (jnp.exp(jnp.sum(lv[0:1, :] * lv[1:2, :], axis=-1, keepdims=True))
           - jnp.exp(jnp.sum(lv[2:3, :] * lv[3:4, :], axis=-1, keepdims=True)) + lam_init)
    low = lax.broadcasted_iota(jnp.int32, (seq, LANES), 1) < DQK
    hr = lax.broadcasted_iota(jnp.int32, (LANES, LANES), 0) < DQK
    hc = lax.broadcasted_iota(jnp.int32, (LANES, LANES), 1) < DQK
    halves = (hr == hc).astype(BF16)

    def apply_rope(x):
        return (x * cos_ref[...] + pltpu.roll(x, LANES - DQK // 4, 1) * s1_ref[...]
                + pltpu.roll(x, DQK // 4, 1) * s2_ref[...])

    qb_rows = min(DIFF_QB, seq)
    for hh in range(heads):
        sl = slice(hh * DH, (hh + 1) * DH)
        qn = _rms_halves(q_ref[:, sl].astype(F32), gq_ref[...], halves)
        kn = _rms_halves(k_ref[:, sl].astype(F32), gk_ref[...], halves)
        v = v_ref[:, sl].astype(F32)
        if emit_ctx:
            ko_ref[hh] = kn
            vo_ref[hh] = v
        if rope:
            qn = apply_rope(qn)
            kn = apply_rope(kn)
        q0 = jnp.where(low, qn, 0.0).astype(BF16)
        q1 = jnp.where(low, 0.0, qn).astype(BF16)
        k_parts = [kn.astype(BF16)]
        vt_parts = [v[c * LANES:(c + 1) * LANES, :].T.astype(BF16) for c in range(seq // LANES)]
        if past:
            k_parts.append(kc_ref[hh].astype(BF16))
            vt_parts += [vc_ref[hh, c * LANES:(c + 1) * LANES, :].T.astype(BF16) for c in range(past // LANES)]
        ka = jnp.concatenate(k_parts, axis=0)
        vt = jnp.concatenate(vt_parts, axis=1)

        def probs_t(qm):
            s = _dot_nt(ka, qm) * scale
            p = jnp.exp(s - jnp.max(s, axis=0, keepdims=True))
            return p / jnp.sum(p, axis=0, keepdims=True)

        g_row = go_ref[:, sl] * (1.0 - lam_init)
        for i in range(seq // qb_rows):
            rows = slice(i * qb_rows, (i + 1) * qb_rows)
            a_t = probs_t(q0[rows, :]) - lam * probs_t(q1[rows, :])
            out_t = _dot(vt, a_t.astype(BF16))
            ms = jnp.mean(out_t * out_t, axis=0, keepdims=True)
            out_t = out_t * lax.rsqrt(ms + EPS)
            for c in range(qb_rows // LANES):
                r0 = i * qb_rows + c * LANES
                o_ref[r0:r0 + LANES, sl] = (out_t[:, c * LANES:(c + 1) * LANES].T * g_row).astype(o_ref.dtype)


def _diff_attn(proj, gq, gk, g_out, lam_vec, lam_init, batch, seq, heads, cache=None, rope=None,
               emit_ctx=False):
    t = proj.shape[0]
    nh = H_C // heads
    w = heads * DH
    kblk = W_CQK // w
    past = 0 if cache is None else cache[0].shape[3]
    in_specs = [
        pl.BlockSpec((seq, w), lambda b, j: (b, j)),
        pl.BlockSpec((seq, w), lambda b, j: (b, kblk + j)),
        pl.BlockSpec((seq, w), lambda b, j: (b, 2 * kblk + j)),
        pl.BlockSpec((1, DH), lambda b, j: (0, 0)),
        pl.BlockSpec((1, DH), lambda b, j: (0, 0)),
        pl.BlockSpec((1, w), lambda b, j: (0, j)),
        pl.BlockSpec((4, DQK), lambda b, j: (0, 0)),
    ]
    args = [proj, proj, proj, jnp.tile(gq, 2).reshape(1, DH), jnp.tile(gk, 2).reshape(1, DH),
            g_out.reshape(1, W_CV), lam_vec]
    if cache is not None:
        ctx_spec = pl.BlockSpec((None, None, heads, past, DH), lambda b, j: (b, 0, j, 0, 0))
        in_specs += [ctx_spec, ctx_spec]
        args += [cache[0].reshape(cache[0].shape[:4] + (DH,)), cache[1]]
    if rope is not None:
        tab = pl.BlockSpec((seq, DH), lambda b, j: (0, 0))
        in_specs += [tab, tab, tab]
        args += list(rope)
    out_specs = [pl.BlockSpec((seq, w), lambda b, j: (b, j))]
    out_shape = [jax.ShapeDtypeStruct((t, W_CV), BF16)]
    if emit_ctx:
        kv_spec = pl.BlockSpec((None, None, heads, seq, DH), lambda b, j: (b, 0, j, 0, 0))
        kv_shape = jax.ShapeDtypeStruct((batch, 1, H_C, seq, DH), F32)
        out_specs += [kv_spec, kv_spec]
        out_shape += [kv_shape, kv_shape]
    return pl.pallas_call(
        functools.partial(_diff_kernel, heads=heads, seq=seq, past=past, rope=rope is not None,
                          emit_ctx=emit_ctx, lam_init=lam_init),
        grid=(batch, nh),
        in_specs=in_specs,
        out_specs=out_specs,
        out_shape=out_shape,
        compiler_params=_cparams("parallel", "arbitrary"),
        name="diff_attn",
    )(*args)


def _rope_tables(seq):
    tpos = jnp.arange(seq)
    pos = jnp.stack([tpos // GRID_W, tpos % GRID_W], axis=-1).astype(F32)
    half = DQK // 2
    inv = ROPE_THETA ** (-jnp.arange(0, half, 2, dtype=F32) / half)
    ang = pos[:, :, None] * inv
    cos = jnp.cos(ang)
    sin = jnp.sin(ang)
    zero = jnp.zeros_like(sin)
    lay = lambda first, second: jnp.tile(jnp.concatenate([first, second], axis=-1).reshape(seq, DQK), (1, 2))
    return lay(cos, cos), lay(-sin, zero), lay(zero, sin)


SGU_TM = 512


def _sgu_kernel(u_ref, vd_ref, sgg_ref, sgw_ref, sgb_ref, o_ref):
    tm = u_ref.shape[0]
    for c in range(tm // SG_CHUNK):
        rows = slice(c * SG_CHUNK, (c + 1) * SG_CHUNK)
        vn = _rms_lanes(_gelu_tanh(vd_ref[rows, :].astype(F32)), sgg_ref[...]).astype(BF16)
        for g in range(G_D):
            sl = slice(g * DH, (g + 1) * DH)
            gate = _dot(sgw_ref[g].astype(BF16), vn[:, sl]) + sgb_ref[:, g:g + 1]
            o_ref[rows, sl] = (_gelu_tanh(u_ref[rows, sl].astype(F32)) * gate).astype(o_ref.dtype)


def _sgu(proj, sg_g, sg_w, sg_b):
    t = proj.shape[0]
    tm = SGU_TM
    ublk = (2 * W_CQK + W_CV) // W_D
    return pl.pallas_call(
        _sgu_kernel,
        grid=(t // tm,),
        in_specs=[
            pl.BlockSpec((tm, W_D), lambda i: (i, ublk)),
            pl.BlockSpec((tm, W_D), lambda i: (i, ublk + 1)),
            pl.BlockSpec((1, W_D), lambda i: (0, 0)),
            pl.BlockSpec((G_D, SG_CHUNK, SG_CHUNK), lambda i: (0, 0, 0)),
            pl.BlockSpec((SG_CHUNK, G_D), lambda i: (0, 0)),
        ],
        out_specs=pl.BlockSpec((tm, W_D), lambda i: (i, 0)),
        out_shape=jax.ShapeDtypeStruct((t, W_D), BF16),
        compiler_params=_cparams("parallel"),
        name="sgu",
    )(proj, proj, sg_g.reshape(1, W_D), sg_w, sg_b.T)


def _gate_lanes(g):
    i_f, f_f, i_b, f_b = (g[..., j * H_B:(j + 1) * H_B] for j in range(4))
    pad = jnp.zeros(g.shape[:-1] + (LANES - 2 * H_B,), g.dtype)
    return jnp.concatenate([i_f, i_b, pad, f_f, f_b, pad], axis=-1)


def _lambda_init(layer):
    return 0.8 - 0.6 * math.exp(-0.3 * layer)


def _trunk(x, mods, p, batch, seq, caches):
    ctx_out = {}
    for l in range(DEPTH):
        mod = mods[l]
        if l % 2 == 0:
            proj, gates = _inproj(x, p["g_mix"][l], mod, 1, 0, p["w_even"], p["w_even_gates"])
            if caches is None:
                mix_a, ctx_out["na_k"], ctx_out["na_v"] = _attn_ctx(proj, p["na_gq"], p["na_gk"], batch, seq)
                mix_b, c1, n1, m1 = _mlstm(proj, gates, p["ml_b"], p["ml_g"], batch, seq, emit_state=True)
                ctx_out.update(mlstm_C=c1, mlstm_n=n1, mlstm_m=m1[..., 0])
            else:
                mix_a = _natten(proj, caches["na_k"], caches["na_v"], p["na_gq"], p["na_gk"], p["na_rpb"], batch, seq)
                (mix_b,) = _mlstm(proj, gates, p["ml_b"], p["ml_g"], batch, seq,
                                  state=(caches["C"], caches["n"], caches["m"]))
        else:
            proj = _inproj(x, p["g_mix"][l], mod, 1, 0, p["w_odd"])
            lam_init = _lambda_init(l)
            if caches is None:
                mix_a, dk, dv = _diff_attn(proj, p["diff_gq"], p["diff_gk"], p["diff_g_out"], p["diff_lam"],
                                           lam_init, batch, seq, heads=H_C, emit_ctx=True)
                ctx_out["diff_k"] = dk.reshape(batch, 1, H_C, seq, 2, DQK)
                ctx_out["diff_v"] = dv
            else:
                (mix_a,) = _diff_attn(proj, p["diff_gq"], p["diff_gk"], p["diff_g_out"], p["diff_lam"],
                                      lam_init, batch, seq, heads=1,
                                      cache=(caches["diff_k"], caches["diff_v"]), rope=_rope_tables(seq))
            mix_b = _sgu(proj, p["sg_g"], p["sg_w"], p["sg_b"])
        x = _outproj(mix_a, mix_b, p["w_out"][l], x, mod, 2)
        x = _ffn(x, p["g_ffn"][l], mod, p["w_up"][l], p["conv_w"][l], p["conv_b"][l], p["w_down"][l], seq)
    return x, ctx_out


def kernel(x_prompt, x_sample, cache_na_k, cache_na_v, state_mlstm_C, state_mlstm_n, state_mlstm_m,
           cache_diff_k, cache_diff_v, c, c_ctx, w_mod, b_mod, g_mix, g_ffn, w_out, w_in_even,
           na_gq, na_gk, na_rpb, ml_b_gates, ml_g_out, w_in_odd, diff_gq, diff_gk, diff_lam,
           diff_g_out, sg_g_v, sg_w, sg_b, ffn_w_up, ffn_conv_w, ffn_conv_b, ffn_w_down):
    batch, seq, _ = x_prompt.shape
    dbatch, dseq, _ = x_sample.shape
    n_main = 3 * W_A + 4 * W_B

    cond8 = jnp.zeros((8, D_MODEL), F32).at[0].set(c_ctx).at[1:1 + dbatch].set(c)
    mod = _modulation(cond8, w_mod, b_mod)
    mods_ctx = [mod[l, 0:1].reshape(1, 6, 1, D_MODEL) for l in range(DEPTH)]
    mods_lat = [mod[l, 1:1 + dbatch].reshape(dbatch, 6, 1, D_MODEL) for l in range(DEPTH)]

    w_even = w_in_even[0]
    p = dict(
        g_mix=g_mix, g_ffn=g_ffn,
        w_even=w_even[:, :n_main].astype(BF16),
        w_even_gates=_gate_lanes(w_even[:, n_main:]).astype(BF16),
        w_odd=w_in_odd[0].astype(BF16),
        w_out=w_out.astype(BF16),
        w_up=ffn_w_up.astype(BF16), w_down=ffn_w_down.astype(BF16),
        conv_w=ffn_conv_w, conv_b=ffn_conv_b,
        na_gq=na_gq[0], na_gk=na_gk[0], na_rpb=na_rpb[0],
        ml_b=_gate_lanes(ml_b_gates[0].reshape(1, N_GATES)), ml_g=ml_g_out[0],
        diff_gq=diff_gq[0], diff_gk=diff_gk[0], diff_lam=diff_lam[0], diff_g_out=diff_g_out[0],
        sg_g=sg_g_v[0], sg_w=sg_w[0], sg_b=sg_b[0],
    )

    y_prompt, ctx = _trunk(x_prompt.reshape(batch * seq, D_MODEL), mods_ctx, p, batch, seq, None)
    caches = dict(
        na_k=cache_na_k, na_v=cache_na_v, C=state_mlstm_C, n=state_mlstm_n,
        m=jnp.broadcast_to(state_mlstm_m[..., None], state_mlstm_m.shape + (LANES,)),
        diff_k=cache_diff_k, diff_v=cache_diff_v,
    )
    y_sample, _ = _trunk(x_sample.reshape(dbatch * dseq, D_MODEL), mods_lat, p, dbatch, dseq, caches)
    return (y_prompt.reshape(batch, seq, D_MODEL), y_sample.reshape(dbatch, dseq, D_MODEL),
            ctx["na_k"], ctx["na_v"], ctx["mlstm_C"], ctx["mlstm_n"], ctx["mlstm_m"],
            ctx["diff_k"], ctx["diff_v"])
```

```python
import functools
import math

import numpy as np
import jax
import jax.numpy as jnp
from jax import lax
from jax.experimental import pallas as pl
from jax.experimental.pallas import tpu as pltpu

F32 = jnp.float32
BF16 = jnp.bfloat16

D_MODEL = 2048
DEPTH = 2
GRID_W = 64
DH = 128
H_A = 8
W_A = H_A * DH
WIN_H = 8
WIN_W = 16
H_B = 8
W_B = H_B * DH
ML_CHUNK = 128
H_C = 8
DQK = 64
W_CQK = H_C * 2 * DQK
W_CV = H_C * DH
ROPE_THETA = 10000.0
G_D = 8
SG_CHUNK = 128
W_D = 1024
D_FF = 5632
Q_BLOCK = 128
EPS = 1e-6
N_GATES = 4 * H_B
N_EVEN_MAIN = 3 * W_A + 4 * W_B

LANES = 128
V7X_VMEM_BYTES = 64 * 1024 * 1024
VMEM_LIMIT = V7X_VMEM_BYTES - 8 * 1024 * 1024

NEG_INF = float("-inf")


def _cparams(*sem):
    return pltpu.CompilerParams(dimension_semantics=sem, vmem_limit_bytes=VMEM_LIMIT)


def _dot(a, b):
    return jnp.dot(a, b, preferred_element_type=F32)


def _dot_nt(a, b):
    return lax.dot_general(a, b, (((1,), (1,)), ((), ())), preferred_element_type=F32)


def _rms_lanes(x, g):
    ms = jnp.mean(x * x, axis=-1, keepdims=True)
    return x * lax.rsqrt(ms + EPS) * g


def _rms_lane_groups(x, g, group_ones, width):
    y = x * x
    hi = y.astype(BF16)
    mid = (y - hi.astype(F32)).astype(BF16)
    ss = _dot(hi, group_ones) + _dot(mid, group_ones)
    return x * lax.rsqrt(ss * (1.0 / width) + EPS) * g


def _norm_mod(x, g, scale, shift):
    return _rms_lanes(x, g) * (1.0 + scale) + shift


def _sigmoid(x):
    return 1.0 / (1.0 + jnp.exp(-x))


def _gelu_tanh(x):
    c = math.sqrt(2.0 / math.pi)
    return 0.5 * x * (1.0 + jnp.tanh(c * (x + 0.044715 * (x * x * x))))


MOD_TN = 1024


def _mod_kernel(c_ref, w_ref, b_ref, o_ref):
    c = c_ref[...]
    s = c * _sigmoid(c)
    o_ref[...] = _dot(s.astype(BF16), w_ref[...].astype(BF16)) + b_ref[...]


def _modulation(cond8, w_mod, b_mod):
    n = 6 * D_MODEL
    return pl.pallas_call(
        _mod_kernel,
        grid=(DEPTH, n // MOD_TN),
        in_specs=[
            pl.BlockSpec((8, D_MODEL), lambda l, j: (0, 0)),
            pl.BlockSpec((None, D_MODEL, MOD_TN), lambda l, j: (l, 0, j)),
            pl.BlockSpec((None, 1, MOD_TN), lambda l, j: (l, 0, j)),
        ],
        out_specs=pl.BlockSpec((None, 8, MOD_TN), lambda l, j: (l, 0, j)),
        out_shape=jax.ShapeDtypeStruct((DEPTH, 8, n), F32),
        compiler_params=_cparams("parallel", "arbitrary"),
        name="modulation",
    )(cond8, w_mod, b_mod.reshape(DEPTH, 1, n))


DENSE_TM = 1024
NORM_ROWS = 256
PROJ_DTYPE = BF16


def _fill_normed(x_ref, g_ref, sc_ref, sh_ref, hb_ref):
    tm = x_ref.shape[0]

    def body(c, carry):
        r = pl.ds(pl.multiple_of(c * NORM_ROWS, NORM_ROWS), NORM_ROWS)
        h = _norm_mod(x_ref[r, :], g_ref[...], sc_ref[...], sh_ref[...])
        hb_ref[r, :] = h.astype(BF16)
        return carry

    lax.fori_loop(0, tm // NORM_ROWS, body, 0)


def _inproj_kernel(x_ref, g_ref, sc_ref, sh_ref, w_ref, o_ref, hb_ref):
    @pl.when(pl.program_id(1) == 0)
    def _():
        _fill_normed(x_ref, g_ref, sc_ref, sh_ref, hb_ref)

    o_ref[...] = _dot(hb_ref[...], w_ref[...]).astype(o_ref.dtype)


def _inproj_gates_kernel(x_ref, g_ref, sc_ref, sh_ref, w_ref, wg_ref, o_ref, og_ref, hb_ref):
    @pl.when(pl.program_id(1) == 0)
    def _():
        _fill_normed(x_ref, g_ref, sc_ref, sh_ref, hb_ref)
        og_ref[...] = _dot(hb_ref[...], wg_ref[...])

    o_ref[...] = _dot(hb_ref[...], w_ref[...]).astype(o_ref.dtype)


def _mod_spec(idx, tm, rows_per_mod):
    return pl.BlockSpec((None, None, 1, D_MODEL), lambda i, n: ((i * tm) // rows_per_mod, idx, 0, 0))


def _inproj(x, g, mod, scale_idx, shift_idx, w, n, gates_block=None, tn=1024):
    t = x.shape[0]
    tm = DENSE_TM
    rows_per_mod = t // mod.shape[0]
    in_specs = [
        pl.BlockSpec((tm, D_MODEL), lambda i, j: (i, 0)),
        pl.BlockSpec((1, D_MODEL), lambda i, j: (0, 0)),
        _mod_spec(scale_idx, tm, rows_per_mod),
        _mod_spec(shift_idx, tm, rows_per_mod),
        pl.BlockSpec((D_MODEL, tn), lambda i, j: (0, j)),
    ]
    out_specs = pl.BlockSpec((tm, tn), lambda i, j: (i, j))
    out_shape = jax.ShapeDtypeStruct((t, n), PROJ_DTYPE)
    args = [x, g.reshape(1, D_MODEL), mod, mod, w]
    kern = _inproj_kernel
    if gates_block is not None:
        in_specs.append(pl.BlockSpec((D_MODEL, LANES), lambda i, j: (0, gates_block)))
        out_specs = [out_specs, pl.BlockSpec((tm, LANES), lambda i, j: (i, 0))]
        out_shape = [out_shape, jax.ShapeDtypeStruct((t, LANES), F32)]
        args.append(w)
        kern = _inproj_gates_kernel
    return pl.pallas_call(
        kern,
        grid=(t // tm, n // tn),
        in_specs=in_specs,
        out_specs=out_specs,
        out_shape=out_shape,
        scratch_shapes=[pltpu.VMEM((tm, D_MODEL), BF16)],
        compiler_params=_cparams("parallel", "arbitrary"),
        name="inproj",
    )(*args)


OUT_TM = 512


def _outproj_kernel(a_ref, b_ref, wa_ref, wb_ref, x_ref, gate_ref, o_ref):
    acc = _dot(a_ref[...], wa_ref[...]) + _dot(b_ref[...], wb_ref[...])
    o_ref[...] = x_ref[...] + gate_ref[...] * acc


def _outproj(mix_a, mix_b, w, x, mod, gate_idx):
    t = x.shape[0]
    tm = OUT_TM
    half = mix_a.shape[1]
    rows_per_mod = t // mod.shape[0]
    return pl.pallas_call(
        _outproj_kernel,
        grid=(t // tm,),
        in_specs=[
            pl.BlockSpec((tm, half), lambda i: (i, 0)),
            pl.BlockSpec((tm, half), lambda i: (i, 0)),
            pl.BlockSpec((half, D_MODEL), lambda i: (0, 0)),
            pl.BlockSpec((half, D_MODEL), lambda i: (1, 0)),
            pl.BlockSpec((tm, D_MODEL), lambda i: (i, 0)),
            pl.BlockSpec((None, None, 1, D_MODEL), lambda i: ((i * tm) // rows_per_mod, gate_idx, 0, 0)),
        ],
        out_specs=pl.BlockSpec((tm, D_MODEL), lambda i: (i, 0)),
        out_shape=jax.ShapeDtypeStruct((t, D_MODEL), F32),
        compiler_params=_cparams("parallel"),
        name="outproj",
    )(mix_a, mix_b, w, w, x, mod)


FFN_TF = 512


def _ffn_kernel(x_ref, g_ref, sc_ref, sh_ref, gate_ref, wg_ref, wv_ref, cwg_ref, cwv_ref, cbg_ref, cbv_ref,
                wd_ref, o_ref, hb_ref, *, seq_len):
    f = pl.program_id(1)
    tm = x_ref.shape[0]

    @pl.when(f == 0)
    def _():
        _fill_normed(x_ref, g_ref, sc_ref, sh_ref, hb_ref)
        o_ref[...] = jnp.zeros_like(o_ref)

    pos = lax.broadcasted_iota(jnp.int32, (tm, 1), 0) & (seq_len - 1)
    first = pos == 0
    last = pos == seq_len - 1

    def conv(a, cw_ref, cb_ref):
        prev = jnp.where(first, 0.0, pltpu.roll(a, 1, 0))
        nxt = jnp.where(last, 0.0, pltpu.roll(a, tm - 1, 0))
        return prev * cw_ref[0:1, :] + a * cw_ref[1:2, :] + nxt * cw_ref[2:3, :] + cb_ref[...]

    hb = hb_ref[...]
    cg = conv(_dot(hb, wg_ref[...]), cwg_ref, cbg_ref)
    cv = conv(_dot(hb, wv_ref[...]), cwv_ref, cbv_ref)
    act = (cg * _sigmoid(cg) * cv).astype(BF16)
    o_ref[...] += _dot(act, wd_ref[...])

    @pl.when(f == pl.num_programs(1) - 1)
    def _():
        o_ref[...] = x_ref[...] + gate_ref[...] * o_ref[...]


def _ffn(x, g, mod, w_up, conv_w, conv_b, w_down, seq_len):
    t = x.shape[0]
    tm, tf = DENSE_TM, FFN_TF
    nf = D_FF // tf
    rows_per_mod = t // mod.shape[0]
    conv_b = conv_b.reshape(1, 2 * D_FF)
    assert seq_len & (seq_len - 1) == 0 and tm % seq_len == 0
    return pl.pallas_call(
        functools.partial(_ffn_kernel, seq_len=seq_len),
        grid=(t // tm, nf),
        in_specs=[
            pl.BlockSpec((tm, D_MODEL), lambda i, f: (i, 0), pipeline_mode=pl.Buffered(1)),
            pl.BlockSpec((1, D_MODEL), lambda i, f: (0, 0)),
            _mod_spec(4, tm, rows_per_mod),
            _mod_spec(3, tm, rows_per_mod),
            _mod_spec(5, tm, rows_per_mod),
            pl.BlockSpec((D_MODEL, tf), lambda i, f: (0, f)),
            pl.BlockSpec((D_MODEL, tf), lambda i, f: (0, nf + f)),
            pl.BlockSpec((3, tf), lambda i, f: (0, f)),
            pl.BlockSpec((3, tf), lambda i, f: (0, nf + f)),
            pl.BlockSpec((1, tf), lambda i, f: (0, f)),
            pl.BlockSpec((1, tf), lambda i, f: (0, nf + f)),
            pl.BlockSpec((tf, D_MODEL), lambda i, f: (f, 0)),
        ],
        out_specs=pl.BlockSpec((tm, D_MODEL), lambda i, f: (i, 0)),
        out_shape=jax.ShapeDtypeStruct((t, D_MODEL), F32),
        scratch_shapes=[pltpu.VMEM((tm, D_MODEL), BF16)],
        compiler_params=_cparams("parallel", "arbitrary"),
        name="convffn",
    )(x, g.reshape(1, D_MODEL), mod, mod, mod, w_up, w_up, conv_w, conv_w, conv_b, conv_b, w_down)


def _attn_ctx_kernel(q_ref, k_ref, v_ref, gq_ref, gk_ref, o_ref, ko_ref, vo_ref):
    scale = DH ** -0.5
    for h in range(H_A):
        sl = slice(h * DH, (h + 1) * DH)
        q = _rms_lanes(q_ref[:, sl].astype(F32), gq_ref[...])
        k = _rms_lanes(k_ref[:, sl].astype(F32), gk_ref[...])
        v = v_ref[:, sl].astype(F32)
        ko_ref[h] = k
        vo_ref[h] = v
        s = _dot_nt(q.astype(BF16), k.astype(BF16)) * scale
        p = jnp.exp(s - jnp.max(s, axis=-1, keepdims=True))
        p = p / jnp.sum(p, axis=-1, keepdims=True)
        o_ref[:, sl] = _dot(p.astype(BF16), v.astype(BF16)).astype(o_ref.dtype)


def _attn_ctx(proj, gq, gk, batch, seq):
    t = proj.shape[0]
    kv_shape = jax.ShapeDtypeStruct((batch, 1, H_A, seq, DH), F32)
    kv_spec = pl.BlockSpec((None, None, H_A, seq, DH), lambda b: (b, 0, 0, 0, 0))
    return pl.pallas_call(
        _attn_ctx_kernel,
        grid=(batch,),
        in_specs=[
            pl.BlockSpec((seq, W_A), lambda b: (b, 0)),
            pl.BlockSpec((seq, W_A), lambda b: (b, 1)),
            pl.BlockSpec((seq, W_A), lambda b: (b, 2)),
            pl.BlockSpec((1, DH), lambda b: (0, 0)),
            pl.BlockSpec((1, DH), lambda b: (0, 0)),
        ],
        out_specs=[pl.BlockSpec((seq, W_A), lambda b: (b, 0)), kv_spec, kv_spec],
        out_shape=[jax.ShapeDtypeStruct((t, W_A), BF16), kv_shape, kv_shape],
        compiler_params=_cparams("parallel"),
        name="attn_ctx",
    )(proj, proj, proj, gq.reshape(1, DH), gk.reshape(1, DH))


NA_ROWS = 16
NA_PAIRS = 2 * WIN_H - 2
RPB_H = 2 * WIN_H - 1
RPB_W = 2 * WIN_W - 1


def _natten_kernel(rpb_ref, q_ref, k_ref, v_ref, kc_ref, vc_ref, gq_ref, gk_ref, o_ref,
                   qn_s, kn_s, v_s, kc_s, vc_s, bias_s):
    h = pl.program_id(1)
    scale = DH ** -0.5
    qn_s[...] = _rms_lanes(q_ref[...].astype(F32), gq_ref[...]).astype(BF16)
    kn_s[...] = _rms_lanes(k_ref[...].astype(F32), gk_ref[...]).astype(BF16)
    v_s[...] = v_ref[...].astype(BF16)
    kc_s[...] = kc_ref[...].astype(BF16)
    vc_s[...] = vc_ref[...].astype(BF16)

    lane = lax.broadcasted_iota(jnp.int32, (GRID_W, LANES), 1)
    qc = lax.broadcasted_iota(jnp.int32, (GRID_W, LANES), 0)
    kcol = lane % GRID_W
    upper = lane >= GRID_W
    dcol = kcol - qc + (WIN_W - 1)
    col0 = jnp.clip(qc - WIN_W // 2, 0, GRID_W - WIN_W)
    col_ok = (kcol >= col0) & (kcol < col0 + WIN_W)
    base = h * (RPB_H * RPB_W)

    def build(p, carry):
        acc = jnp.zeros((GRID_W, LANES), F32)
        for d in range(RPB_W):
            lo = rpb_ref[base + p * RPB_W + d]
            hi = rpb_ref[base + (p + 1) * RPB_W + d]
            acc = jnp.where(dcol == d, jnp.where(upper, hi, lo), acc)
        bias_s[p] = jnp.where(col_ok, acc, NEG_INF)
        return carry

    lax.fori_loop(0, NA_PAIRS, build, 0)

    kh = WIN_H
    n_loc = kh * GRID_W
    row0 = [min(max(r - kh // 2, 0), NA_ROWS - kh) for r in range(NA_ROWS)]
    q = [qn_s[r * GRID_W:(r + 1) * GRID_W, :] for r in range(NA_ROWS)]
    kc = kc_s[...]
    s_loc, s_ctx = [], []
    for r in range(NA_ROWS):
        dr0 = row0[r] - r + WIN_H - 1
        k_loc = kn_s[row0[r] * GRID_W:row0[r] * GRID_W + n_loc, :]
        bias = jnp.concatenate([bias_s[dr0 + 2 * i] for i in range(kh // 2)], axis=-1)
        s_loc.append(_dot_nt(q[r], k_loc) * scale + bias)
        s_ctx.append(_dot_nt(q[r], kc) * scale)
    p_loc, p_ctx = [], []
    for r in range(NA_ROWS):
        m = jnp.maximum(jnp.max(s_loc[r], axis=-1, keepdims=True), jnp.max(s_ctx[r], axis=-1, keepdims=True))
        e_loc = jnp.exp(s_loc[r] - m)
        e_ctx = jnp.exp(s_ctx[r] - m)
        l = jnp.sum(e_loc, axis=-1, keepdims=True) + jnp.sum(e_ctx, axis=-1, keepdims=True)
        p_loc.append((e_loc / l).astype(BF16))
        p_ctx.append((e_ctx / l).astype(BF16))
    vc = vc_s[...]
    for r in range(NA_ROWS):
        v_loc = v_s[row0[r] * GRID_W:row0[r] * GRID_W + n_loc, :]
        out = _dot(p_loc[r], v_loc) + _dot(p_ctx[r], vc)
        o_ref[r * GRID_W:(r + 1) * GRID_W, :] = out.astype(o_ref.dtype)


def _natten(proj, cache_k, cache_v, gq, gk, rpb, batch, seq):
    t = proj.shape[0]
    past = cache_k.shape[3]
    ctx_spec = pl.BlockSpec((None, None, None, past, DH), lambda b, h: (b, 0, h, 0, 0))
    return pl.pallas_call(
        _natten_kernel,
        grid=(batch, H_A),
        in_specs=[
            pl.BlockSpec(memory_space=pltpu.SMEM),
            pl.BlockSpec((seq, DH), lambda b, h: (b, h)),
            pl.BlockSpec((seq, DH), lambda b, h: (b, H_A + h)),
            pl.BlockSpec((seq, DH), lambda b, h: (b, 2 * H_A + h)),
            ctx_spec,
            ctx_spec,
            pl.BlockSpec((1, DH), lambda b, h: (0, 0)),
            pl.BlockSpec((1, DH), lambda b, h: (0, 0)),
        ],
        out_specs=pl.BlockSpec((seq, DH), lambda b, h: (b, h)),
        out_shape=jax.ShapeDtypeStruct((t, W_A), BF16),
        scratch_shapes=[
            pltpu.VMEM((seq, DH), BF16),
            pltpu.VMEM((seq, DH), BF16),
            pltpu.VMEM((seq, DH), BF16),
            pltpu.VMEM((past, DH), BF16),
            pltpu.VMEM((past, DH), BF16),
            pltpu.VMEM((NA_PAIRS, GRID_W, LANES), F32),
        ],
        compiler_params=_cparams("parallel", "arbitrary"),
        name="natten",
    )(rpb.reshape(-1), proj, proj, proj, cache_k, cache_v, gq.reshape(1, DH), gk.reshape(1, DH))


def _split3(x):
    hi = x.astype(BF16)
    r1 = x - hi.astype(F32)
    mid = r1.astype(BF16)
    lo = (r1 - mid.astype(F32)).astype(BF16)
    return hi, mid, lo


def _mlstm_kernel(*refs, has_state, emit_state, seq):
    it = iter(refs)
    q_ref, k_ref, v_ref, ob_ref, gt_ref, mlb_ref, mlg_ref = (next(it) for _ in range(7))
    if has_state:
        c0_ref, n0_ref, m0_ref = (next(it) for _ in range(3))
    o_ref = next(it)
    if emit_state:
        co_ref, no_ref, mo_ref = (next(it) for _ in range(3))
    r_s, rt_s, bt_s, hf_s, hb_s, ct_s, n_s, m_s = (next(it) for _ in range(8))

    L = ML_CHUNK
    nc = seq // L
    kscale = DH ** -0.5

    ri = lax.broadcasted_iota(jnp.int32, (L, L), 0)
    ci = lax.broadcasted_iota(jnp.int32, (L, L), 1)
    tri_pre = (ci <= ri).astype(BF16)
    tri_suf = (ci >= ri).astype(BF16)
    mask_t = (ri <= ci, ri >= ci)
    fwd_lane = lax.broadcasted_iota(jnp.int32, (L, LANES), 1) < 2 * H_B

    for c in range(nc):
        rows = slice(c * L, (c + 1) * L)
        g = gt_ref[rows, :] + mlb_ref[...]
        gf = jnp.minimum(g, 0.0) - jnp.log1p(jnp.exp(-jnp.abs(g)))
        hi, mid, lo = _split3(gf)
        pre = _dot(tri_pre, hi) + _dot(tri_pre, mid) + _dot(tri_pre, lo)
        suf = _dot(tri_suf, hi) + _dot(tri_suf, mid) + _dot(tri_suf, lo)
        b = pltpu.roll(jnp.where(fwd_lane, pre, suf), LANES - H_B, 1)
        r = g - b
        r_s[rows, :] = r
        rt_s[c] = r.T
        bt_s[c] = b.T

    for d in range(2):
        for h in range(H_B):
            ct_s[d, h] = c0_ref[d, h].T if has_state else jnp.zeros((DH, DH), F32)
    if has_state:
        n_s[...] = n0_ref[...]
        m_s[...] = m0_ref[...]
    else:
        n_s[...] = jnp.zeros_like(n_s)
        m_s[...] = jnp.zeros_like(m_s)

    pairs = [(d, h) for d in range(2) for h in range(H_B)]

    def step(c, carry):
        rows, r_all, rt_all, bt_all = [], [], [], []
        for d in range(2):
            cc = c if d == 0 else nc - 1 - c
            rows.append(pl.ds(pl.multiple_of(cc * L, L), L))
            r_all.append(r_s[rows[d], :])
            rt_all.append(rt_s[cc])
            bt_all.append(bt_s[cc])

        qb, kb, v_t, rm, r_row, b_row, a_row, qk = {}, {}, {}, {}, {}, {}, {}, {}
        for p in pairs:
            d, h = p
            col = 2 * H_B * d + h
            sl = slice(h * DH, (h + 1) * DH)
            qb[p] = q_ref[rows[d], sl].astype(BF16)
            kb[p] = (k_ref[rows[d], sl].astype(F32) * kscale).astype(BF16)
            v_t[p] = v_ref[rows[d], sl].astype(F32).T
            r_row[p] = rt_all[d][col:col + 1, :]
            b_row[p] = bt_all[d][col:col + 1, :]
            rm[p] = jnp.where(mask_t[d], jnp.broadcast_to(r_all[d][:, col:col + 1], (L, L)), NEG_INF)
            a_row[p] = jnp.max(rm[p], axis=0, keepdims=True)
            qk[p] = _dot_nt(kb[p], qb[p])

        m_row = {(d, h): m_s[d, h:h + 1, :] for d, h in pairs}
        n_row = {(d, h): n_s[d, h:h + 1, :] for d, h in pairs}
        ct = {(d, h): ct_s[d, h] for d, h in pairs}
        big_m = {p: jnp.maximum(m_row[p], a_row[p]) for p in pairs}
        w_in = {p: jnp.exp(m_row[p] - big_m[p]) for p in pairs}
        cq = {p: _dot_nt(ct[p].astype(BF16), qb[p]) for p in pairs}
        qn = {p: _dot_nt(jnp.broadcast_to(n_row[p], (8, DH)).astype(BF16), qb[p])[0:1, :] for p in pairs}
        w_t = {p: jnp.exp(rm[p] - big_m[p]) * qk[p] for p in pairs}
        pv = {p: _dot(v_t[p].astype(BF16), w_t[p].astype(BF16)) for p in pairs}
        for p in pairs:
            d, h = p
            den = w_in[p] * qn[p] + jnp.sum(w_t[p], axis=0, keepdims=True)
            h_t = (w_in[p] * cq[p] + pv[p]) / jnp.maximum(jnp.abs(den), jnp.exp(-(b_row[p] + big_m[p])))
            (hf_s if d == 0 else hb_s)[rows[d], slice(h * DH, (h + 1) * DH)] = h_t.T

        m_last, ws = {}, {}
        for p in pairs:
            m_last[p] = jnp.maximum(m_row[p], jnp.max(r_row[p], axis=-1, keepdims=True))
            ws[p] = jnp.exp(r_row[p] - m_last[p])
        kv = {p: _dot((v_t[p] * ws[p]).astype(BF16), kb[p]) for p in pairs}
        nk = {p: _dot(jnp.broadcast_to(ws[p], (8, L)).astype(BF16), kb[p])[0:1, :] for p in pairs}
        for p in pairs:
            d, h = p
            a_prev = jnp.exp(m_row[p] - m_last[p])
            bl = b_row[p][:, L - 1:L] if d == 0 else b_row[p][:, 0:1]
            ct_s[d, h] = a_prev * ct[p] + kv[p]
            n_s[d, h:h + 1, :] = a_prev * n_row[p] + nk[p]
            m_s[d, h:h + 1, :] = bl + m_last[p]
        return carry

    lax.fori_loop(0, nc, step, 0)

    if emit_state:
        for d in range(2):
            for h in range(H_B):
                co_ref[d, h] = ct_s[d, h].T
        no_ref[...] = n_s[...]
        mo_ref[...] = m_s[...]

    rows_ep = 256
    for r in range(seq // rows_ep):
        rr = slice(r * rows_ep, (r + 1) * rows_ep)
        for h in range(H_B):
            sl = slice(h * DH, (h + 1) * DH)
            hm = hf_s[rr, sl] + hb_s[rr, sl]
            ob = ob_ref[rr, sl].astype(F32)
            o_ref[rr, sl] = (_sigmoid(ob) * _rms_lanes(hm, mlg_ref[:, sl])).astype(o_ref.dtype)


def _mlstm(proj, gates, ml_b, ml_g, batch, seq, state=None, emit_state=False):
    t = proj.shape[0]
    col0 = 3 * W_A // W_B
    big = lambda j: pl.BlockSpec((seq, W_B), lambda b: (b, col0 + j))
    in_specs = [big(0), big(1), big(2), big(3),
                pl.BlockSpec((seq, LANES), lambda b: (b, 0)),
                pl.BlockSpec((1, LANES), lambda b: (0, 0)),
                pl.BlockSpec((1, W_B), lambda b: (0, 0))]
    args = [proj, proj, proj, proj, gates, ml_b, ml_g.reshape(1, W_B)]
    c_spec = pl.BlockSpec((None, None, 2, H_B, DH, DH), lambda b: (b, 0, 0, 0, 0, 0))
    n_spec = pl.BlockSpec((None, None, 2, H_B, DH), lambda b: (b, 0, 0, 0, 0))
    if state is not None:
        in_specs += [c_spec, n_spec, n_spec]
        args += list(state)
    out_specs = [pl.BlockSpec((seq, W_B), lambda b: (b, 0))]
    out_shape = [jax.ShapeDtypeStruct((t, W_B), BF16)]
    if emit_state:
        out_specs += [c_spec, n_spec, n_spec]
        out_shape += [jax.ShapeDtypeStruct((batch, 1, 2, H_B, DH, DH), F32),
                      jax.ShapeDtypeStruct((batch, 1, 2, H_B, DH), F32),
                      jax.ShapeDtypeStruct((batch, 1, 2, H_B, LANES), F32)]
    nc = seq // ML_CHUNK
    return pl.pallas_call(
        functools.partial(_mlstm_kernel, has_state=state is not None, emit_state=emit_state, seq=seq),
        grid=(batch,),
        in_specs=in_specs,
        out_specs=out_specs,
        out_shape=out_shape,
        scratch_shapes=[
            pltpu.VMEM((seq, LANES), F32),
            pltpu.VMEM((nc, ML_CHUNK, LANES), F32),
            pltpu.VMEM((nc, ML_CHUNK, LANES), F32),
            pltpu.VMEM((seq, W_B), F32),
            pltpu.VMEM((seq, W_B), F32),
            pltpu.VMEM((2, H_B, DH, DH), F32),
            pltpu.VMEM((2, H_B, DH), F32),
            pltpu.VMEM((2, H_B, LANES), F32),
        ],
        compiler_params=_cparams("parallel"),
        name="mlstm",
    )(*args)


DIFF_QB = 256


def _diff_kernel(*refs, heads, seq, past, rope, emit_ctx, lam_init):
    it = iter(refs)
    q_ref, k_ref, v_ref, gq_ref, gk_ref, go_ref, lam_ref = (next(it) for _ in range(7))
    if past:
        kc_ref, vc_ref = next(it), next(it)
    if rope:
        cos_ref, s1_ref, s2_ref = next(it), next(it), next(it)
    o_ref = next(it)
    if emit_ctx:
        ko_ref, vo_ref = next(it), next(it)

    scale = DQK ** -0.5
    lv = lam_ref[...]
    lam = (jnp.exp(jnp.sum(lv[0:1, :] * lv[1:2, :], axis=-1, keepdims=True))
           - jnp.exp(jnp.sum(lv[2:3, :] * lv[3:4, :], axis=-1, keepdims=True)) + lam_init)
    low = lax.broadcasted_iota(jnp.int32, (seq, LANES), 1) < DQK
    hr = lax.broadcasted_iota(jnp.int32, (LANES, LANES), 0) < DQK
    hc = lax.broadcasted_iota(jnp.int32, (LANES, LANES), 1) < DQK
    halves = (hr == hc).astype(BF16)

    def apply_rope(x):
        return (x * cos_ref[...] + pltpu.roll(x, LANES - DQK // 4, 1) * s1_ref[...]
                + pltpu.roll(x, DQK // 4, 1) * s2_ref[...])

    qb_rows = min(DIFF_QB, seq)
    blocks = [slice(i * qb_rows, (i + 1) * qb_rows) for i in range(seq // qb_rows)]
    head_sl = [slice(hh * DH, (hh + 1) * DH) for hh in range(heads)]
    qn = [_rms_lane_groups(q_ref[:, sl].astype(F32), gq_ref[...], halves, DQK) for sl in head_sl]
    kn = [_rms_lane_groups(k_ref[:, sl].astype(F32), gk_ref[...], halves, DQK) for sl in head_sl]
    v = [v_ref[:, sl].astype(F32) for sl in head_sl]
    if emit_ctx:
        for hh in range(heads):
            ko_ref[hh] = kn[hh]
            vo_ref[hh] = v[hh]
    if rope:
        qn = [apply_rope(x) for x in qn]
        kn = [apply_rope(x) for x in kn]
    assert math.frexp(scale)[0] == 0.5
    q0 = [(jnp.where(low, x, 0.0) * scale).astype(BF16) for x in qn]
    q1 = [(jnp.where(low, 0.0, x) * scale).astype(BF16) for x in qn]
    ka, vt = [], []
    for hh in range(heads):
        k_parts = [kn[hh].astype(BF16)]
        vt_parts = [v[hh][c * LANES:(c + 1) * LANES, :].T.astype(BF16) for c in range(seq // LANES)]
        if past:
            k_parts.append(kc_ref[hh].astype(BF16))
            vt_parts += [vc_ref[hh, c * LANES:(c + 1) * LANES, :].T.astype(BF16) for c in range(past // LANES)]
        ka.append(jnp.concatenate(k_parts, axis=0))
        vt.append(jnp.concatenate(vt_parts, axis=1))

    work = [(hh, rows) for hh in range(heads) for rows in blocks]
    s0 = [_dot_nt(ka[hh], q0[hh][rows, :]) for hh, rows in work]
    s1 = [_dot_nt(ka[hh], q1[hh][rows, :]) for hh, rows in work]

    def softmax_t(s, weight):
        p = jnp.exp(s - jnp.max(s, axis=0, keepdims=True))
        return p * (weight / jnp.sum(p, axis=0, keepdims=True))

    a_t = [(softmax_t(a, 1.0) - softmax_t(b, lam)).astype(BF16) for a, b in zip(s0, s1)]
    out_t = [_dot(vt[hh], a) for (hh, _), a in zip(work, a_t)]
    for (hh, rows), o in zip(work, out_t):
        o = o * lax.rsqrt(jnp.mean(o * o, axis=0, keepdims=True) + EPS)
        g_row = go_ref[:, head_sl[hh]] * (1.0 - lam_init)
        for c in range(qb_rows // LANES):
            r0 = rows.start + c * LANES
            o_ref[r0:r0 + LANES, head_sl[hh]] = (o[:, c * LANES:(c + 1) * LANES].T * g_row).astype(o_ref.dtype)


def _diff_attn(proj, gq, gk, g_out, lam_vec, lam_init, batch, seq, heads, cache=None, rope=None,
               emit_ctx=False):
    t = proj.shape[0]
    nh = H_C // heads
    w = heads * DH
    kblk = W_CQK // w
    past = 0 if cache is None else cache[0].shape[3]
    in_specs = [
        pl.BlockSpec((seq, w), lambda b, j: (b, j)),
        pl.BlockSpec((seq, w), lambda b, j: (b, kblk + j)),
        pl.BlockSpec((seq, w), lambda b, j: (b, 2 * kblk + j)),
        pl.BlockSpec((1, DH), lambda b, j: (0, 0)),
        pl.BlockSpec((1, DH), lambda b, j: (0, 0)),
        pl.BlockSpec((1, w), lambda b, j: (0, j)),
        pl.BlockSpec((4, DQK), lambda b, j: (0, 0)),
    ]
    args = [proj, proj, proj, jnp.tile(gq, 2).reshape(1, DH), jnp.tile(gk, 2).reshape(1, DH),
            g_out.reshape(1, W_CV), lam_vec]
    if cache is not None:
        ctx_spec = pl.BlockSpec((None, None, heads, past, DH), lambda b, j: (b, 0, j, 0, 0))
        in_specs += [ctx_spec, ctx_spec]
        args += [cache[0].reshape(cache[0].shape[:4] + (DH,)), cache[1]]
    if rope is not None:
        tab = pl.BlockSpec((seq, DH), lambda b, j: (0, 0))
        in_specs += [tab, tab, tab]
        args += list(rope)
    out_specs = [pl.BlockSpec((seq, w), lambda b, j: (b, j))]
    out_shape = [jax.ShapeDtypeStruct((t, W_CV), BF16)]
    if emit_ctx:
        kv_spec = pl.BlockSpec((None, None, heads, seq, DH), lambda b, j: (b, 0, j, 0, 0))
        kv_shape = jax.ShapeDtypeStruct((batch, 1, H_C, seq, DH), F32)
        out_specs += [kv_spec, kv_spec]
        out_shape += [kv_shape, kv_shape]
    return pl.pallas_call(
        functools.partial(_diff_kernel, heads=heads, seq=seq, past=past, rope=rope is not None,
                          emit_ctx=emit_ctx, lam_init=lam_init),
        grid=(batch, nh),
        in_specs=in_specs,
        out_specs=out_specs,
        out_shape=out_shape,
        compiler_params=_cparams("parallel", "arbitrary"),
        name="diff_attn",
    )(*args)


def _rope_tables(seq):
    tpos = jnp.arange(seq)
    pos = jnp.stack([tpos // GRID_W, tpos % GRID_W], axis=-1).astype(F32)
    half = DQK // 2
    inv = ROPE_THETA ** (-jnp.arange(0, half, 2, dtype=F32) / half)
    ang = pos[:, :, None] * inv
    cos = jnp.cos(ang)
    sin = jnp.sin(ang)
    zero = jnp.zeros_like(sin)
    lay = lambda first, second: jnp.tile(jnp.concatenate([first, second], axis=-1).reshape(seq, DQK), (1, 2))
    return lay(cos, cos), lay(-sin, zero), lay(zero, sin)


SGU_TM = 512


def _sgu_kernel(u_ref, vd_ref, sgg_ref, sgw_ref, sgb_ref, o_ref):
    tm = u_ref.shape[0]
    for c in range(tm // SG_CHUNK):
        rows = slice(c * SG_CHUNK, (c + 1) * SG_CHUNK)
        vn = _rms_lanes(_gelu_tanh(vd_ref[rows, :].astype(F32)), sgg_ref[...]).astype(BF16)
        for g in range(G_D):
            sl = slice(g * DH, (g + 1) * DH)
            gate = _dot(sgw_ref[g].astype(BF16), vn[:, sl]) + sgb_ref[:, g:g + 1]
            o_ref[rows, sl] = (_gelu_tanh(u_ref[rows, sl].astype(F32)) * gate).astype(o_ref.dtype)


def _sgu(proj, sg_g, sg_w, sg_b):
    t = proj.shape[0]
    tm = SGU_TM
    ublk = (2 * W_CQK + W_CV) // W_D
    return pl.pallas_call(
        _sgu_kernel,
        grid=(t // tm,),
        in_specs=[
            pl.BlockSpec((tm, W_D), lambda i: (i, ublk)),
            pl.BlockSpec((tm, W_D), lambda i: (i, ublk + 1)),
            pl.BlockSpec((1, W_D), lambda i: (0, 0)),
            pl.BlockSpec((G_D, SG_CHUNK, SG_CHUNK), lambda i: (0, 0, 0)),
            pl.BlockSpec((SG_CHUNK, G_D), lambda i: (0, 0)),
        ],
        out_specs=pl.BlockSpec((tm, W_D), lambda i: (i, 0)),
        out_shape=jax.ShapeDtypeStruct((t, W_D), BF16),
        compiler_params=_cparams("parallel"),
        name="sgu",
    )(proj, proj, sg_g.reshape(1, W_D), sg_w, sg_b.T)


def _lambda_init(layer):
    return 0.8 - 0.6 * math.exp(-0.3 * layer)


def _trunk(x, mods, p, batch, seq, caches):
    ctx_out = {}
    for l in range(DEPTH):
        mod = mods[l]
        if l % 2 == 0:
            proj, gates = _inproj(x, p["g_mix"][l], mod, 1, 0, p["w_even"], N_EVEN_MAIN, N_EVEN_MAIN // LANES)
            if caches is None:
                mix_a, ctx_out["na_k"], ctx_out["na_v"] = _attn_ctx(proj, p["na_gq"], p["na_gk"], batch, seq)
                mix_b, c1, n1, m1 = _mlstm(proj, gates, p["ml_b"], p["ml_g"], batch, seq, emit_state=True)
                ctx_out.update(mlstm_C=c1, mlstm_n=n1, mlstm_m=m1[..., 0])
            else:
                mix_a = _natten(proj, caches["na_k"], caches["na_v"], p["na_gq"], p["na_gk"], p["na_rpb"], batch, seq)
                (mix_b,) = _mlstm(proj, gates, p["ml_b"], p["ml_g"], batch, seq,
                                  state=(caches["C"], caches["n"], caches["m"]))
        else:
            proj = _inproj(x, p["g_mix"][l], mod, 1, 0, p["w_odd"], p["w_odd"].shape[1])
            lam_init = _lambda_init(l)
            if caches is None:
                mix_a, dk, dv = _diff_attn(proj, p["diff_gq"], p["diff_gk"], p["diff_g_out"], p["diff_lam"],
                                           lam_init, batch, seq, heads=H_C, emit_ctx=True)
                ctx_out["diff_k"] = dk.reshape(batch, 1, H_C, seq, 2, DQK)
                ctx_out["diff_v"] = dv
            else:
                (mix_a,) = _diff_attn(proj, p["diff_gq"], p["diff_gk"], p["diff_g_out"], p["diff_lam"],
                                      lam_init, batch, seq, heads=1,
                                      cache=(caches["diff_k"], caches["diff_v"]), rope=_rope_tables(seq))
            mix_b = _sgu(proj, p["sg_g"], p["sg_w"], p["sg_b"])
        x = _outproj(mix_a, mix_b, p["w_out"][l], x, mod, 2)
        x = _ffn(x, p["g_ffn"][l], mod, p["w_up"][l], p["conv_w"][l], p["conv_b"][l], p["w_down"][l], seq)
    return x, ctx_out


def kernel(x_prompt, x_sample, cache_na_k, cache_na_v, state_mlstm_C, state_mlstm_n, state_mlstm_m,
           cache_diff_k, cache_diff_v, c, c_ctx, w_mod, b_mod, g_mix, g_ffn, w_out, w_in_even,
           na_gq, na_gk, na_rpb, ml_b_gates, ml_g_out, w_in_odd, diff_gq, diff_gk, diff_lam,
           diff_g_out, sg_g_v, sg_w, sg_b, ffn_w_up, ffn_conv_w, ffn_conv_b, ffn_w_down):
    batch, seq, _ = x_prompt.shape
    dbatch, dseq, _ = x_sample.shape

    cond8 = jnp.zeros((8, D_MODEL), F32).at[0].set(c_ctx).at[1:1 + dbatch].set(c)
    mod = _modulation(cond8, w_mod, b_mod)
    mods_ctx = [mod[l, 0:1].reshape(1, 6, 1, D_MODEL) for l in range(DEPTH)]
    mods_lat = [mod[l, 1:1 + dbatch].reshape(dbatch, 6, 1, D_MODEL) for l in range(DEPTH)]

    p = dict(
        g_mix=g_mix, g_ffn=g_ffn,
        w_even=jnp.pad(w_in_even[0], ((0, 0), (0, LANES - N_GATES))).astype(BF16),
        w_odd=w_in_odd[0].astype(BF16),
        w_out=w_out.astype(BF16),
        w_up=ffn_w_up.astype(BF16), w_down=ffn_w_down.astype(BF16),
        conv_w=ffn_conv_w, conv_b=ffn_conv_b,
        na_gq=na_gq[0], na_gk=na_gk[0], na_rpb=na_rpb[0],
        ml_b=jnp.pad(ml_b_gates[0], (0, LANES - N_GATES)).reshape(1, LANES), ml_g=ml_g_out[0],
        diff_gq=diff_gq[0], diff_gk=diff_gk[0], diff_lam=diff_lam[0], diff_g_out=diff_g_out[0],
        sg_g=sg_g_v[0], sg_w=sg_w[0], sg_b=sg_b[0],
    )

    y_prompt, ctx = _trunk(x_prompt.reshape(batch * seq, D_MODEL), mods_ctx, p, batch, seq, None)
    caches = dict(
        na_k=cache_na_k, na_v=cache_na_v, C=state_mlstm_C, n=state_mlstm_n,
        m=jnp.broadcast_to(state_mlstm_m[..., None], state_mlstm_m.shape + (LANES,)),
        diff_k=cache_diff_k, diff_v=cache_diff_v,
    )
    y_sample, _ = _trunk(x_sample.reshape(dbatch * dseq, D_MODEL), mods_lat, p, dbatch, dseq, caches)
    return (y_prompt.reshape(batch, seq, D_MODEL), y_sample.reshape(dbatch, dseq, D_MODEL),
            ctx["na_k"], ctx["na_v"], ctx["mlstm_C"], ctx["mlstm_n"], ctx["mlstm_m"],
            ctx["diff_k"], ctx["diff_v"])
```

```python
import functools
import math

import numpy as np
import jax
import jax.numpy as jnp
from jax import lax
from jax.experimental import pallas as pl
from jax.experimental.pallas import tpu as pltpu

F32 = jnp.float32
BF16 = jnp.bfloat16

D_MODEL = 2048
DEPTH = 2
GRID_W = 64
DH = 128
H_A = 8
W_A = H_A * DH
WIN_H = 8
WIN_W = 16
H_B = 8
W_B = H_B * DH
ML_CHUNK = 128
H_C = 8
DQK = 64
W_CQK = H_C * 2 * DQK
W_CV = H_C * DH
ROPE_THETA = 10000.0
G_D = 8
SG_CHUNK = 128
W_D = 1024
D_FF = 5632
Q_BLOCK = 128
EPS = 1e-6
N_GATES = 4 * H_B
N_EVEN_MAIN = 3 * W_A + 4 * W_B

LANES = 128
V7X_VMEM_BYTES = 64 * 1024 * 1024
VMEM_LIMIT = V7X_VMEM_BYTES - 8 * 1024 * 1024

NEG_INF = float("-inf")


def _cparams(*sem):
    return pltpu.CompilerParams(dimension_semantics=sem, vmem_limit_bytes=VMEM_LIMIT)


def _dot(a, b):
    return jnp.dot(a, b, preferred_element_type=F32)


def _dot_nt(a, b):
    return lax.dot_general(a, b, (((1,), (1,)), ((), ())), preferred_element_type=F32)


def _rms_lanes(x, g):
    ms = jnp.mean(x * x, axis=-1, keepdims=True)
    return x * lax.rsqrt(ms + EPS) * g


def _rms_lane_groups(x, g, group_ones, width):
    y = x * x
    hi = y.astype(BF16)
    mid = (y - hi.astype(F32)).astype(BF16)
    ss = _dot(hi, group_ones) + _dot(mid, group_ones)
    return x * lax.rsqrt(ss * (1.0 / width) + EPS) * g


def _norm_mod(x, g, scale, shift):
    return _rms_lanes(x, g) * (1.0 + scale) + shift


def _sigmoid(x):
    return 1.0 / (1.0 + jnp.exp(-x))


def _gelu_tanh(x):
    c = math.sqrt(2.0 / math.pi)
    return 0.5 * x * (1.0 + jnp.tanh(c * (x + 0.044715 * (x * x * x))))


MOD_TN = 1024


def _mod_kernel(c_ref, w_ref, b_ref, o_ref):
    c = c_ref[...]
    s = c * _sigmoid(c)
    o_ref[...] = _dot(s.astype(BF16), w_ref[...].astype(BF16)) + b_ref[...]


def _modulation(cond8, w_mod, b_mod):
    n = 6 * D_MODEL
    return pl.pallas_call(
        _mod_kernel,
        grid=(DEPTH, n // MOD_TN),
        in_specs=[
            pl.BlockSpec((8, D_MODEL), lambda l, j: (0, 0)),
            pl.BlockSpec((None, D_MODEL, MOD_TN), lambda l, j: (l, 0, j)),
            pl.BlockSpec((None, 1, MOD_TN), lambda l, j: (l, 0, j)),
        ],
        out_specs=pl.BlockSpec((None, 8, MOD_TN), lambda l, j: (l, 0, j)),
        out_shape=jax.ShapeDtypeStruct((DEPTH, 8, n), F32),
        compiler_params=_cparams("parallel", "arbitrary"),
        name="modulation",
    )(cond8, w_mod, b_mod.reshape(DEPTH, 1, n))


DENSE_TM = 1024
NORM_ROWS = 256
PROJ_DTYPE = BF16


def _fill_normed(x_ref, g_ref, sc_ref, sh_ref, hb_ref):
    tm = x_ref.shape[0]

    def body(c, carry):
        r = pl.ds(pl.multiple_of(c * NORM_ROWS, NORM_ROWS), NORM_ROWS)
        h = _norm_mod(x_ref[r, :], g_ref[...], sc_ref[...], sh_ref[...])
        hb_ref[r, :] = h.astype(BF16)
        return carry

    lax.fori_loop(0, tm // NORM_ROWS, body, 0)


def _inproj_kernel(x_ref, g_ref, sc_ref, sh_ref, w_ref, o_ref, hb_ref):
    @pl.when(pl.program_id(1) == 0)
    def _():
        _fill_normed(x_ref, g_ref, sc_ref, sh_ref, hb_ref)

    o_ref[...] = _dot(hb_ref[...], w_ref[...]).astype(o_ref.dtype)


def _inproj_gates_kernel(x_ref, g_ref, sc_ref, sh_ref, w_ref, wg_ref, o_ref, og_ref, hb_ref):
    @pl.when(pl.program_id(1) == 0)
    def _():
        _fill_normed(x_ref, g_ref, sc_ref, sh_ref, hb_ref)
        og_ref[...] = _dot(hb_ref[...], wg_ref[...])

    o_ref[...] = _dot(hb_ref[...], w_ref[...]).astype(o_ref.dtype)


def _mod_spec(idx, tm, rows_per_mod):
    return pl.BlockSpec((None, None, 1, D_MODEL), lambda i, n: ((i * tm) // rows_per_mod, idx, 0, 0))


def _inproj(x, g, mod, scale_idx, shift_idx, w, n, gates_block=None, tn=1024):
    t = x.shape[0]
    tm = DENSE_TM
    rows_per_mod = t // mod.shape[0]
    in_specs = [
        pl.BlockSpec((tm, D_MODEL), lambda i, j: (i, 0)),
        pl.BlockSpec((1, D_MODEL), lambda i, j: (0, 0)),
        _mod_spec(scale_idx, tm, rows_per_mod),
        _mod_spec(shift_idx, tm, rows_per_mod),
        pl.BlockSpec((D_MODEL, tn), lambda i, j: (0, j)),
    ]
    out_specs = pl.BlockSpec((tm, tn), lambda i, j: (i, j))
    out_shape = jax.ShapeDtypeStruct((t, n), PROJ_DTYPE)
    args = [x, g.reshape(1, D_MODEL), mod, mod, w]
    kern = _inproj_kernel
    if gates_block is not None:
        in_specs.append(pl.BlockSpec((D_MODEL, LANES), lambda i, j: (0, gates_block)))
        out_specs = [out_specs, pl.BlockSpec((tm, LANES), lambda i, j: (i, 0))]
        out_shape = [out_shape, jax.ShapeDtypeStruct((t, LANES), F32)]
        args.append(w)
        kern = _inproj_gates_kernel
    return pl.pallas_call(
        kern,
        grid=(t // tm, n // tn),
        in_specs=in_specs,
        out_specs=out_specs,
        out_shape=out_shape,
        scratch_shapes=[pltpu.VMEM((tm, D_MODEL), BF16)],
        compiler_params=_cparams("parallel", "arbitrary"),
        name="inproj",
    )(*args)


OUT_TM = 512


def _outproj_kernel(a_ref, b_ref, wa_ref, wb_ref, x_ref, gate_ref, o_ref):
    acc = _dot(a_ref[...], wa_ref[...]) + _dot(b_ref[...], wb_ref[...])
    o_ref[...] = x_ref[...] + gate_ref[...] * acc


def _outproj(mix_a, mix_b, w, x, mod, gate_idx):
    t = x.shape[0]
    tm = OUT_TM
    half = mix_a.shape[1]
    rows_per_mod = t // mod.shape[0]
    return pl.pallas_call(
        _outproj_kernel,
        grid=(t // tm,),
        in_specs=[
            pl.BlockSpec((tm, half), lambda i: (i, 0)),
            pl.BlockSpec((tm, half), lambda i: (i, 0)),
            pl.BlockSpec((half, D_MODEL), lambda i: (0, 0)),
            pl.BlockSpec((half, D_MODEL), lambda i: (1, 0)),
            pl.BlockSpec((tm, D_MODEL), lambda i: (i, 0)),
            pl.BlockSpec((None, None, 1, D_MODEL), lambda i: ((i * tm) // rows_per_mod, gate_idx, 0, 0)),
        ],
        out_specs=pl.BlockSpec((tm, D_MODEL), lambda i: (i, 0)),
        out_shape=jax.ShapeDtypeStruct((t, D_MODEL), F32),
        compiler_params=_cparams("parallel"),
        name="outproj",
    )(mix_a, mix_b, w, w, x, mod)


FFN_TF = 512
FFN_ROWS = 512


def _ffn_kernel(x_ref, g_ref, sc_ref, sh_ref, gate_ref, wg_ref, wv_ref, cwg_ref, cwv_ref, cbg_ref, cbv_ref,
                wd_ref, o_ref, hb_ref, *, seq_len):
    f = pl.program_id(1)
    tm = x_ref.shape[0]

    @pl.when(f == 0)
    def _():
        _fill_normed(x_ref, g_ref, sc_ref, sh_ref, hb_ref)
        o_ref[...] = jnp.zeros_like(o_ref)

    rp = max(seq_len, FFN_ROWS)
    pos = lax.broadcasted_iota(jnp.int32, (rp, 1), 0) & (seq_len - 1)
    first = pos == 0
    last = pos == seq_len - 1

    def conv(a, cw_ref, cb_ref):
        prev = jnp.where(first, 0.0, pltpu.roll(a, 1, 0))
        nxt = jnp.where(last, 0.0, pltpu.roll(a, rp - 1, 0))
        return prev * cw_ref[0:1, :] + a * cw_ref[1:2, :] + nxt * cw_ref[2:3, :] + cb_ref[...]

    parts = [slice(r, r + rp) for r in range(0, tm, rp)]
    acts = []
    for rows in parts:
        hb = hb_ref[rows, :]
        cg = conv(_dot(hb, wg_ref[...]), cwg_ref, cbg_ref)
        cv = conv(_dot(hb, wv_ref[...]), cwv_ref, cbv_ref)
        acts.append((cg * _sigmoid(cg) * cv).astype(BF16))
    for rows, act in zip(parts, acts):
        o_ref[rows, :] += _dot(act, wd_ref[...])

    @pl.when(f == pl.num_programs(1) - 1)
    def _():
        o_ref[...] = x_ref[...] + gate_ref[...] * o_ref[...]


def _ffn(x, g, mod, w_up, conv_w, conv_b, w_down, seq_len):
    t = x.shape[0]
    tm, tf = DENSE_TM, FFN_TF
    nf = D_FF // tf
    rows_per_mod = t // mod.shape[0]
    conv_b = conv_b.reshape(1, 2 * D_FF)
    assert seq_len & (seq_len - 1) == 0 and tm % seq_len == 0
    return pl.pallas_call(
        functools.partial(_ffn_kernel, seq_len=seq_len),
        grid=(t // tm, nf),
        in_specs=[
            pl.BlockSpec((tm, D_MODEL), lambda i, f: (i, 0), pipeline_mode=pl.Buffered(1)),
            pl.BlockSpec((1, D_MODEL), lambda i, f: (0, 0)),
            _mod_spec(4, tm, rows_per_mod),
            _mod_spec(3, tm, rows_per_mod),
            _mod_spec(5, tm, rows_per_mod),
            pl.BlockSpec((D_MODEL, tf), lambda i, f: (0, f)),
            pl.BlockSpec((D_MODEL, tf), lambda i, f: (0, nf + f)),
            pl.BlockSpec((3, tf), lambda i, f: (0, f)),
            pl.BlockSpec((3, tf), lambda i, f: (0, nf + f)),
            pl.BlockSpec((1, tf), lambda i, f: (0, f)),
            pl.BlockSpec((1, tf), lambda i, f: (0, nf + f)),
            pl.BlockSpec((tf, D_MODEL), lambda i, f: (f, 0)),
        ],
        out_specs=pl.BlockSpec((tm, D_MODEL), lambda i, f: (i, 0)),
        out_shape=jax.ShapeDtypeStruct((t, D_MODEL), F32),
        scratch_shapes=[pltpu.VMEM((tm, D_MODEL), BF16)],
        compiler_params=_cparams("parallel", "arbitrary"),
        name="convffn",
    )(x, g.reshape(1, D_MODEL), mod, mod, mod, w_up, w_up, conv_w, conv_w, conv_b, conv_b, w_down)


def _attn_ctx_kernel(q_ref, k_ref, v_ref, gq_ref, gk_ref, o_ref, ko_ref, vo_ref):
    scale = DH ** -0.5
    for h in range(H_A):
        sl = slice(h * DH, (h + 1) * DH)
        q = _rms_lanes(q_ref[:, sl].astype(F32), gq_ref[...])
        k = _rms_lanes(k_ref[:, sl].astype(F32), gk_ref[...])
        v = v_ref[:, sl].astype(F32)
        ko_ref[h] = k
        vo_ref[h] = v
        s = _dot_nt(q.astype(BF16), k.astype(BF16)) * scale
        p = jnp.exp(s - jnp.max(s, axis=-1, keepdims=True))
        p = p / jnp.sum(p, axis=-1, keepdims=True)
        o_ref[:, sl] = _dot(p.astype(BF16), v.astype(BF16)).astype(o_ref.dtype)


def _attn_ctx(proj, gq, gk, batch, seq):
    t = proj.shape[0]
    kv_shape = jax.ShapeDtypeStruct((batch, 1, H_A, seq, DH), F32)
    kv_spec = pl.BlockSpec((None, None, H_A, seq, DH), lambda b: (b, 0, 0, 0, 0))
    return pl.pallas_call(
        _attn_ctx_kernel,
        grid=(batch,),
        in_specs=[
            pl.BlockSpec((seq, W_A), lambda b: (b, 0)),
            pl.BlockSpec((seq, W_A), lambda b: (b, 1)),
            pl.BlockSpec((seq, W_A), lambda b: (b, 2)),
            pl.BlockSpec((1, DH), lambda b: (0, 0)),
            pl.BlockSpec((1, DH), lambda b: (0, 0)),
        ],
        out_specs=[pl.BlockSpec((seq, W_A), lambda b: (b, 0)), kv_spec, kv_spec],
        out_shape=[jax.ShapeDtypeStruct((t, W_A), BF16), kv_shape, kv_shape],
        compiler_params=_cparams("parallel"),
        name="attn_ctx",
    )(proj, proj, proj, gq.reshape(1, DH), gk.reshape(1, DH))


NA_ROWS = 16
NA_PAIRS = 2 * WIN_H - 2
RPB_H = 2 * WIN_H - 1
RPB_W = 2 * WIN_W - 1


def _natten_kernel(rpb_ref, q_ref, k_ref, v_ref, kc_ref, vc_ref, gq_ref, gk_ref, o_ref,
                   qn_s, kn_s, v_s, kc_s, vc_s, bias_s):
    h = pl.program_id(1)
    scale = DH ** -0.5
    qn_s[...] = _rms_lanes(q_ref[...].astype(F32), gq_ref[...]).astype(BF16)
    kn_s[...] = _rms_lanes(k_ref[...].astype(F32), gk_ref[...]).astype(BF16)
    v_s[...] = v_ref[...].astype(BF16)
    kc_s[...] = kc_ref[...].astype(BF16)
    vc_s[...] = vc_ref[...].astype(BF16)

    lane = lax.broadcasted_iota(jnp.int32, (GRID_W, LANES), 1)
    qc = lax.broadcasted_iota(jnp.int32, (GRID_W, LANES), 0)
    kcol = lane % GRID_W
    upper = lane >= GRID_W
    dcol = kcol - qc + (WIN_W - 1)
    col0 = jnp.clip(qc - WIN_W // 2, 0, GRID_W - WIN_W)
    col_ok = (kcol >= col0) & (kcol < col0 + WIN_W)
    base = h * (RPB_H * RPB_W)

    def build(p, carry):
        acc = jnp.zeros((GRID_W, LANES), F32)
        for d in range(RPB_W):
            lo = rpb_ref[base + p * RPB_W + d]
            hi = rpb_ref[base + (p + 1) * RPB_W + d]
            acc = jnp.where(dcol == d, jnp.where(upper, hi, lo), acc)
        bias_s[p] = jnp.where(col_ok, acc, NEG_INF)
        return carry

    lax.fori_loop(0, NA_PAIRS, build, 0)

    kh = WIN_H
    n_loc = kh * GRID_W
    row0 = [min(max(r - kh // 2, 0), NA_ROWS - kh) for r in range(NA_ROWS)]
    q = [qn_s[r * GRID_W:(r + 1) * GRID_W, :] for r in range(NA_ROWS)]
    kc = kc_s[...]
    s_loc, s_ctx = [], []
    for r in range(NA_ROWS):
        dr0 = row0[r] - r + WIN_H - 1
        k_loc = kn_s[row0[r] * GRID_W:row0[r] * GRID_W + n_loc, :]
        bias = jnp.concatenate([bias_s[dr0 + 2 * i] for i in range(kh // 2)], axis=-1)
        s_loc.append(_dot_nt(q[r], k_loc) * scale + bias)
        s_ctx.append(_dot_nt(q[r], kc) * scale)
    p_loc, p_ctx = [], []
    for r in range(NA_ROWS):
        m = jnp.maximum(jnp.max(s_loc[r], axis=-1, keepdims=True), jnp.max(s_ctx[r], axis=-1, keepdims=True))
        e_loc = jnp.exp(s_loc[r] - m)
        e_ctx = jnp.exp(s_ctx[r] - m)
        l = jnp.sum(e_loc, axis=-1, keepdims=True) + jnp.sum(e_ctx, axis=-1, keepdims=True)
        p_loc.append((e_loc / l).astype(BF16))
        p_ctx.append((e_ctx / l).astype(BF16))
    vc = vc_s[...]
    for r in range(NA_ROWS):
        v_loc = v_s[row0[r] * GRID_W:row0[r] * GRID_W + n_loc, :]
        out = _dot(p_loc[r], v_loc) + _dot(p_ctx[r], vc)
        o_ref[r * GRID_W:(r + 1) * GRID_W, :] = out.astype(o_ref.dtype)


def _natten(proj, cache_k, cache_v, gq, gk, rpb, batch, seq):
    t = proj.shape[0]
    past = cache_k.shape[3]
    ctx_spec = pl.BlockSpec((None, None, None, past, DH), lambda b, h: (b, 0, h, 0, 0))
    return pl.pallas_call(
        _natten_kernel,
        grid=(batch, H_A),
        in_specs=[
            pl.BlockSpec(memory_space=pltpu.SMEM),
            pl.BlockSpec((seq, DH), lambda b, h: (b, h)),
            pl.BlockSpec((seq, DH), lambda b, h: (b, H_A + h)),
            pl.BlockSpec((seq, DH), lambda b, h: (b, 2 * H_A + h)),
            ctx_spec,
            ctx_spec,
            pl.BlockSpec((1, DH), lambda b, h: (0, 0)),
            pl.BlockSpec((1, DH), lambda b, h: (0, 0)),
        ],
        out_specs=pl.BlockSpec((seq, DH), lambda b, h: (b, h)),
        out_shape=jax.ShapeDtypeStruct((t, W_A), BF16),
        scratch_shapes=[
            pltpu.VMEM((seq, DH), BF16),
            pltpu.VMEM((seq, DH), BF16),
            pltpu.VMEM((seq, DH), BF16),
            pltpu.VMEM((past, DH), BF16),
            pltpu.VMEM((past, DH), BF16),
            pltpu.VMEM((NA_PAIRS, GRID_W, LANES), F32),
        ],
        compiler_params=_cparams("parallel", "arbitrary"),
        name="natten",
    )(rpb.reshape(-1), proj, proj, proj, cache_k, cache_v, gq.reshape(1, DH), gk.reshape(1, DH))


def _split3(x):
    hi = x.astype(BF16)
    r1 = x - hi.astype(F32)
    mid = r1.astype(BF16)
    lo = (r1 - mid.astype(F32)).astype(BF16)
    return hi, mid, lo


def _mlstm_kernel(*refs, has_state, emit_state, seq):
    it = iter(refs)
    q_ref, k_ref, v_ref, ob_ref, gt_ref, mlb_ref, mlg_ref = (next(it) for _ in range(7))
    if has_state:
        c0_ref, n0_ref, m0_ref = (next(it) for _ in range(3))
    o_ref = next(it)
    if emit_state:
        co_ref, no_ref, mo_ref = (next(it) for _ in range(3))
    r_s, rt_s, bt_s, hf_s, hb_s, ct_s, n_s, m_s = (next(it) for _ in range(8))

    L = ML_CHUNK
    nc = seq // L
    kscale = DH ** -0.5

    ri = lax.broadcasted_iota(jnp.int32, (L, L), 0)
    ci = lax.broadcasted_iota(jnp.int32, (L, L), 1)
    tri_pre = (ci <= ri).astype(BF16)
    tri_suf = (ci >= ri).astype(BF16)
    mask_t = (ri <= ci, ri >= ci)
    fwd_lane = lax.broadcasted_iota(jnp.int32, (L, LANES), 1) < 2 * H_B

    for c in range(nc):
        rows = slice(c * L, (c + 1) * L)
        g = gt_ref[rows, :] + mlb_ref[...]
        gf = jnp.minimum(g, 0.0) - jnp.log1p(jnp.exp(-jnp.abs(g)))
        hi, mid, lo = _split3(gf)
        pre = _dot(tri_pre, hi) + _dot(tri_pre, mid) + _dot(tri_pre, lo)
        suf = _dot(tri_suf, hi) + _dot(tri_suf, mid) + _dot(tri_suf, lo)
        b = pltpu.roll(jnp.where(fwd_lane, pre, suf), LANES - H_B, 1)
        r = g - b
        r_s[rows, :] = r
        rt_s[c] = r.T
        bt_s[c] = b.T

    for d in range(2):
        for h in range(H_B):
            ct_s[d, h] = c0_ref[d, h].T if has_state else jnp.zeros((DH, DH), F32)
    if has_state:
        n_s[...] = n0_ref[...]
        m_s[...] = m0_ref[...]
    else:
        n_s[...] = jnp.zeros_like(n_s)
        m_s[...] = jnp.zeros_like(m_s)

    pairs = [(d, h) for d in range(2) for h in range(H_B)]

    def step(c, carry):
        rows, r_all, rt_all, bt_all = [], [], [], []
        for d in range(2):
            cc = c if d == 0 else nc - 1 - c
            rows.append(pl.ds(pl.multiple_of(cc * L, L), L))
            r_all.append(r_s[rows[d], :])
            rt_all.append(rt_s[cc])
            bt_all.append(bt_s[cc])

        qb, kb, v_t, rm, r_row, b_row, a_row, qk = {}, {}, {}, {}, {}, {}, {}, {}
        for p in pairs:
            d, h = p
            col = 2 * H_B * d + h
            sl = slice(h * DH, (h + 1) * DH)
            qb[p] = q_ref[rows[d], sl].astype(BF16)
            kb[p] = (k_ref[rows[d], sl].astype(F32) * kscale).astype(BF16)
            v_t[p] = v_ref[rows[d], sl].astype(F32).T
            r_row[p] = rt_all[d][col:col + 1, :]
            b_row[p] = bt_all[d][col:col + 1, :]
            rm[p] = jnp.where(mask_t[d], jnp.broadcast_to(r_all[d][:, col:col + 1], (L, L)), NEG_INF)
            a_row[p] = jnp.max(rm[p], axis=0, keepdims=True)
            qk[p] = _dot_nt(kb[p], qb[p])

        m_row = {(d, h): m_s[d, h:h + 1, :] for d, h in pairs}
        n_row = {(d, h): n_s[d, h:h + 1, :] for d, h in pairs}
        ct = {(d, h): ct_s[d, h] for d, h in pairs}
        big_m = {p: jnp.maximum(m_row[p], a_row[p]) for p in pairs}
        w_in = {p: jnp.exp(m_row[p] - big_m[p]) for p in pairs}
        cq = {p: _dot_nt(ct[p].astype(BF16), qb[p]) for p in pairs}
        qn = {p: _dot_nt(jnp.broadcast_to(n_row[p], (8, DH)).astype(BF16), qb[p])[0:1, :] for p in pairs}
        w_t = {p: jnp.exp(rm[p] - big_m[p]) * qk[p] for p in pairs}
        pv = {p: _dot(v_t[p].astype(BF16), w_t[p].astype(BF16)) for p in pairs}
        for p in pairs:
            d, h = p
            den = w_in[p] * qn[p] + jnp.sum(w_t[p], axis=0, keepdims=True)
            h_t = (w_in[p] * cq[p] + pv[p]) / jnp.maximum(jnp.abs(den), jnp.exp(-(b_row[p] + big_m[p])))
            (hf_s if d == 0 else hb_s)[rows[d], slice(h * DH, (h + 1) * DH)] = h_t.T

        m_last, ws = {}, {}
        for p in pairs:
            m_last[p] = jnp.maximum(m_row[p], jnp.max(r_row[p], axis=-1, keepdims=True))
            ws[p] = jnp.exp(r_row[p] - m_last[p])
        kv = {p: _dot((v_t[p] * ws[p]).astype(BF16), kb[p]) for p in pairs}
        nk = {p: _dot(jnp.broadcast_to(ws[p], (8, L)).astype(BF16), kb[p])[0:1, :] for p in pairs}
        for p in pairs:
            d, h = p
            a_prev = jnp.exp(m_row[p] - m_last[p])
            bl = b_row[p][:, L - 1:L] if d == 0 else b_row[p][:, 0:1]
            ct_s[d, h] = a_prev * ct[p] + kv[p]
            n_s[d, h:h + 1, :] = a_prev * n_row[p] + nk[p]
            m_s[d, h:h + 1, :] = bl + m_last[p]
        return carry

    lax.fori_loop(0, nc, step, 0)

    if emit_state:
        for d in range(2):
            for h in range(H_B):
                co_ref[d, h] = ct_s[d, h].T
        no_ref[...] = n_s[...]
        mo_ref[...] = m_s[...]

    rows_ep = 256
    for r in range(seq // rows_ep):
        rr = slice(r * rows_ep, (r + 1) * rows_ep)
        for h in range(H_B):
            sl = slice(h * DH, (h + 1) * DH)
            hm = hf_s[rr, sl] + hb_s[rr, sl]
            ob = ob_ref[rr, sl].astype(F32)
            o_ref[rr, sl] = (_sigmoid(ob) * _rms_lanes(hm, mlg_ref[:, sl])).astype(o_ref.dtype)


def _mlstm(proj, gates, ml_b, ml_g, batch, seq, state=None, emit_state=False):
    t = proj.shape[0]
    col0 = 3 * W_A // W_B
    big = lambda j: pl.BlockSpec((seq, W_B), lambda b: (b, col0 + j))
    in_specs = [big(0), big(1), big(2), big(3),
                pl.BlockSpec((seq, LANES), lambda b: (b, 0)),
                pl.BlockSpec((1, LANES), lambda b: (0, 0)),
                pl.BlockSpec((1, W_B), lambda b: (0, 0))]
    args = [proj, proj, proj, proj, gates, ml_b, ml_g.reshape(1, W_B)]
    c_spec = pl.BlockSpec((None, None, 2, H_B, DH, DH), lambda b: (b, 0, 0, 0, 0, 0))
    n_spec = pl.BlockSpec((None, None, 2, H_B, DH), lambda b: (b, 0, 0, 0, 0))
    if state is not None:
        in_specs += [c_spec, n_spec, n_spec]
        args += list(state)
    out_specs = [pl.BlockSpec((seq, W_B), lambda b: (b, 0))]
    out_shape = [jax.ShapeDtypeStruct((t, W_B), BF16)]
    if emit_state:
        out_specs += [c_spec, n_spec, n_spec]
        out_shape += [jax.ShapeDtypeStruct((batch, 1, 2, H_B, DH, DH), F32),
                      jax.ShapeDtypeStruct((batch, 1, 2, H_B, DH), F32),
                      jax.ShapeDtypeStruct((batch, 1, 2, H_B, LANES), F32)]
    nc = seq // ML_CHUNK
    return pl.pallas_call(
        functools.partial(_mlstm_kernel, has_state=state is not None, emit_state=emit_state, seq=seq),
        grid=(batch,),
        in_specs=in_specs,
        out_specs=out_specs,
        out_shape=out_shape,
        scratch_shapes=[
            pltpu.VMEM((seq, LANES), F32),
            pltpu.VMEM((nc, ML_CHUNK, LANES), F32),
            pltpu.VMEM((nc, ML_CHUNK, LANES), F32),
            pltpu.VMEM((seq, W_B), F32),
            pltpu.VMEM((seq, W_B), F32),
            pltpu.VMEM((2, H_B, DH, DH), F32),
            pltpu.VMEM((2, H_B, DH), F32),
            pltpu.VMEM((2, H_B, LANES), F32),
        ],
        compiler_params=_cparams("parallel"),
        name="mlstm",
    )(*args)


DIFF_QB = 256


def _diff_kernel(*refs, heads, seq, past, rope, emit_ctx, lam_init):
    it = iter(refs)
    q_ref, k_ref, v_ref, gq_ref, gk_ref, go_ref, lam_ref = (next(it) for _ in range(7))
    if past:
        kc_ref, vc_ref = next(it), next(it)
    if rope:
        cos_ref, s1_ref, s2_ref = next(it), next(it), next(it)
    o_ref = next(it)
    if emit_ctx:
        ko_ref, vo_ref = next(it), next(it)

    scale = DQK ** -0.5
    lv = lam_ref[...]
    lam = (jnp.exp(jnp.sum(lv[0:1, :] * lv[1:2, :], axis=-1, keepdims=True))
           - jnp.exp(jnp.sum(lv[2:3, :] * lv[3:4, :], axis=-1, keepdims=True)) + lam_init)
    low = lax.broadcasted_iota(jnp.int32, (seq, LANES), 1) < DQK
    hr = lax.broadcasted_iota(jnp.int32, (LANES, LANES), 0) < DQK
    hc = lax.broadcasted_iota(jnp.int32, (LANES, LANES), 1) < DQK
    halves = (hr == hc).astype(BF16)

    def apply_rope(x):
        return (x * cos_ref[...] + pltpu.roll(x, LANES - DQK // 4, 1) * s1_ref[...]
                + pltpu.roll(x, DQK // 4, 1) * s2_ref[...])

    qb_rows = min(DIFF_QB, seq)
    blocks = [slice(i * qb_rows, (i + 1) * qb_rows) for i in range(seq // qb_rows)]
    head_sl = [slice(hh * DH, (hh + 1) * DH) for hh in range(heads)]
    qn = [_rms_lane_groups(q_ref[:, sl].astype(F32), gq_ref[...], halves, DQK) for sl in head_sl]
    kn = [_rms_lane_groups(k_ref[:, sl].astype(F32), gk_ref[...], halves, DQK) for sl in head_sl]
    v = [v_ref[:, sl].astype(F32) for sl in head_sl]
    if emit_ctx:
        for hh in range(heads):
            ko_ref[hh] = kn[hh]
            vo_ref[hh] = v[hh]
    if rope:
        qn = [apply_rope(x) for x in qn]
        kn = [apply_rope(x) for x in kn]
    assert math.frexp(scale)[0] == 0.5
    q0 = [(jnp.where(low, x, 0.0) * scale).astype(BF16) for x in qn]
    q1 = [(jnp.where(low, 0.0, x) * scale).astype(BF16) for x in qn]
    ka, vt = [], []
    for hh in range(heads):
        k_parts = [kn[hh].astype(BF16)]
        vt_parts = [v[hh][c * LANES:(c + 1) * LANES, :].T.astype(BF16) for c in range(seq // LANES)]
        if past:
            k_parts.append(kc_ref[hh].astype(BF16))
            vt_parts += [vc_ref[hh, c * LANES:(c + 1) * LANES, :].T.astype(BF16) for c in range(past // LANES)]
        ka.append(jnp.concatenate(k_parts, axis=0))
        vt.append(jnp.concatenate(vt_parts, axis=1))

    work = [(hh, rows) for hh in range(heads) for rows in blocks]
    s0 = [_dot_nt(ka[hh], q0[hh][rows, :]) for hh, rows in work]
    s1 = [_dot_nt(ka[hh], q1[hh][rows, :]) for hh, rows in work]

    def softmax_t(s, weight):
        p = jnp.exp(s - jnp.max(s, axis=0, keepdims=True))
        return p * (weight / jnp.sum(p, axis=0, keepdims=True))

    a_t = [(softmax_t(a, 1.0) - softmax_t(b, lam)).astype(BF16) for a, b in zip(s0, s1)]
    out_t = [_dot(vt[hh], a) for (hh, _), a in zip(work, a_t)]
    for (hh, rows), o in zip(work, out_t):
        o = o * lax.rsqrt(jnp.mean(o * o, axis=0, keepdims=True) + EPS)
        g_row = go_ref[:, head_sl[hh]] * (1.0 - lam_init)
        for c in range(qb_rows // LANES):
            r0 = rows.start + c * LANES
            o_ref[r0:r0 + LANES, head_sl[hh]] = (o[:, c * LANES:(c + 1) * LANES].T * g_row).astype(o_ref.dtype)


def _diff_attn(proj, gq, gk, g_out, lam_vec, lam_init, batch, seq, heads, cache=None, rope=None,
               emit_ctx=False):
    t = proj.shape[0]
    nh = H_C // heads
    w = heads * DH
    kblk = W_CQK // w
    past = 0 if cache is None else cache[0].shape[3]
    in_specs = [
        pl.BlockSpec((seq, w), lambda b, j: (b, j)),
        pl.BlockSpec((seq, w), lambda b, j: (b, kblk + j)),
        pl.BlockSpec((seq, w), lambda b, j: (b, 2 * kblk + j)),
        pl.BlockSpec((1, DH), lambda b, j: (0, 0)),
        pl.BlockSpec((1, DH), lambda b, j: (0, 0)),
        pl.BlockSpec((1, w), lambda b, j: (0, j)),
        pl.BlockSpec((4, DQK), lambda b, j: (0, 0)),
    ]
    args = [proj, proj, proj, jnp.tile(gq, 2).reshape(1, DH), jnp.tile(gk, 2).reshape(1, DH),
            g_out.reshape(1, W_CV), lam_vec]
    if cache is not None:
        ctx_spec = pl.BlockSpec((None, None, heads, past, DH), lambda b, j: (b, 0, j, 0, 0))
        in_specs += [ctx_spec, ctx_spec]
        args += [cache[0].reshape(cache[0].shape[:4] + (DH,)), cache[1]]
    if rope is not None:
        tab = pl.BlockSpec((seq, DH), lambda b, j: (0, 0))
        in_specs += [tab, tab, tab]
        args += list(rope)
    out_specs = [pl.BlockSpec((seq, w), lambda b, j: (b, j))]
    out_shape = [jax.ShapeDtypeStruct((t, W_CV), BF16)]
    if emit_ctx:
        kv_spec = pl.BlockSpec((None, None, heads, seq, DH), lambda b, j: (b, 0, j, 0, 0))
        kv_shape = jax.ShapeDtypeStruct((batch, 1, H_C, seq, DH), F32)
        out_specs += [kv_spec, kv_spec]
        out_shape += [kv_shape, kv_shape]
    return pl.pallas_call(
        functools.partial(_diff_kernel, heads=heads, seq=seq, past=past, rope=rope is not None,
                          emit_ctx=emit_ctx, lam_init=lam_init),
        grid=(batch, nh),
        in_specs=in_specs,
        out_specs=out_specs,
        out_shape=out_shape,
        compiler_params=_cparams("parallel", "arbitrary"),
        name="diff_attn",
    )(*args)


def _rope_tables(seq):
    tpos = jnp.arange(seq)
    pos = jnp.stack([tpos // GRID_W, tpos % GRID_W], axis=-1).astype(F32)
    half = DQK // 2
    inv = ROPE_THETA ** (-jnp.arange(0, half, 2, dtype=F32) / half)
    ang = pos[:, :, None] * inv
    cos = jnp.cos(ang)
    sin = jnp.sin(ang)
    zero = jnp.zeros_like(sin)
    lay = lambda first, second: jnp.tile(jnp.concatenate([first, second], axis=-1).reshape(seq, DQK), (1, 2))
    return lay(cos, cos), lay(-sin, zero), lay(zero, sin)


SGU_TM = 512


def _sgu_kernel(u_ref, vd_ref, sgg_ref, sgw_ref, sgb_ref, o_ref):
    tm = u_ref.shape[0]
    for c in range(tm // SG_CHUNK):
        rows = slice(c * SG_CHUNK, (c + 1) * SG_CHUNK)
        vn = _rms_lanes(_gelu_tanh(vd_ref[rows, :].astype(F32)), sgg_ref[...]).astype(BF16)
        for g in range(G_D):
            sl = slice(g * DH, (g + 1) * DH)
            gate = _dot(sgw_ref[g].astype(BF16), vn[:, sl]) + sgb_ref[:, g:g + 1]
            o_ref[rows, sl] = (_gelu_tanh(u_ref[rows, sl].astype(F32)) * gate).astype(o_ref.dtype)


def _sgu(proj, sg_g, sg_w, sg_b):
    t = proj.shape[0]
    tm = SGU_TM
    ublk = (2 * W_CQK + W_CV) // W_D
    return pl.pallas_call(
        _sgu_kernel,
        grid=(t // tm,),
        in_specs=[
            pl.BlockSpec((tm, W_D), lambda i: (i, ublk)),
            pl.BlockSpec((tm, W_D), lambda i: (i, ublk + 1)),
            pl.BlockSpec((1, W_D), lambda i: (0, 0)),
            pl.BlockSpec((G_D, SG_CHUNK, SG_CHUNK), lambda i: (0, 0, 0)),
            pl.BlockSpec((SG_CHUNK, G_D), lambda i: (0, 0)),
        ],
        out_specs=pl.BlockSpec((tm, W_D), lambda i: (i, 0)),
        out_shape=jax.ShapeDtypeStruct((t, W_D), BF16),
        compiler_params=_cparams("parallel"),
        name="sgu",
    )(proj, proj, sg_g.reshape(1, W_D), sg_w, sg_b.T)


def _lambda_init(layer):
    return 0.8 - 0.6 * math.exp(-0.3 * layer)


def _trunk(x, mods, p, batch, seq, caches):
    ctx_out = {}
    for l in range(DEPTH):
        mod = mods[l]
        if l % 2 == 0:
            proj, gates = _inproj(x, p["g_mix"][l], mod, 1, 0, p["w_even"], N_EVEN_MAIN, N_EVEN_MAIN // LANES)
            if caches is None:
                mix_a, ctx_out["na_k"], ctx_out["na_v"] = _attn_ctx(proj, p["na_gq"], p["na_gk"], batch, seq)
                mix_b, c1, n1, m1 = _mlstm(proj, gates, p["ml_b"], p["ml_g"], batch, seq, emit_state=True)
                ctx_out.update(mlstm_C=c1, mlstm_n=n1, mlstm_m=m1[..., 0])
            else:
                mix_a = _natten(proj, caches["na_k"], caches["na_v"], p["na_gq"], p["na_gk"], p["na_rpb"], batch, seq)
                (mix_b,) = _mlstm(proj, gates, p["ml_b"], p["ml_g"], batch, seq,
                                  state=(caches["C"], caches["n"], caches["m"]))
        else:
            proj = _inproj(x, p["g_mix"][l], mod, 1, 0, p["w_odd"], p["w_odd"].shape[1])
            lam_init = _lambda_init(l)
            if caches is None:
                mix_a, dk, dv = _diff_attn(proj, p["diff_gq"], p["diff_gk"], p["diff_g_out"], p["diff_lam"],
                                           lam_init, batch, seq, heads=H_C, emit_ctx=True)
                ctx_out["diff_k"] = dk.reshape(batch, 1, H_C, seq, 2, DQK)
                ctx_out["diff_v"] = dv
            else:
                (mix_a,) = _diff_attn(proj, p["diff_gq"], p["diff_gk"], p["diff_g_out"], p["diff_lam"],
                                      lam_init, batch, seq, heads=1,
                                      cache=(caches["diff_k"], caches["diff_v"]), rope=_rope_tables(seq))
            mix_b = _sgu(proj, p["sg_g"], p["sg_w"], p["sg_b"])
        x = _outproj(mix_a, mix_b, p["w_out"][l], x, mod, 2)
        x = _ffn(x, p["g_ffn"][l], mod, p["w_up"][l], p["conv_w"][l], p["conv_b"][l], p["w_down"][l], seq)
    return x, ctx_out


def kernel(x_prompt, x_sample, cache_na_k, cache_na_v, state_mlstm_C, state_mlstm_n, state_mlstm_m,
           cache_diff_k, cache_diff_v, c, c_ctx, w_mod, b_mod, g_mix, g_ffn, w_out, w_in_even,
           na_gq, na_gk, na_rpb, ml_b_gates, ml_g_out, w_in_odd, diff_gq, diff_gk, diff_lam,
           diff_g_out, sg_g_v, sg_w, sg_b, ffn_w_up, ffn_conv_w, ffn_conv_b, ffn_w_down):
    batch, seq, _ = x_prompt.shape
    dbatch, dseq, _ = x_sample.shape

    cond8 = jnp.zeros((8, D_MODEL), F32).at[0].set(c_ctx).at[1:1 + dbatch].set(c)
    mod = _modulation(cond8, w_mod, b_mod)
    mods_ctx = [mod[l, 0:1].reshape(1, 6, 1, D_MODEL) for l in range(DEPTH)]
    mods_lat = [mod[l, 1:1 + dbatch].reshape(dbatch, 6, 1, D_MODEL) for l in range(DEPTH)]

    p = dict(
        g_mix=g_mix, g_ffn=g_ffn,
        w_even=jnp.pad(w_in_even[0], ((0, 0), (0, LANES - N_GATES))).astype(BF16),
        w_odd=w_in_odd[0].astype(BF16),
        w_out=[w_out[l].astype(BF16) for l in range(DEPTH)],
        w_up=[ffn_w_up[l].astype(BF16) for l in range(DEPTH)],
        w_down=[ffn_w_down[l].astype(BF16) for l in range(DEPTH)],
        conv_w=ffn_conv_w, conv_b=ffn_conv_b,
        na_gq=na_gq[0], na_gk=na_gk[0], na_rpb=na_rpb[0],
        ml_b=jnp.pad(ml_b_gates[0], (0, LANES - N_GATES)).reshape(1, LANES), ml_g=ml_g_out[0],
        diff_gq=diff_gq[0], diff_gk=diff_gk[0], diff_lam=diff_lam[0], diff_g_out=diff_g_out[0],
        sg_g=sg_g_v[0], sg_w=sg_w[0], sg_b=sg_b[0],
    )

    y_prompt, ctx = _trunk(x_prompt.reshape(batch * seq, D_MODEL), mods_ctx, p, batch, seq, None)
    caches = dict(
        na_k=cache_na_k, na_v=cache_na_v, C=state_mlstm_C, n=state_mlstm_n,
        m=jnp.broadcast_to(state_mlstm_m[..., None], state_mlstm_m.shape + (LANES,)),
        diff_k=cache_diff_k, diff_v=cache_diff_v,
    )
    y_sample, _ = _trunk(x_sample.reshape(dbatch * dseq, D_MODEL), mods_lat, p, dbatch, dseq, caches)
    return (y_prompt.reshape(batch, seq, D_MODEL), y_sample.reshape(dbatch, dseq, D_MODEL),
            ctx["na_k"], ctx["na_v"], ctx["mlstm_C"], ctx["mlstm_n"], ctx["mlstm_m"],
            ctx["diff_k"], ctx["diff_v"])
```

```python
import functools
import math

import numpy as np
import jax
import jax.numpy as jnp
from jax import lax
from jax.experimental import pallas as pl
from jax.experimental.pallas import tpu as pltpu

F32 = jnp.float32
BF16 = jnp.bfloat16

D_MODEL = 2048
DEPTH = 2
GRID_W = 64
DH = 128
H_A = 8
W_A = H_A * DH
WIN_H = 8
WIN_W = 16
H_B = 8
W_B = H_B * DH
ML_CHUNK = 128
H_C = 8
DQK = 64
W_CQK = H_C * 2 * DQK
W_CV = H_C * DH
ROPE_THETA = 10000.0
G_D = 8
SG_CHUNK = 128
W_D = 1024
D_FF = 5632
Q_BLOCK = 128
EPS = 1e-6
N_GATES = 4 * H_B
N_EVEN_MAIN = 3 * W_A + 4 * W_B

LANES = 128
V7X_VMEM_BYTES = 64 * 1024 * 1024
VMEM_LIMIT = V7X_VMEM_BYTES - 8 * 1024 * 1024

NEG_INF = float("-inf")


def _cparams(*sem):
    return pltpu.CompilerParams(dimension_semantics=sem, vmem_limit_bytes=VMEM_LIMIT)


def _dot(a, b):
    return jnp.dot(a, b, preferred_element_type=F32)


def _dot_nt(a, b):
    return lax.dot_general(a, b, (((1,), (1,)), ((), ())), preferred_element_type=F32)


def _rms_lanes(x, g):
    ms = jnp.mean(x * x, axis=-1, keepdims=True)
    return x * lax.rsqrt(ms + EPS) * g


def _rms_lane_groups(x, g, group_ones, width):
    y = x * x
    hi = y.astype(BF16)
    mid = (y - hi.astype(F32)).astype(BF16)
    ss = _dot(hi, group_ones) + _dot(mid, group_ones)
    return x * lax.rsqrt(ss * (1.0 / width) + EPS) * g


def _norm_mod(x, g, scale, shift):
    return _rms_lanes(x, g) * (1.0 + scale) + shift


def _sigmoid(x):
    return 1.0 / (1.0 + jnp.exp(-x))


def _gelu_tanh(x):
    c = math.sqrt(2.0 / math.pi)
    return 0.5 * x * (1.0 + jnp.tanh(c * (x + 0.044715 * (x * x * x))))


MOD_TN = 1024


def _mod_kernel(c_ref, w_ref, b_ref, o_ref):
    c = c_ref[...]
    s = c * _sigmoid(c)
    o_ref[...] = _dot(s.astype(BF16), w_ref[...].astype(BF16)) + b_ref[...]


def _modulation(cond8, w_mod, b_mod):
    n = 6 * D_MODEL
    return pl.pallas_call(
        _mod_kernel,
        grid=(DEPTH, n // MOD_TN),
        in_specs=[
            pl.BlockSpec((8, D_MODEL), lambda l, j: (0, 0)),
            pl.BlockSpec((None, D_MODEL, MOD_TN), lambda l, j: (l, 0, j)),
            pl.BlockSpec((None, 1, MOD_TN), lambda l, j: (l, 0, j)),
        ],
        out_specs=pl.BlockSpec((None, 8, MOD_TN), lambda l, j: (l, 0, j)),
        out_shape=jax.ShapeDtypeStruct((DEPTH, 8, n), F32),
        compiler_params=_cparams("parallel", "arbitrary"),
        name="modulation",
    )(cond8, w_mod, b_mod.reshape(DEPTH, 1, n))


DENSE_TM = 1024
NORM_ROWS = 256
PROJ_DTYPE = BF16


def _inproj_body(x_ref, g_ref, sc_ref, sh_ref, w_ref, o_ref, hb_ref, wg_ref=None, og_ref=None):
    j = pl.program_id(1)
    tm = x_ref.shape[0]

    @pl.when(j == 0)
    def _():
        for c in range(tm // NORM_ROWS):
            r = slice(c * NORM_ROWS, (c + 1) * NORM_ROWS)
            h = _norm_mod(x_ref[r, :], g_ref[...], sc_ref[...], sh_ref[...]).astype(BF16)
            hb_ref[r, :] = h
            o_ref[r, :] = _dot(h, w_ref[...]).astype(o_ref.dtype)
            if og_ref is not None:
                og_ref[r, :] = _dot(h, wg_ref[...])

    @pl.when(j > 0)
    def _():
        o_ref[...] = _dot(hb_ref[...], w_ref[...]).astype(o_ref.dtype)


def _inproj_kernel(x_ref, g_ref, sc_ref, sh_ref, w_ref, o_ref, hb_ref):
    _inproj_body(x_ref, g_ref, sc_ref, sh_ref, w_ref, o_ref, hb_ref)


def _inproj_gates_kernel(x_ref, g_ref, sc_ref, sh_ref, w_ref, wg_ref, o_ref, og_ref, hb_ref):
    _inproj_body(x_ref, g_ref, sc_ref, sh_ref, w_ref, o_ref, hb_ref, wg_ref, og_ref)


def _mod_spec(idx, tm, rows_per_mod):
    return pl.BlockSpec((None, None, 1, D_MODEL), lambda i, n: ((i * tm) // rows_per_mod, idx, 0, 0))


def _inproj(x, g, mod, scale_idx, shift_idx, w, n, gates_block=None, tn=1024):
    t = x.shape[0]
    tm = DENSE_TM
    rows_per_mod = t // mod.shape[0]
    in_specs = [
        pl.BlockSpec((tm, D_MODEL), lambda i, j: (i, 0)),
        pl.BlockSpec((1, D_MODEL), lambda i, j: (0, 0)),
        _mod_spec(scale_idx, tm, rows_per_mod),
        _mod_spec(shift_idx, tm, rows_per_mod),
        pl.BlockSpec((D_MODEL, tn), lambda i, j: (0, j)),
    ]
    out_specs = pl.BlockSpec((tm, tn), lambda i, j: (i, j))
    out_shape = jax.ShapeDtypeStruct((t, n), PROJ_DTYPE)
    args = [x, g.reshape(1, D_MODEL), mod, mod, w]
    kern = _inproj_kernel
    if gates_block is not None:
        in_specs.append(pl.BlockSpec((D_MODEL, LANES), lambda i, j: (0, gates_block)))
        out_specs = [out_specs, pl.BlockSpec((tm, LANES), lambda i, j: (i, 0))]
        out_shape = [out_shape, jax.ShapeDtypeStruct((t, LANES), F32)]
        args.append(w)
        kern = _inproj_gates_kernel
    return pl.pallas_call(
        kern,
        grid=(t // tm, n // tn),
        in_specs=in_specs,
        out_specs=out_specs,
        out_shape=out_shape,
        scratch_shapes=[pltpu.VMEM((tm, D_MODEL), BF16)],
        compiler_params=_cparams("parallel", "arbitrary"),
        name="inproj",
    )(*args)


OUT_TM = 512


def _outproj_kernel(a_ref, b_ref, wa_ref, wb_ref, x_ref, gate_ref, o_ref):
    acc = _dot(a_ref[...], wa_ref[...]) + _dot(b_ref[...], wb_ref[...])
    o_ref[...] = x_ref[...] + gate_ref[...] * acc


def _outproj(mix_a, mix_b, w, layer, x, mod, gate_idx):
    t = x.shape[0]
    tm = OUT_TM
    half = mix_a.shape[1]
    rows_per_mod = t // mod.shape[0]
    return pl.pallas_call(
        _outproj_kernel,
        grid=(t // tm,),
        in_specs=[
            pl.BlockSpec((tm, half), lambda i: (i, 0)),
            pl.BlockSpec((tm, half), lambda i: (i, 0)),
            pl.BlockSpec((None, half, D_MODEL), lambda i: (layer, 0, 0)),
            pl.BlockSpec((None, half, D_MODEL), lambda i: (layer, 1, 0)),
            pl.BlockSpec((tm, D_MODEL), lambda i: (i, 0)),
            pl.BlockSpec((None, None, 1, D_MODEL), lambda i: ((i * tm) // rows_per_mod, gate_idx, 0, 0)),
        ],
        out_specs=pl.BlockSpec((tm, D_MODEL), lambda i: (i, 0)),
        out_shape=jax.ShapeDtypeStruct((t, D_MODEL), F32),
        compiler_params=_cparams("parallel"),
        name="outproj",
    )(mix_a, mix_b, w, w, x, mod)


FFN_TF = 512
FFN_ROWS = 512


def _ffn_kernel(x_ref, g_ref, sc_ref, sh_ref, gate_ref, wg_ref, wv_ref, cwg_ref, cwv_ref, cbg_ref, cbv_ref,
                wd_ref, o_ref, hb_ref, *, seq_len):
    f = pl.program_id(1)
    tm = x_ref.shape[0]

    rp = max(seq_len, FFN_ROWS)
    pos = lax.broadcasted_iota(jnp.int32, (rp, 1), 0) & (seq_len - 1)
    first = pos == 0
    last = pos == seq_len - 1

    def conv(a, cw_ref, cb_ref):
        prev = jnp.where(first, 0.0, pltpu.roll(a, 1, 0))
        nxt = jnp.where(last, 0.0, pltpu.roll(a, rp - 1, 0))
        return prev * cw_ref[0:1, :] + a * cw_ref[1:2, :] + nxt * cw_ref[2:3, :] + cb_ref[...]

    parts = [slice(r, r + rp) for r in range(0, tm, rp)]

    def hidden_block(first):
        acts = []
        for rows in parts:
            if first:
                for c in range(rows.start, rows.stop, NORM_ROWS):
                    r = slice(c, c + NORM_ROWS)
                    h = _norm_mod(x_ref[r, :], g_ref[...], sc_ref[...], sh_ref[...])
                    hb_ref[r, :] = h.astype(BF16)
            hb = hb_ref[rows, :]
            cg = conv(_dot(hb, wg_ref[...]), cwg_ref, cbg_ref)
            cv = conv(_dot(hb, wv_ref[...]), cwv_ref, cbv_ref)
            acts.append((cg * _sigmoid(cg) * cv).astype(BF16))
        for rows, act in zip(parts, acts):
            if first:
                o_ref[rows, :] = _dot(act, wd_ref[...])
            else:
                o_ref[rows, :] += _dot(act, wd_ref[...])

    @pl.when(f == 0)
    def _():
        hidden_block(True)

    @pl.when(f > 0)
    def _():
        hidden_block(False)

    @pl.when(f == pl.num_programs(1) - 1)
    def _():
        o_ref[...] = x_ref[...] + gate_ref[...] * o_ref[...]


def _ffn(x, g, mod, w_up, conv_w, conv_b, w_down, layer, seq_len):
    t = x.shape[0]
    tm, tf = DENSE_TM, FFN_TF
    nf = D_FF // tf
    rows_per_mod = t // mod.shape[0]
    conv_b = conv_b.reshape(1, 2 * D_FF)
    assert seq_len & (seq_len - 1) == 0 and tm % seq_len == 0
    return pl.pallas_call(
        functools.partial(_ffn_kernel, seq_len=seq_len),
        grid=(t // tm, nf),
        in_specs=[
            pl.BlockSpec((tm, D_MODEL), lambda i, f: (i, 0), pipeline_mode=pl.Buffered(1)),
            pl.BlockSpec((1, D_MODEL), lambda i, f: (0, 0)),
            _mod_spec(4, tm, rows_per_mod),
            _mod_spec(3, tm, rows_per_mod),
            _mod_spec(5, tm, rows_per_mod),
            pl.BlockSpec((None, D_MODEL, tf), lambda i, f: (layer, 0, f)),
            pl.BlockSpec((None, D_MODEL, tf), lambda i, f: (layer, 0, nf + f)),
            pl.BlockSpec((3, tf), lambda i, f: (0, f)),
            pl.BlockSpec((3, tf), lambda i, f: (0, nf + f)),
            pl.BlockSpec((1, tf), lambda i, f: (0, f)),
            pl.BlockSpec((1, tf), lambda i, f: (0, nf + f)),
            pl.BlockSpec((None, tf, D_MODEL), lambda i, f: (layer, f, 0)),
        ],
        out_specs=pl.BlockSpec((tm, D_MODEL), lambda i, f: (i, 0)),
        out_shape=jax.ShapeDtypeStruct((t, D_MODEL), F32),
        scratch_shapes=[pltpu.VMEM((tm, D_MODEL), BF16)],
        compiler_params=_cparams("parallel", "arbitrary"),
        name="convffn",
    )(x, g.reshape(1, D_MODEL), mod, mod, mod, w_up, w_up, conv_w, conv_w, conv_b, conv_b, w_down)


def _attn_ctx_kernel(q_ref, k_ref, v_ref, gq_ref, gk_ref, o_ref, ko_ref, vo_ref):
    scale = DH ** -0.5
    for h in range(H_A):
        sl = slice(h * DH, (h + 1) * DH)
        q = _rms_lanes(q_ref[:, sl].astype(F32), gq_ref[...])
        k = _rms_lanes(k_ref[:, sl].astype(F32), gk_ref[...])
        v = v_ref[:, sl].astype(F32)
        ko_ref[h] = k
        vo_ref[h] = v
        s = _dot_nt(q.astype(BF16), k.astype(BF16)) * scale
        p = jnp.exp(s - jnp.max(s, axis=-1, keepdims=True))
        p = p / jnp.sum(p, axis=-1, keepdims=True)
        o_ref[:, sl] = _dot(p.astype(BF16), v.astype(BF16)).astype(o_ref.dtype)


def _attn_ctx(proj, gq, gk, batch, seq):
    t = proj.shape[0]
    kv_shape = jax.ShapeDtypeStruct((batch, 1, H_A, seq, DH), F32)
    kv_spec = pl.BlockSpec((None, None, H_A, seq, DH), lambda b: (b, 0, 0, 0, 0))
    return pl.pallas_call(
        _attn_ctx_kernel,
        grid=(batch,),
        in_specs=[
            pl.BlockSpec((seq, W_A), lambda b: (b, 0)),
            pl.BlockSpec((seq, W_A), lambda b: (b, 1)),
            pl.BlockSpec((seq, W_A), lambda b: (b, 2)),
            pl.BlockSpec((1, DH), lambda b: (0, 0)),
            pl.BlockSpec((1, DH), lambda b: (0, 0)),
        ],
        out_specs=[pl.BlockSpec((seq, W_A), lambda b: (b, 0)), kv_spec, kv_spec],
        out_shape=[jax.ShapeDtypeStruct((t, W_A), BF16), kv_shape, kv_shape],
        compiler_params=_cparams("parallel"),
        name="attn_ctx",
    )(proj, proj, proj, gq.reshape(1, DH), gk.reshape(1, DH))


NA_ROWS = 16
NA_PAIRS = 2 * WIN_H - 2
RPB_H = 2 * WIN_H - 1
RPB_W = 2 * WIN_W - 1


def _natten_kernel(rpb_ref, q_ref, k_ref, v_ref, kc_ref, vc_ref, gq_ref, gk_ref, o_ref,
                   qn_s, kn_s, v_s, kc_s, vc_s, bias_s):
    h = pl.program_id(1)
    scale = DH ** -0.5
    qn_s[...] = _rms_lanes(q_ref[...].astype(F32), gq_ref[...]).astype(BF16)
    kn_s[...] = _rms_lanes(k_ref[...].astype(F32), gk_ref[...]).astype(BF16)
    v_s[...] = v_ref[...].astype(BF16)
    kc_s[...] = kc_ref[...].astype(BF16)
    vc_s[...] = vc_ref[...].astype(BF16)

    lane = lax.broadcasted_iota(jnp.int32, (GRID_W, LANES), 1)
    qc = lax.broadcasted_iota(jnp.int32, (GRID_W, LANES), 0)
    kcol = lane % GRID_W
    upper = lane >= GRID_W
    dcol = kcol - qc + (WIN_W - 1)
    col0 = jnp.clip(qc - WIN_W // 2, 0, GRID_W - WIN_W)
    col_ok = (kcol >= col0) & (kcol < col0 + WIN_W)
    base = h * (RPB_H * RPB_W)

    def build(p, carry):
        acc = jnp.zeros((GRID_W, LANES), F32)
        for d in range(RPB_W):
            lo = rpb_ref[base + p * RPB_W + d]
            hi = rpb_ref[base + (p + 1) * RPB_W + d]
            acc = jnp.where(dcol == d, jnp.where(upper, hi, lo), acc)
        bias_s[p] = jnp.where(col_ok, acc, NEG_INF)
        return carry

    lax.fori_loop(0, NA_PAIRS, build, 0)

    kh = WIN_H
    n_loc = kh * GRID_W
    row0 = [min(max(r - kh // 2, 0), NA_ROWS - kh) for r in range(NA_ROWS)]
    q = [qn_s[r * GRID_W:(r + 1) * GRID_W, :] for r in range(NA_ROWS)]
    kc = kc_s[...]
    s_loc, s_ctx = [], []
    for r in range(NA_ROWS):
        dr0 = row0[r] - r + WIN_H - 1
        k_loc = kn_s[row0[r] * GRID_W:row0[r] * GRID_W + n_loc, :]
        bias = jnp.concatenate([bias_s[dr0 + 2 * i] for i in range(kh // 2)], axis=-1)
        s_loc.append(_dot_nt(q[r], k_loc) * scale + bias)
        s_ctx.append(_dot_nt(q[r], kc) * scale)
    p_loc, p_ctx = [], []
    for r in range(NA_ROWS):
        m = jnp.maximum(jnp.max(s_loc[r], axis=-1, keepdims=True), jnp.max(s_ctx[r], axis=-1, keepdims=True))
        e_loc = jnp.exp(s_loc[r] - m)
        e_ctx = jnp.exp(s_ctx[r] - m)
        l = jnp.sum(e_loc, axis=-1, keepdims=True) + jnp.sum(e_ctx, axis=-1, keepdims=True)
        p_loc.append((e_loc / l).astype(BF16))
        p_ctx.append((e_ctx / l).astype(BF16))
    vc = vc_s[...]
    for r in range(NA_ROWS):
        v_loc = v_s[row0[r] * GRID_W:row0[r] * GRID_W + n_loc, :]
        out = _dot(p_loc[r], v_loc) + _dot(p_ctx[r], vc)
        o_ref[r * GRID_W:(r + 1) * GRID_W, :] = out.astype(o_ref.dtype)


def _natten(proj, cache_k, cache_v, gq, gk, rpb, batch, seq):
    t = proj.shape[0]
    past = cache_k.shape[3]
    ctx_spec = pl.BlockSpec((None, None, None, past, DH), lambda b, h: (b, 0, h, 0, 0))
    return pl.pallas_call(
        _natten_kernel,
        grid=(batch, H_A),
        in_specs=[
            pl.BlockSpec(memory_space=pltpu.SMEM),
            pl.BlockSpec((seq, DH), lambda b, h: (b, h)),
            pl.BlockSpec((seq, DH), lambda b, h: (b, H_A + h)),
            pl.BlockSpec((seq, DH), lambda b, h: (b, 2 * H_A + h)),
            ctx_spec,
            ctx_spec,
            pl.BlockSpec((1, DH), lambda b, h: (0, 0)),
            pl.BlockSpec((1, DH), lambda b, h: (0, 0)),
        ],
        out_specs=pl.BlockSpec((seq, DH), lambda b, h: (b, h)),
        out_shape=jax.ShapeDtypeStruct((t, W_A), BF16),
        scratch_shapes=[
            pltpu.VMEM((seq, DH), BF16),
            pltpu.VMEM((seq, DH), BF16),
            pltpu.VMEM((seq, DH), BF16),
            pltpu.VMEM((past, DH), BF16),
            pltpu.VMEM((past, DH), BF16),
            pltpu.VMEM((NA_PAIRS, GRID_W, LANES), F32),
        ],
        compiler_params=_cparams("parallel", "arbitrary"),
        name="natten",
    )(rpb.reshape(-1), proj, proj, proj, cache_k, cache_v, gq.reshape(1, DH), gk.reshape(1, DH))


def _split3(x):
    hi = x.astype(BF16)
    r1 = x - hi.astype(F32)
    mid = r1.astype(BF16)
    lo = (r1 - mid.astype(F32)).astype(BF16)
    return hi, mid, lo


def _mlstm_kernel(*refs, has_state, emit_state, seq):
    it = iter(refs)
    q_ref, k_ref, v_ref, ob_ref, gt_ref, mlb_ref, mlg_ref = (next(it) for _ in range(7))
    if has_state:
        c0_ref, n0_ref, m0_ref = (next(it) for _ in range(3))
    o_ref = next(it)
    if emit_state:
        co_ref, no_ref, mo_ref = (next(it) for _ in range(3))
    r_s, rt_s, bt_s, hf_s, hb_s, ct_s, n_s, m_s = (next(it) for _ in range(8))

    L = ML_CHUNK
    nc = seq // L
    kscale = DH ** -0.5

    ri = lax.broadcasted_iota(jnp.int32, (L, L), 0)
    ci = lax.broadcasted_iota(jnp.int32, (L, L), 1)
    tri_pre = (ci <= ri).astype(BF16)
    tri_suf = (ci >= ri).astype(BF16)
    mask_t = (ri <= ci, ri >= ci)
    fwd_lane = lax.broadcasted_iota(jnp.int32, (L, LANES), 1) < 2 * H_B

    for c in range(nc):
        rows = slice(c * L, (c + 1) * L)
        g = gt_ref[rows, :] + mlb_ref[...]
        gf = jnp.minimum(g, 0.0) - jnp.log1p(jnp.exp(-jnp.abs(g)))
        hi, mid, lo = _split3(gf)
        pre = _dot(tri_pre, hi) + _dot(tri_pre, mid) + _dot(tri_pre, lo)
        suf = _dot(tri_suf, hi) + _dot(tri_suf, mid) + _dot(tri_suf, lo)
        b = pltpu.roll(jnp.where(fwd_lane, pre, suf), LANES - H_B, 1)
        r = g - b
        r_s[rows, :] = r
        rt_s[c] = r.T
        bt_s[c] = b.T

    for d in range(2):
        for h in range(H_B):
            ct_s[d, h] = c0_ref[d, h].T if has_state else jnp.zeros((DH, DH), F32)
    if has_state:
        n_s[...] = n0_ref[...]
        m_s[...] = m0_ref[...]
    else:
        n_s[...] = jnp.zeros_like(n_s)
        m_s[...] = jnp.zeros_like(m_s)

    pairs = [(d, h) for d in range(2) for h in range(H_B)]

    def step(c, carry):
        rows, r_all, rt_all, bt_all = [], [], [], []
        for d in range(2):
            cc = c if d == 0 else nc - 1 - c
            rows.append(pl.ds(pl.multiple_of(cc * L, L), L))
            r_all.append(r_s[rows[d], :])
            rt_all.append(rt_s[cc])
            bt_all.append(bt_s[cc])

        qb, kb, v_t, rm, r_row, b_row, a_row, qk = {}, {}, {}, {}, {}, {}, {}, {}
        for p in pairs:
            d, h = p
            col = 2 * H_B * d + h
            sl = slice(h * DH, (h + 1) * DH)
            qb[p] = q_ref[rows[d], sl].astype(BF16)
            kb[p] = (k_ref[rows[d], sl].astype(F32) * kscale).astype(BF16)
            v_t[p] = v_ref[rows[d], sl].astype(F32).T
            r_row[p] = rt_all[d][col:col + 1, :]
            b_row[p] = bt_all[d][col:col + 1, :]
            rm[p] = jnp.where(mask_t[d], jnp.broadcast_to(r_all[d][:, col:col + 1], (L, L)), NEG_INF)
            a_row[p] = jnp.max(rm[p], axis=0, keepdims=True)
            qk[p] = _dot_nt(kb[p], qb[p])

        m_row = {(d, h): m_s[d, h:h + 1, :] for d, h in pairs}
        n_row = {(d, h): n_s[d, h:h + 1, :] for d, h in pairs}
        ct = {(d, h): ct_s[d, h] for d, h in pairs}
        big_m = {p: jnp.maximum(m_row[p], a_row[p]) for p in pairs}
        w_in = {p: jnp.exp(m_row[p] - big_m[p]) for p in pairs}
        cq = {p: _dot_nt(ct[p].astype(BF16), qb[p]) for p in pairs}
        qn = {p: _dot_nt(jnp.broadcast_to(n_row[p], (8, DH)).astype(BF16), qb[p])[0:1, :] for p in pairs}
        w_t = {p: jnp.exp(rm[p] - big_m[p]) * qk[p] for p in pairs}
        pv = {p: _dot(v_t[p].astype(BF16), w_t[p].astype(BF16)) for p in pairs}
        for p in pairs:
            d, h = p
            den = w_in[p] * qn[p] + jnp.sum(w_t[p], axis=0, keepdims=True)
            h_t = (w_in[p] * cq[p] + pv[p]) / jnp.maximum(jnp.abs(den), jnp.exp(-(b_row[p] + big_m[p])))
            (hf_s if d == 0 else hb_s)[rows[d], slice(h * DH, (h + 1) * DH)] = h_t.T

        m_last, ws = {}, {}
        for p in pairs:
            m_last[p] = jnp.maximum(m_row[p], jnp.max(r_row[p], axis=-1, keepdims=True))
            ws[p] = jnp.exp(r_row[p] - m_last[p])
        kv = {p: _dot((v_t[p] * ws[p]).astype(BF16), kb[p]) for p in pairs}
        nk = {p: _dot(jnp.broadcast_to(ws[p], (8, L)).astype(BF16), kb[p])[0:1, :] for p in pairs}
        for p in pairs:
            d, h = p
            a_prev = jnp.exp(m_row[p] - m_last[p])
            bl = b_row[p][:, L - 1:L] if d == 0 else b_row[p][:, 0:1]
            ct_s[d, h] = a_prev * ct[p] + kv[p]
            n_s[d, h:h + 1, :] = a_prev * n_row[p] + nk[p]
            m_s[d, h:h + 1, :] = bl + m_last[p]
        return carry

    lax.fori_loop(0, nc, step, 0)

    if emit_state:
        for d in range(2):
            for h in range(H_B):
                co_ref[d, h] = ct_s[d, h].T
        no_ref[...] = n_s[...]
        mo_ref[...] = m_s[...]

    rows_ep = 256
    for r in range(seq // rows_ep):
        rr = slice(r * rows_ep, (r + 1) * rows_ep)
        for h in range(H_B):
            sl = slice(h * DH, (h + 1) * DH)
            hm = hf_s[rr, sl] + hb_s[rr, sl]
            ob = ob_ref[rr, sl].astype(F32)
            o_ref[rr, sl] = (_sigmoid(ob) * _rms_lanes(hm, mlg_ref[:, sl])).astype(o_ref.dtype)


def _mlstm(proj, gates, ml_b, ml_g, batch, seq, state=None, emit_state=False):
    t = proj.shape[0]
    col0 = 3 * W_A // W_B
    big = lambda j: pl.BlockSpec((seq, W_B), lambda b: (b, col0 + j))
    in_specs = [big(0), big(1), big(2), big(3),
                pl.BlockSpec((seq, LANES), lambda b: (b, 0)),
                pl.BlockSpec((1, LANES), lambda b: (0, 0)),
                pl.BlockSpec((1, W_B), lambda b: (0, 0))]
    args = [proj, proj, proj, proj, gates, ml_b, ml_g.reshape(1, W_B)]
    c_spec = pl.BlockSpec((None, None, 2, H_B, DH, DH), lambda b: (b, 0, 0, 0, 0, 0))
    n_spec = pl.BlockSpec((None, None, 2, H_B, DH), lambda b: (b, 0, 0, 0, 0))
    if state is not None:
        in_specs += [c_spec, n_spec, n_spec]
        args += list(state)
    out_specs = [pl.BlockSpec((seq, W_B), lambda b: (b, 0))]
    out_shape = [jax.ShapeDtypeStruct((t, W_B), BF16)]
    if emit_state:
        out_specs += [c_spec, n_spec, n_spec]
        out_shape += [jax.ShapeDtypeStruct((batch, 1, 2, H_B, DH, DH), F32),
                      jax.ShapeDtypeStruct((batch, 1, 2, H_B, DH), F32),
                      jax.ShapeDtypeStruct((batch, 1, 2, H_B, LANES), F32)]
    nc = seq // ML_CHUNK
    return pl.pallas_call(
        functools.partial(_mlstm_kernel, has_state=state is not None, emit_state=emit_state, seq=seq),
        grid=(batch,),
        in_specs=in_specs,
        out_specs=out_specs,
        out_shape=out_shape,
        scratch_shapes=[
            pltpu.VMEM((seq, LANES), F32),
            pltpu.VMEM((nc, ML_CHUNK, LANES), F32),
            pltpu.VMEM((nc, ML_CHUNK, LANES), F32),
            pltpu.VMEM((seq, W_B), F32),
            pltpu.VMEM((seq, W_B), F32),
            pltpu.VMEM((2, H_B, DH, DH), F32),
            pltpu.VMEM((2, H_B, DH), F32),
            pltpu.VMEM((2, H_B, LANES), F32),
        ],
        compiler_params=_cparams("parallel"),
        name="mlstm",
    )(*args)


DIFF_QB = 256


def _diff_kernel(*refs, heads, seq, past, rope, emit_ctx, lam_init):
    it = iter(refs)
    q_ref, k_ref, v_ref, gq_ref, gk_ref, go_ref, lam_ref = (next(it) for _ in range(7))
    if past:
        kc_ref, vc_ref = next(it), next(it)
    if rope:
        cos_ref, s1_ref, s2_ref = next(it), next(it), next(it)
    o_ref = next(it)
    if emit_ctx:
        ko_ref, vo_ref = next(it), next(it)

    scale = DQK ** -0.5
    lv = lam_ref[...]
    lam = (jnp.exp(jnp.sum(lv[0:1, :] * lv[1:2, :], axis=-1, keepdims=True))
           - jnp.exp(jnp.sum(lv[2:3, :] * lv[3:4, :], axis=-1, keepdims=True)) + lam_init)
    low = lax.broadcasted_iota(jnp.int32, (seq, LANES), 1) < DQK
    hr = lax.broadcasted_iota(jnp.int32, (LANES, LANES), 0) < DQK
    hc = lax.broadcasted_iota(jnp.int32, (LANES, LANES), 1) < DQK
    halves = (hr == hc).astype(BF16)

    def apply_rope(x):
        return (x * cos_ref[...] + pltpu.roll(x, LANES - DQK // 4, 1) * s1_ref[...]
                + pltpu.roll(x, DQK // 4, 1) * s2_ref[...])

    qb_rows = min(DIFF_QB, seq)
    blocks = [slice(i * qb_rows, (i + 1) * qb_rows) for i in range(seq // qb_rows)]
    head_sl = [slice(hh * DH, (hh + 1) * DH) for hh in range(heads)]
    qn = [_rms_lane_groups(q_ref[:, sl].astype(F32), gq_ref[...], halves, DQK) for sl in head_sl]
    kn = [_rms_lane_groups(k_ref[:, sl].astype(F32), gk_ref[...], halves, DQK) for sl in head_sl]
    v = [v_ref[:, sl].astype(F32) for sl in head_sl]
    if emit_ctx:
        for hh in range(heads):
            ko_ref[hh] = kn[hh]
            vo_ref[hh] = v[hh]
    if rope:
        qn = [apply_rope(x) for x in qn]
        kn = [apply_rope(x) for x in kn]
    assert math.frexp(scale)[0] == 0.5
    q0 = [(jnp.where(low, x, 0.0) * scale).astype(BF16) for x in qn]
    q1 = [(jnp.where(low, 0.0, x) * scale).astype(BF16) for x in qn]
    ka, vt = [], []
    for hh in range(heads):
        k_parts = [kn[hh].astype(BF16)]
        vt_parts = [v[hh][c * LANES:(c + 1) * LANES, :].T.astype(BF16) for c in range(seq // LANES)]
        if past:
            k_parts.append(kc_ref[hh].astype(BF16))
            vt_parts += [vc_ref[hh, c * LANES:(c + 1) * LANES, :].T.astype(BF16) for c in range(past // LANES)]
        ka.append(jnp.concatenate(k_parts, axis=0))
        vt.append(jnp.concatenate(vt_parts, axis=1))

    work = [(hh, rows) for hh in range(heads) for rows in blocks]
    s0 = [_dot_nt(ka[hh], q0[hh][rows, :]) for hh, rows in work]
    s1 = [_dot_nt(ka[hh], q1[hh][rows, :]) for hh, rows in work]

    def softmax_t(s, weight):
        p = jnp.exp(s - jnp.max(s, axis=0, keepdims=True))
        return p * (weight / jnp.sum(p, axis=0, keepdims=True))

    a_t = [(softmax_t(a, 1.0) - softmax_t(b, lam)).astype(BF16) for a, b in zip(s0, s1)]
    out_t = [_dot(vt[hh], a) for (hh, _), a in zip(work, a_t)]
    for (hh, rows), o in zip(work, out_t):
        o = o * lax.rsqrt(jnp.mean(o * o, axis=0, keepdims=True) + EPS)
        g_row = go_ref[:, head_sl[hh]] * (1.0 - lam_init)
        for c in range(qb_rows // LANES):
            r0 = rows.start + c * LANES
            o_ref[r0:r0 + LANES, head_sl[hh]] = (o[:, c * LANES:(c + 1) * LANES].T * g_row).astype(o_ref.dtype)


def _diff_attn(proj, gq, gk, g_out, lam_vec, lam_init, batch, seq, heads, cache=None, rope=None,
               emit_ctx=False):
    t = proj.shape[0]
    nh = H_C // heads
    w = heads * DH
    kblk = W_CQK // w
    past = 0 if cache is None else cache[0].shape[3]
    in_specs = [
        pl.BlockSpec((seq, w), lambda b, j: (b, j)),
        pl.BlockSpec((seq, w), lambda b, j: (b, kblk + j)),
        pl.BlockSpec((seq, w), lambda b, j: (b, 2 * kblk + j)),
        pl.BlockSpec((1, DH), lambda b, j: (0, 0)),
        pl.BlockSpec((1, DH), lambda b, j: (0, 0)),
        pl.BlockSpec((1, w), lambda b, j: (0, j)),
        pl.BlockSpec((4, DQK), lambda b, j: (0, 0)),
    ]
    args = [proj, proj, proj, jnp.tile(gq, 2).reshape(1, DH), jnp.tile(gk, 2).reshape(1, DH),
            g_out.reshape(1, W_CV), lam_vec]
    if cache is not None:
        ctx_spec = pl.BlockSpec((None, None, heads, past, DH), lambda b, j: (b, 0, j, 0, 0))
        in_specs += [ctx_spec, ctx_spec]
        args += [cache[0].reshape(cache[0].shape[:4] + (DH,)), cache[1]]
    if rope is not None:
        tab = pl.BlockSpec((seq, DH), lambda b, j: (0, 0))
        in_specs += [tab, tab, tab]
        args += list(rope)
    out_specs = [pl.BlockSpec((seq, w), lambda b, j: (b, j))]
    out_shape = [jax.ShapeDtypeStruct((t, W_CV), BF16)]
    if emit_ctx:
        kv_spec = pl.BlockSpec((None, None, heads, seq, DH), lambda b, j: (b, 0, j, 0, 0))
        kv_shape = jax.ShapeDtypeStruct((batch, 1, H_C, seq, DH), F32)
        out_specs += [kv_spec, kv_spec]
        out_shape += [kv_shape, kv_shape]
    return pl.pallas_call(
        functools.partial(_diff_kernel, heads=heads, seq=seq, past=past, rope=rope is not None,
                          emit_ctx=emit_ctx, lam_init=lam_init),
        grid=(batch, nh),
        in_specs=in_specs,
        out_specs=out_specs,
        out_shape=out_shape,
        compiler_params=_cparams("parallel", "arbitrary"),
        name="diff_attn",
    )(*args)


def _rope_tables(seq):
    tpos = jnp.arange(seq)
    pos = jnp.stack([tpos // GRID_W, tpos % GRID_W], axis=-1).astype(F32)
    half = DQK // 2
    inv = ROPE_THETA ** (-jnp.arange(0, half, 2, dtype=F32) / half)
    ang = pos[:, :, None] * inv
    cos = jnp.cos(ang)
    sin = jnp.sin(ang)
    zero = jnp.zeros_like(sin)
    lay = lambda first, second: jnp.tile(jnp.concatenate([first, second], axis=-1).reshape(seq, DQK), (1, 2))
    return lay(cos, cos), lay(-sin, zero), lay(zero, sin)


SGU_TM = 512


def _sgu_kernel(u_ref, vd_ref, sgg_ref, sgw_ref, sgb_ref, o_ref):
    tm = u_ref.shape[0]
    for c in range(tm // SG_CHUNK):
        rows = slice(c * SG_CHUNK, (c + 1) * SG_CHUNK)
        vn = _rms_lanes(_gelu_tanh(vd_ref[rows, :].astype(F32)), sgg_ref[...]).astype(BF16)
        for g in range(G_D):
            sl = slice(g * DH, (g + 1) * DH)
            gate = _dot(sgw_ref[g].astype(BF16), vn[:, sl]) + sgb_ref[:, g:g + 1]
            o_ref[rows, sl] = (_gelu_tanh(u_ref[rows, sl].astype(F32)) * gate).astype(o_ref.dtype)


def _sgu(proj, sg_g, sg_w, sg_b):
    t = proj.shape[0]
    tm = SGU_TM
    ublk = (2 * W_CQK + W_CV) // W_D
    return pl.pallas_call(
        _sgu_kernel,
        grid=(t // tm,),
        in_specs=[
            pl.BlockSpec((tm, W_D), lambda i: (i, ublk)),
            pl.BlockSpec((tm, W_D), lambda i: (i, ublk + 1)),
            pl.BlockSpec((1, W_D), lambda i: (0, 0)),
            pl.BlockSpec((G_D, SG_CHUNK, SG_CHUNK), lambda i: (0, 0, 0)),
            pl.BlockSpec((SG_CHUNK, G_D), lambda i: (0, 0)),
        ],
        out_specs=pl.BlockSpec((tm, W_D), lambda i: (i, 0)),
        out_shape=jax.ShapeDtypeStruct((t, W_D), BF16),
        compiler_params=_cparams("parallel"),
        name="sgu",
    )(proj, proj, sg_g.reshape(1, W_D), sg_w, sg_b.T)


def _lambda_init(layer):
    return 0.8 - 0.6 * math.exp(-0.3 * layer)


def _trunk(x, mods, p, batch, seq, caches):
    ctx_out = {}
    for l in range(DEPTH):
        mod = mods[l]
        if l % 2 == 0:
            proj, gates = _inproj(x, p["g_mix"][l], mod, 1, 0, p["w_even"], N_EVEN_MAIN, N_EVEN_MAIN // LANES)
            if caches is None:
                mix_a, ctx_out["na_k"], ctx_out["na_v"] = _attn_ctx(proj, p["na_gq"], p["na_gk"], batch, seq)
                mix_b, c1, n1, m1 = _mlstm(proj, gates, p["ml_b"], p["ml_g"], batch, seq, emit_state=True)
                ctx_out.update(mlstm_C=c1, mlstm_n=n1, mlstm_m=m1[..., 0])
            else:
                mix_a = _natten(proj, caches["na_k"], caches["na_v"], p["na_gq"], p["na_gk"], p["na_rpb"], batch, seq)
                (mix_b,) = _mlstm(proj, gates, p["ml_b"], p["ml_g"], batch, seq,
                                  state=(caches["C"], caches["n"], caches["m"]))
        else:
            proj = _inproj(x, p["g_mix"][l], mod, 1, 0, p["w_odd"], p["w_odd"].shape[1])
            lam_init = _lambda_init(l)
            if caches is None:
                mix_a, dk, dv = _diff_attn(proj, p["diff_gq"], p["diff_gk"], p["diff_g_out"], p["diff_lam"],
                                           lam_init, batch, seq, heads=H_C, emit_ctx=True)
                ctx_out["diff_k"] = dk.reshape(batch, 1, H_C, seq, 2, DQK)
                ctx_out["diff_v"] = dv
            else:
                (mix_a,) = _diff_attn(proj, p["diff_gq"], p["diff_gk"], p["diff_g_out"], p["diff_lam"],
                                      lam_init, batch, seq, heads=1,
                                      cache=(caches["diff_k"], caches["diff_v"]), rope=_rope_tables(seq))
            mix_b = _sgu(proj, p["sg_g"], p["sg_w"], p["sg_b"])
        x = _outproj(mix_a, mix_b, p["w_out"], l, x, mod, 2)
        x = _ffn(x, p["g_ffn"][l], mod, p["w_up"], p["conv_w"][l], p["conv_b"][l], p["w_down"], l, seq)
    return x, ctx_out


def kernel(x_prompt, x_sample, cache_na_k, cache_na_v, state_mlstm_C, state_mlstm_n, state_mlstm_m,
           cache_diff_k, cache_diff_v, c, c_ctx, w_mod, b_mod, g_mix, g_ffn, w_out, w_in_even,
           na_gq, na_gk, na_rpb, ml_b_gates, ml_g_out, w_in_odd, diff_gq, diff_gk, diff_lam,
           diff_g_out, sg_g_v, sg_w, sg_b, ffn_w_up, ffn_conv_w, ffn_conv_b, ffn_w_down):
    batch, seq, _ = x_prompt.shape
    dbatch, dseq, _ = x_sample.shape

    cond8 = jnp.zeros((8, D_MODEL), F32).at[0].set(c_ctx).at[1:1 + dbatch].set(c)
    mod = _modulation(cond8, w_mod, b_mod)
    mods_ctx = [mod[l, 0:1].reshape(1, 6, 1, D_MODEL) for l in range(DEPTH)]
    mods_lat = [mod[l, 1:1 + dbatch].reshape(dbatch, 6, 1, D_MODEL) for l in range(DEPTH)]

    p = dict(
        g_mix=g_mix, g_ffn=g_ffn,
        w_even=jnp.pad(w_in_even[0], ((0, 0), (0, LANES - N_GATES))).astype(BF16),
        w_odd=w_in_odd[0].astype(BF16),
        w_out=w_out.astype(BF16), w_up=ffn_w_up.astype(BF16), w_down=ffn_w_down.astype(BF16),
        conv_w=ffn_conv_w, conv_b=ffn_conv_b,
        na_gq=na_gq[0], na_gk=na_gk[0], na_rpb=na_rpb[0],
        ml_b=jnp.pad(ml_b_gates[0], (0, LANES - N_GATES)).reshape(1, LANES), ml_g=ml_g_out[0],
        diff_gq=diff_gq[0], diff_gk=diff_gk[0], diff_lam=diff_lam[0], diff_g_out=diff_g_out[0],
        sg_g=sg_g_v[0], sg_w=sg_w[0], sg_b=sg_b[0],
    )

    y_prompt, ctx = _trunk(x_prompt.reshape(batch * seq, D_MODEL), mods_ctx, p, batch, seq, None)
    caches = dict(
        na_k=cache_na_k, na_v=cache_na_v, C=state_mlstm_C, n=state_mlstm_n,
        m=jnp.broadcast_to(state_mlstm_m[..., None], state_mlstm_m.shape + (LANES,)),
        diff_k=cache_diff_k, diff_v=cache_diff_v,
    )
    y_sample, _ = _trunk(x_sample.reshape(dbatch * dseq, D_MODEL), mods_lat, p, dbatch, dseq, caches)
    return (y_prompt.reshape(batch, seq, D_MODEL), y_sample.reshape(dbatch, dseq, D_MODEL),
            ctx["na_k"], ctx["na_v"], ctx["mlstm_C"], ctx["mlstm_n"], ctx["mlstm_m"],
            ctx["diff_k"], ctx["diff_v"])
```

```python
import functools
import math

import numpy as np
import jax
import jax.numpy as jnp
from jax import lax
from jax.experimental import pallas as pl
from jax.experimental.pallas import tpu as pltpu

F32 = jnp.float32
BF16 = jnp.bfloat16

D_MODEL = 2048
DEPTH = 2
GRID_W = 64
DH = 128
H_A = 8
W_A = H_A * DH
WIN_H = 8
WIN_W = 16
H_B = 8
W_B = H_B * DH
ML_CHUNK = 128
H_C = 8
DQK = 64
W_CQK = H_C * 2 * DQK
W_CV = H_C * DH
ROPE_THETA = 10000.0
G_D = 8
SG_CHUNK = 128
W_D = 1024
D_FF = 5632
Q_BLOCK = 128
EPS = 1e-6
N_GATES = 4 * H_B
N_EVEN_MAIN = 3 * W_A + 4 * W_B

LANES = 128
V7X_VMEM_BYTES = 64 * 1024 * 1024
VMEM_LIMIT = V7X_VMEM_BYTES - 8 * 1024 * 1024

NEG_INF = float("-inf")


def _cparams(*sem):
    return pltpu.CompilerParams(dimension_semantics=sem, vmem_limit_bytes=VMEM_LIMIT)


def _dot(a, b):
    return jnp.dot(a, b, preferred_element_type=F32)


def _dot_nt(a, b):
    return lax.dot_general(a, b, (((1,), (1,)), ((), ())), preferred_element_type=F32)


def _rms_lanes(x, g):
    ms = jnp.mean(x * x, axis=-1, keepdims=True)
    return x * lax.rsqrt(ms + EPS) * g


def _rms_lane_groups(x, g, group_ones, width):
    y = x * x
    hi = y.astype(BF16)
    mid = (y - hi.astype(F32)).astype(BF16)
    ss = _dot(hi, group_ones) + _dot(mid, group_ones)
    return x * lax.rsqrt(ss * (1.0 / width) + EPS) * g


def _norm_mod(x, g, scale, shift):
    return _rms_lanes(x, g) * (1.0 + scale) + shift


def _sigmoid(x):
    return 1.0 / (1.0 + jnp.exp(-x))


def _gelu_tanh(x):
    c = math.sqrt(2.0 / math.pi)
    return 0.5 * x * (1.0 + jnp.tanh(c * (x + 0.044715 * (x * x * x))))


CAST_BLK = 512


def _cast_job(w, layer, axis, step_of, n_steps):
    nblk = w.shape[axis] // CAST_BLK
    assert w.shape[axis] % CAST_BLK == 0 and n_steps >= nblk
    blk_of = lambda *g: jnp.minimum(step_of(*g), nblk - 1)
    if axis == 2:
        in_spec = pl.BlockSpec((None, w.shape[1], CAST_BLK), lambda *g: (layer, 0, blk_of(*g)))
        out_spec = pl.BlockSpec((w.shape[1], CAST_BLK), lambda *g: (0, blk_of(*g)))
    else:
        in_spec = pl.BlockSpec((None, CAST_BLK, w.shape[2]), lambda *g: (layer, blk_of(*g), 0))
        out_spec = pl.BlockSpec((CAST_BLK, w.shape[2]), lambda *g: (blk_of(*g), 0))
    return in_spec, out_spec, jax.ShapeDtypeStruct(w.shape[1:], BF16)


MOD_TN = 1024


def _mod_kernel(c_ref, w_ref, b_ref, o_ref):
    c = c_ref[...]
    s = c * _sigmoid(c)
    o_ref[...] = _dot(s.astype(BF16), w_ref[...].astype(BF16)) + b_ref[...]


def _modulation(cond8, w_mod, b_mod):
    n = 6 * D_MODEL
    return pl.pallas_call(
        _mod_kernel,
        grid=(DEPTH, n // MOD_TN),
        in_specs=[
            pl.BlockSpec((8, D_MODEL), lambda l, j: (0, 0)),
            pl.BlockSpec((None, D_MODEL, MOD_TN), lambda l, j: (l, 0, j)),
            pl.BlockSpec((None, 1, MOD_TN), lambda l, j: (l, 0, j)),
        ],
        out_specs=pl.BlockSpec((None, 8, MOD_TN), lambda l, j: (l, 0, j)),
        out_shape=jax.ShapeDtypeStruct((DEPTH, 8, n), F32),
        compiler_params=_cparams("parallel", "arbitrary"),
        name="modulation",
    )(cond8, w_mod, b_mod.reshape(DEPTH, 1, n))


DENSE_TM = 1024
NORM_ROWS = 256
PROJ_DTYPE = BF16


def _inproj_body(x_ref, g_ref, sc_ref, sh_ref, w_ref, o_ref, hb_ref, wg_ref=None, og_ref=None):
    j = pl.program_id(1)
    tm = x_ref.shape[0]

    @pl.when(j == 0)
    def _():
        for c in range(tm // NORM_ROWS):
            r = slice(c * NORM_ROWS, (c + 1) * NORM_ROWS)
            h = _norm_mod(x_ref[r, :], g_ref[...], sc_ref[...], sh_ref[...]).astype(BF16)
            hb_ref[r, :] = h
            o_ref[r, :] = _dot(h, w_ref[...]).astype(o_ref.dtype)
            if og_ref is not None:
                og_ref[r, :] = _dot(h, wg_ref[...])

    @pl.when(j > 0)
    def _():
        o_ref[...] = _dot(hb_ref[...], w_ref[...]).astype(o_ref.dtype)


def _inproj_kernel(x_ref, g_ref, sc_ref, sh_ref, w_ref, o_ref, hb_ref):
    _inproj_body(x_ref, g_ref, sc_ref, sh_ref, w_ref, o_ref, hb_ref)


def _inproj_gates_kernel(x_ref, g_ref, sc_ref, sh_ref, w_ref, wg_ref, o_ref, og_ref, hb_ref):
    _inproj_body(x_ref, g_ref, sc_ref, sh_ref, w_ref, o_ref, hb_ref, wg_ref, og_ref)


def _mod_spec(idx, tm, rows_per_mod):
    return pl.BlockSpec((None, None, 1, D_MODEL), lambda i, n: ((i * tm) // rows_per_mod, idx, 0, 0))


def _inproj(x, g, mod, scale_idx, shift_idx, w, n, gates_block=None, tn=1024):
    t = x.shape[0]
    tm = DENSE_TM
    rows_per_mod = t // mod.shape[0]
    in_specs = [
        pl.BlockSpec((tm, D_MODEL), lambda i, j: (i, 0)),
        pl.BlockSpec((1, D_MODEL), lambda i, j: (0, 0)),
        _mod_spec(scale_idx, tm, rows_per_mod),
        _mod_spec(shift_idx, tm, rows_per_mod),
        pl.BlockSpec((D_MODEL, tn), lambda i, j: (0, j)),
    ]
    out_specs = pl.BlockSpec((tm, tn), lambda i, j: (i, j))
    out_shape = jax.ShapeDtypeStruct((t, n), PROJ_DTYPE)
    args = [x, g.reshape(1, D_MODEL), mod, mod, w]
    kern = _inproj_kernel
    if gates_block is not None:
        in_specs.append(pl.BlockSpec((D_MODEL, LANES), lambda i, j: (0, gates_block)))
        out_specs = [out_specs, pl.BlockSpec((tm, LANES), lambda i, j: (i, 0))]
        out_shape = [out_shape, jax.ShapeDtypeStruct((t, LANES), F32)]
        args.append(w)
        kern = _inproj_gates_kernel
    return pl.pallas_call(
        kern,
        grid=(t // tm, n // tn),
        in_specs=in_specs,
        out_specs=out_specs,
        out_shape=out_shape,
        scratch_shapes=[pltpu.VMEM((tm, D_MODEL), BF16)],
        compiler_params=_cparams("parallel", "arbitrary"),
        name="inproj",
    )(*args)


OUT_TM = 512


def _outproj_kernel(a_ref, b_ref, wa_ref, wb_ref, x_ref, gate_ref, o_ref):
    acc = _dot(a_ref[...], wa_ref[...]) + _dot(b_ref[...], wb_ref[...])
    o_ref[...] = x_ref[...] + gate_ref[...] * acc


def _outproj(mix_a, mix_b, w, layer, x, mod, gate_idx):
    t = x.shape[0]
    tm = OUT_TM
    half = mix_a.shape[1]
    rows_per_mod = t // mod.shape[0]
    return pl.pallas_call(
        _outproj_kernel,
        grid=(t // tm,),
        in_specs=[
            pl.BlockSpec((tm, half), lambda i: (i, 0)),
            pl.BlockSpec((tm, half), lambda i: (i, 0)),
            pl.BlockSpec((None, half, D_MODEL), lambda i: (layer, 0, 0)),
            pl.BlockSpec((None, half, D_MODEL), lambda i: (layer, 1, 0)),
            pl.BlockSpec((tm, D_MODEL), lambda i: (i, 0)),
            pl.BlockSpec((None, None, 1, D_MODEL), lambda i: ((i * tm) // rows_per_mod, gate_idx, 0, 0)),
        ],
        out_specs=pl.BlockSpec((tm, D_MODEL), lambda i: (i, 0)),
        out_shape=jax.ShapeDtypeStruct((t, D_MODEL), F32),
        compiler_params=_cparams("parallel"),
        name="outproj",
    )(mix_a, mix_b, w, w, x, mod)


FFN_TF = 512
FFN_ROWS = 512


def _ffn_kernel(x_ref, g_ref, sc_ref, sh_ref, gate_ref, wg_ref, wv_ref, cwg_ref, cwv_ref, cbg_ref, cbv_ref,
                wd_ref, o_ref, hb_ref, *, seq_len):
    f = pl.program_id(1)
    tm = x_ref.shape[0]

    rp = max(seq_len, FFN_ROWS)
    pos = lax.broadcasted_iota(jnp.int32, (rp, 1), 0) & (seq_len - 1)
    first = pos == 0
    last = pos == seq_len - 1

    def conv(a, cw_ref, cb_ref):
        prev = jnp.where(first, 0.0, pltpu.roll(a, 1, 0))
        nxt = jnp.where(last, 0.0, pltpu.roll(a, rp - 1, 0))
        return prev * cw_ref[0:1, :] + a * cw_ref[1:2, :] + nxt * cw_ref[2:3, :] + cb_ref[...]

    parts = [slice(r, r + rp) for r in range(0, tm, rp)]

    def hidden_block(first):
        acts = []
        for rows in parts:
            if first:
                for c in range(rows.start, rows.stop, NORM_ROWS):
                    r = slice(c, c + NORM_ROWS)
                    h = _norm_mod(x_ref[r, :], g_ref[...], sc_ref[...], sh_ref[...])
                    hb_ref[r, :] = h.astype(BF16)
            hb = hb_ref[rows, :]
            cg = conv(_dot(hb, wg_ref[...]), cwg_ref, cbg_ref)
            cv = conv(_dot(hb, wv_ref[...]), cwv_ref, cbv_ref)
            acts.append((cg * _sigmoid(cg) * cv).astype(BF16))
        for rows, act in zip(parts, acts):
            if first:
                o_ref[rows, :] = _dot(act, wd_ref[...])
            else:
                o_ref[rows, :] += _dot(act, wd_ref[...])

    @pl.when(f == 0)
    def _():
        hidden_block(True)

    @pl.when(f > 0)
    def _():
        hidden_block(False)

    @pl.when(f == pl.num_programs(1) - 1)
    def _():
        o_ref[...] = x_ref[...] + gate_ref[...] * o_ref[...]


def _ffn(x, g, mod, w_up, conv_w, conv_b, w_down, seq_len):
    t = x.shape[0]
    tm, tf = DENSE_TM, FFN_TF
    nf = D_FF // tf
    rows_per_mod = t // mod.shape[0]
    conv_b = conv_b.reshape(1, 2 * D_FF)
    assert seq_len & (seq_len - 1) == 0 and tm % seq_len == 0
    return pl.pallas_call(
        functools.partial(_ffn_kernel, seq_len=seq_len),
        grid=(t // tm, nf),
        in_specs=[
            pl.BlockSpec((tm, D_MODEL), lambda i, f: (i, 0), pipeline_mode=pl.Buffered(1)),
            pl.BlockSpec((1, D_MODEL), lambda i, f: (0, 0)),
            _mod_spec(4, tm, rows_per_mod),
            _mod_spec(3, tm, rows_per_mod),
            _mod_spec(5, tm, rows_per_mod),
            pl.BlockSpec((D_MODEL, tf), lambda i, f: (0, f)),
            pl.BlockSpec((D_MODEL, tf), lambda i, f: (0, nf + f)),
            pl.BlockSpec((3, tf), lambda i, f: (0, f)),
            pl.BlockSpec((3, tf), lambda i, f: (0, nf + f)),
            pl.BlockSpec((1, tf), lambda i, f: (0, f)),
            pl.BlockSpec((1, tf), lambda i, f: (0, nf + f)),
            pl.BlockSpec((tf, D_MODEL), lambda i, f: (f, 0)),
        ],
        out_specs=pl.BlockSpec((tm, D_MODEL), lambda i, f: (i, 0)),
        out_shape=jax.ShapeDtypeStruct((t, D_MODEL), F32),
        scratch_shapes=[pltpu.VMEM((tm, D_MODEL), BF16)],
        compiler_params=_cparams("parallel", "arbitrary"),
        name="convffn",
    )(x, g.reshape(1, D_MODEL), mod, mod, mod, w_up, w_up, conv_w, conv_w, conv_b, conv_b, w_down)


def _attn_ctx_kernel(q_ref, k_ref, v_ref, gq_ref, gk_ref, wc_ref, o_ref, ko_ref, vo_ref, wo_ref):
    wo_ref[...] = wc_ref[...].astype(BF16)
    scale = DH ** -0.5
    for h in range(H_A):
        sl = slice(h * DH, (h + 1) * DH)
        q = _rms_lanes(q_ref[:, sl].astype(F32), gq_ref[...])
        k = _rms_lanes(k_ref[:, sl].astype(F32), gk_ref[...])
        v = v_ref[:, sl].astype(F32)
        ko_ref[h] = k
        vo_ref[h] = v
        s = _dot_nt(q.astype(BF16), k.astype(BF16)) * scale
        p = jnp.exp(s - jnp.max(s, axis=-1, keepdims=True))
        p = p / jnp.sum(p, axis=-1, keepdims=True)
        o_ref[:, sl] = _dot(p.astype(BF16), v.astype(BF16)).astype(o_ref.dtype)


def _attn_ctx(proj, gq, gk, batch, seq, cast):
    t = proj.shape[0]
    c_in, c_out, c_shape = _cast_job(*cast, lambda b: b, batch)
    kv_shape = jax.ShapeDtypeStruct((batch, 1, H_A, seq, DH), F32)
    kv_spec = pl.BlockSpec((None, None, H_A, seq, DH), lambda b: (b, 0, 0, 0, 0))
    return pl.pallas_call(
        _attn_ctx_kernel,
        grid=(batch,),
        in_specs=[
            pl.BlockSpec((seq, W_A), lambda b: (b, 0)),
            pl.BlockSpec((seq, W_A), lambda b: (b, 1)),
            pl.BlockSpec((seq, W_A), lambda b: (b, 2)),
            pl.BlockSpec((1, DH), lambda b: (0, 0)),
            pl.BlockSpec((1, DH), lambda b: (0, 0)),
            c_in,
        ],
        out_specs=[pl.BlockSpec((seq, W_A), lambda b: (b, 0)), kv_spec, kv_spec, c_out],
        out_shape=[jax.ShapeDtypeStruct((t, W_A), BF16), kv_shape, kv_shape, c_shape],
        compiler_params=_cparams("arbitrary"),
        name="attn_ctx",
    )(proj, proj, proj, gq.reshape(1, DH), gk.reshape(1, DH), cast[0])


NA_ROWS = 16
NA_PAIRS = 2 * WIN_H - 2
RPB_H = 2 * WIN_H - 1
RPB_W = 2 * WIN_W - 1


def _natten_kernel(rpb_ref, q_ref, k_ref, v_ref, kc_ref, vc_ref, gq_ref, gk_ref, o_ref,
                   qn_s, kn_s, v_s, kc_s, vc_s, bias_s):
    h = pl.program_id(1)
    scale = DH ** -0.5
    qn_s[...] = _rms_lanes(q_ref[...].astype(F32), gq_ref[...]).astype(BF16)
    kn_s[...] = _rms_lanes(k_ref[...].astype(F32), gk_ref[...]).astype(BF16)
    v_s[...] = v_ref[...].astype(BF16)
    kc_s[...] = kc_ref[...].astype(BF16)
    vc_s[...] = vc_ref[...].astype(BF16)

    lane = lax.broadcasted_iota(jnp.int32, (GRID_W, LANES), 1)
    qc = lax.broadcasted_iota(jnp.int32, (GRID_W, LANES), 0)
    kcol = lane % GRID_W
    upper = lane >= GRID_W
    dcol = kcol - qc + (WIN_W - 1)
    col0 = jnp.clip(qc - WIN_W // 2, 0, GRID_W - WIN_W)
    col_ok = (kcol >= col0) & (kcol < col0 + WIN_W)
    base = h * (RPB_H * RPB_W)

    def build(p, carry):
        acc = jnp.zeros((GRID_W, LANES), F32)
        for d in range(RPB_W):
            lo = rpb_ref[base + p * RPB_W + d]
            hi = rpb_ref[base + (p + 1) * RPB_W + d]
            acc = jnp.where(dcol == d, jnp.where(upper, hi, lo), acc)
        bias_s[p] = jnp.where(col_ok, acc, NEG_INF)
        return carry

    lax.fori_loop(0, NA_PAIRS, build, 0)

    kh = WIN_H
    n_loc = kh * GRID_W
    row0 = [min(max(r - kh // 2, 0), NA_ROWS - kh) for r in range(NA_ROWS)]
    q = [qn_s[r * GRID_W:(r + 1) * GRID_W, :] for r in range(NA_ROWS)]
    kc = kc_s[...]
    s_loc, s_ctx = [], []
    for r in range(NA_ROWS):
        dr0 = row0[r] - r + WIN_H - 1
        k_loc = kn_s[row0[r] * GRID_W:row0[r] * GRID_W + n_loc, :]
        bias = jnp.concatenate([bias_s[dr0 + 2 * i] for i in range(kh // 2)], axis=-1)
        s_loc.append(_dot_nt(q[r], k_loc) * scale + bias)
        s_ctx.append(_dot_nt(q[r], kc) * scale)
    p_loc, p_ctx = [], []
    for r in range(NA_ROWS):
        m = jnp.maximum(jnp.max(s_loc[r], axis=-1, keepdims=True), jnp.max(s_ctx[r], axis=-1, keepdims=True))
        e_loc = jnp.exp(s_loc[r] - m)
        e_ctx = jnp.exp(s_ctx[r] - m)
        l = jnp.sum(e_loc, axis=-1, keepdims=True) + jnp.sum(e_ctx, axis=-1, keepdims=True)
        p_loc.append((e_loc / l).astype(BF16))
        p_ctx.append((e_ctx / l).astype(BF16))
    vc = vc_s[...]
    for r in range(NA_ROWS):
        v_loc = v_s[row0[r] * GRID_W:row0[r] * GRID_W + n_loc, :]
        out = _dot(p_loc[r], v_loc) + _dot(p_ctx[r], vc)
        o_ref[r * GRID_W:(r + 1) * GRID_W, :] = out.astype(o_ref.dtype)


def _natten(proj, cache_k, cache_v, gq, gk, rpb, batch, seq):
    t = proj.shape[0]
    past = cache_k.shape[3]
    ctx_spec = pl.BlockSpec((None, None, None, past, DH), lambda b, h: (b, 0, h, 0, 0))
    return pl.pallas_call(
        _natten_kernel,
        grid=(batch, H_A),
        in_specs=[
            pl.BlockSpec(memory_space=pltpu.SMEM),
            pl.BlockSpec((seq, DH), lambda b, h: (b, h)),
            pl.BlockSpec((seq, DH), lambda b, h: (b, H_A + h)),
            pl.BlockSpec((seq, DH), lambda b, h: (b, 2 * H_A + h)),
            ctx_spec,
            ctx_spec,
            pl.BlockSpec((1, DH), lambda b, h: (0, 0)),
            pl.BlockSpec((1, DH), lambda b, h: (0, 0)),
        ],
        out_specs=pl.BlockSpec((seq, DH), lambda b, h: (b, h)),
        out_shape=jax.ShapeDtypeStruct((t, W_A), BF16),
        scratch_shapes=[
            pltpu.VMEM((seq, DH), BF16),
            pltpu.VMEM((seq, DH), BF16),
            pltpu.VMEM((seq, DH), BF16),
            pltpu.VMEM((past, DH), BF16),
            pltpu.VMEM((past, DH), BF16),
            pltpu.VMEM((NA_PAIRS, GRID_W, LANES), F32),
        ],
        compiler_params=_cparams("parallel", "arbitrary"),
        name="natten",
    )(rpb.reshape(-1), proj, proj, proj, cache_k, cache_v, gq.reshape(1, DH), gk.reshape(1, DH))


def _split3(x):
    hi = x.astype(BF16)
    r1 = x - hi.astype(F32)
    mid = r1.astype(BF16)
    lo = (r1 - mid.astype(F32)).astype(BF16)
    return hi, mid, lo


def _mlstm_kernel(*refs, has_state, emit_state, has_cast, seq):
    it = iter(refs)
    q_ref, k_ref, v_ref, ob_ref, gt_ref, mlb_ref, mlg_ref = (next(it) for _ in range(7))
    if has_state:
        c0_ref, n0_ref, m0_ref = (next(it) for _ in range(3))
    if has_cast:
        wc_ref = next(it)
    o_ref = next(it)
    if emit_state:
        co_ref, no_ref, mo_ref = (next(it) for _ in range(3))
    if has_cast:
        wo_ref = next(it)
        wo_ref[...] = wc_ref[...].astype(BF16)
    r_s, rt_s, bt_s, hf_s, hb_s, ct_s, n_s, m_s = (next(it) for _ in range(8))

    L = ML_CHUNK
    nc = seq // L
    kscale = DH ** -0.5

    ri = lax.broadcasted_iota(jnp.int32, (L, L), 0)
    ci = lax.broadcasted_iota(jnp.int32, (L, L), 1)
    tri_pre = (ci <= ri).astype(BF16)
    tri_suf = (ci >= ri).astype(BF16)
    mask_t = (ri <= ci, ri >= ci)
    fwd_lane = lax.broadcasted_iota(jnp.int32, (L, LANES), 1) < 2 * H_B

    for c in range(nc):
        rows = slice(c * L, (c + 1) * L)
        g = gt_ref[rows, :] + mlb_ref[...]
        gf = jnp.minimum(g, 0.0) - jnp.log1p(jnp.exp(-jnp.abs(g)))
        hi, mid, lo = _split3(gf)
        pre = _dot(tri_pre, hi) + _dot(tri_pre, mid) + _dot(tri_pre, lo)
        suf = _dot(tri_suf, hi) + _dot(tri_suf, mid) + _dot(tri_suf, lo)
        b = pltpu.roll(jnp.where(fwd_lane, pre, suf), LANES - H_B, 1)
        r = g - b
        r_s[rows, :] = r
        rt_s[c] = r.T
        bt_s[c] = b.T

    for d in range(2):
        for h in range(H_B):
            ct_s[d, h] = c0_ref[d, h].T if has_state else jnp.zeros((DH, DH), F32)
    if has_state:
        n_s[...] = n0_ref[...]
        m_s[...] = m0_ref[...]
    else:
        n_s[...] = jnp.zeros_like(n_s)
        m_s[...] = jnp.zeros_like(m_s)

    pairs = [(d, h) for d in range(2) for h in range(H_B)]

    def step(c, carry):
        rows, r_all, rt_all, bt_all = [], [], [], []
        for d in range(2):
            cc = c if d == 0 else nc - 1 - c
            rows.append(pl.ds(pl.multiple_of(cc * L, L), L))
            r_all.append(r_s[rows[d], :])
            rt_all.append(rt_s[cc])
            bt_all.append(bt_s[cc])

        qb, kb, v_t, rm, r_row, b_row, a_row, qk = {}, {}, {}, {}, {}, {}, {}, {}
        for p in pairs:
            d, h = p
            col = 2 * H_B * d + h
            sl = slice(h * DH, (h + 1) * DH)
            qb[p] = q_ref[rows[d], sl].astype(BF16)
            kb[p] = (k_ref[rows[d], sl].astype(F32) * kscale).astype(BF16)
            v_t[p] = v_ref[rows[d], sl].astype(F32).T
            r_row[p] = rt_all[d][col:col + 1, :]
            b_row[p] = bt_all[d][col:col + 1, :]
            rm[p] = jnp.where(mask_t[d], jnp.broadcast_to(r_all[d][:, col:col + 1], (L, L)), NEG_INF)
            a_row[p] = jnp.max(rm[p], axis=0, keepdims=True)
            qk[p] = _dot_nt(kb[p], qb[p])

        m_row = {(d, h): m_s[d, h:h + 1, :] for d, h in pairs}
        n_row = {(d, h): n_s[d, h:h + 1, :] for d, h in pairs}
        ct = {(d, h): ct_s[d, h] for d, h in pairs}
        big_m = {p: jnp.maximum(m_row[p], a_row[p]) for p in pairs}
        w_in = {p: jnp.exp(m_row[p] - big_m[p]) for p in pairs}
        cq = {p: _dot_nt(ct[p].astype(BF16), qb[p]) for p in pairs}
        qn = {p: _dot_nt(jnp.broadcast_to(n_row[p], (8, DH)).astype(BF16), qb[p])[0:1, :] for p in pairs}
        w_t = {p: jnp.exp(rm[p] - big_m[p]) * qk[p] for p in pairs}
        pv = {p: _dot(v_t[p].astype(BF16), w_t[p].astype(BF16)) for p in pairs}
        for p in pairs:
            d, h = p
            den = w_in[p] * qn[p] + jnp.sum(w_t[p], axis=0, keepdims=True)
            h_t = (w_in[p] * cq[p] + pv[p]) / jnp.maximum(jnp.abs(den), jnp.exp(-(b_row[p] + big_m[p])))
            (hf_s if d == 0 else hb_s)[rows[d], slice(h * DH, (h + 1) * DH)] = h_t.T

        m_last, ws = {}, {}
        for p in pairs:
            m_last[p] = jnp.maximum(m_row[p], jnp.max(r_row[p], axis=-1, keepdims=True))
            ws[p] = jnp.exp(r_row[p] - m_last[p])
        kv = {p: _dot((v_t[p] * ws[p]).astype(BF16), kb[p]) for p in pairs}
        nk = {p: _dot(jnp.broadcast_to(ws[p], (8, L)).astype(BF16), kb[p])[0:1, :] for p in pairs}
        for p in pairs:
            d, h = p
            a_prev = jnp.exp(m_row[p] - m_last[p])
            bl = b_row[p][:, L - 1:L] if d == 0 else b_row[p][:, 0:1]
            ct_s[d, h] = a_prev * ct[p] + kv[p]
            n_s[d, h:h + 1, :] = a_prev * n_row[p] + nk[p]
            m_s[d, h:h + 1, :] = bl + m_last[p]
        return carry

    lax.fori_loop(0, nc, step, 0)

    if emit_state:
        for d in range(2):
            for h in range(H_B):
                co_ref[d, h] = ct_s[d, h].T
        no_ref[...] = n_s[...]
        mo_ref[...] = m_s[...]

    rows_ep = 256
    for r in range(seq // rows_ep):
        rr = slice(r * rows_ep, (r + 1) * rows_ep)
        for h in range(H_B):
            sl = slice(h * DH, (h + 1) * DH)
            hm = hf_s[rr, sl] + hb_s[rr, sl]
            ob = ob_ref[rr, sl].astype(F32)
            o_ref[rr, sl] = (_sigmoid(ob) * _rms_lanes(hm, mlg_ref[:, sl])).astype(o_ref.dtype)


def _mlstm(proj, gates, ml_b, ml_g, batch, seq, state=None, emit_state=False, cast=None):
    t = proj.shape[0]
    col0 = 3 * W_A // W_B
    big = lambda j: pl.BlockSpec((seq, W_B), lambda b: (b, col0 + j))
    in_specs = [big(0), big(1), big(2), big(3),
                pl.BlockSpec((seq, LANES), lambda b: (b, 0)),
                pl.BlockSpec((1, LANES), lambda b: (0, 0)),
                pl.BlockSpec((1, W_B), lambda b: (0, 0))]
    args = [proj, proj, proj, proj, gates, ml_b, ml_g.reshape(1, W_B)]
    c_spec = pl.BlockSpec((None, None, 2, H_B, DH, DH), lambda b: (b, 0, 0, 0, 0, 0))
    n_spec = pl.BlockSpec((None, None, 2, H_B, DH), lambda b: (b, 0, 0, 0, 0))
    if state is not None:
        in_specs += [c_spec, n_spec, n_spec]
        args += list(state)
    out_specs = [pl.BlockSpec((seq, W_B), lambda b: (b, 0))]
    out_shape = [jax.ShapeDtypeStruct((t, W_B), BF16)]
    if emit_state:
        out_specs += [c_spec, n_spec, n_spec]
        out_shape += [jax.ShapeDtypeStruct((batch, 1, 2, H_B, DH, DH), F32),
                      jax.ShapeDtypeStruct((batch, 1, 2, H_B, DH), F32),
                      jax.ShapeDtypeStruct((batch, 1, 2, H_B, LANES), F32)]
    if cast is not None:
        c_in, c_out, c_shape = _cast_job(*cast, lambda b: b, batch)
        in_specs.append(c_in)
        args.append(cast[0])
        out_specs.append(c_out)
        out_shape.append(c_shape)
    nc = seq // ML_CHUNK
    return pl.pallas_call(
        functools.partial(_mlstm_kernel, has_state=state is not None, emit_state=emit_state,
                          has_cast=cast is not None, seq=seq),
        grid=(batch,),
        in_specs=in_specs,
        out_specs=out_specs,
        out_shape=out_shape,
        scratch_shapes=[
            pltpu.VMEM((seq, LANES), F32),
            pltpu.VMEM((nc, ML_CHUNK, LANES), F32),
            pltpu.VMEM((nc, ML_CHUNK, LANES), F32),
            pltpu.VMEM((seq, W_B), F32),
            pltpu.VMEM((seq, W_B), F32),
            pltpu.VMEM((2, H_B, DH, DH), F32),
            pltpu.VMEM((2, H_B, DH), F32),
            pltpu.VMEM((2, H_B, LANES), F32),
        ],
        compiler_params=_cparams("arbitrary"),
        name="mlstm",
    )(*args)


DIFF_QB = 256


def _diff_kernel(*refs, heads, seq, past, rope, emit_ctx, has_cast, lam_init):
    it = iter(refs)
    q_ref, k_ref, v_ref, gq_ref, gk_ref, go_ref, lam_ref = (next(it) for _ in range(7))
    if past:
        kc_ref, vc_ref = next(it), next(it)
    if rope:
        cos_ref, s1_ref, s2_ref = next(it), next(it), next(it)
    if has_cast:
        wc_ref = next(it)
    o_ref = next(it)
    if emit_ctx:
        ko_ref, vo_ref = next(it), next(it)
    if has_cast:
        wo_ref = next(it)
        wo_ref[...] = wc_ref[...].astype(BF16)

    scale = DQK ** -0.5
    lv = lam_ref[...]
    lam = (jnp.exp(jnp.sum(lv[0:1, :] * lv[1:2, :], axis=-1, keepdims=True))
           - jnp.exp(jnp.sum(lv[2:3, :] * lv[3:4, :], axis=-1, keepdims=True)) + lam_init)
    low = lax.broadcasted_iota(jnp.int32, (seq, LANES), 1) < DQK
    hr = lax.broadcasted_iota(jnp.int32, (LANES, LANES), 0) < DQK
    hc = lax.broadcasted_iota(jnp.int32, (LANES, LANES), 1) < DQK
    halves = (hr == hc).astype(BF16)

    def apply_rope(x):
        return (x * cos_ref[...] + pltpu.roll(x, LANES - DQK // 4, 1) * s1_ref[...]
                + pltpu.roll(x, DQK // 4, 1) * s2_ref[...])

    qb_rows = min(DIFF_QB, seq)
    blocks = [slice(i * qb_rows, (i + 1) * qb_rows) for i in range(seq // qb_rows)]
    head_sl = [slice(hh * DH, (hh + 1) * DH) for hh in range(heads)]
    qn = [_rms_lane_groups(q_ref[:, sl].astype(F32), gq_ref[...], halves, DQK) for sl in head_sl]
    kn = [_rms_lane_groups(k_ref[:, sl].astype(F32), gk_ref[...], halves, DQK) for sl in head_sl]
    v = [v_ref[:, sl].astype(F32) for sl in head_sl]
    if emit_ctx:
        for hh in range(heads):
            ko_ref[hh] = kn[hh]
            vo_ref[hh] = v[hh]
    if rope:
        qn = [apply_rope(x) for x in qn]
        kn = [apply_rope(x) for x in kn]
    assert math.frexp(scale)[0] == 0.5
    q0 = [(jnp.where(low, x, 0.0) * scale).astype(BF16) for x in qn]
    q1 = [(jnp.where(low, 0.0, x) * scale).astype(BF16) for x in qn]
    ka, vt = [], []
    for hh in range(heads):
        k_parts = [kn[hh].astype(BF16)]
        vt_parts = [v[hh][c * LANES:(c + 1) * LANES, :].T.astype(BF16) for c in range(seq // LANES)]
        if past:
            k_parts.append(kc_ref[hh].astype(BF16))
            vt_parts += [vc_ref[hh, c * LANES:(c + 1) * LANES, :].T.astype(BF16) for c in range(past // LANES)]
        ka.append(jnp.concatenate(k_parts, axis=0))
        vt.append(jnp.concatenate(vt_parts, axis=1))

    work = [(hh, rows) for hh in range(heads) for rows in blocks]
    s0 = [_dot_nt(ka[hh], q0[hh][rows, :]) for hh, rows in work]
    s1 = [_dot_nt(ka[hh], q1[hh][rows, :]) for hh, rows in work]

    def softmax_t(s, weight):
        p = jnp.exp(s - jnp.max(s, axis=0, keepdims=True))
        return p * (weight / jnp.sum(p, axis=0, keepdims=True))

    a_t = [(softmax_t(a, 1.0) - softmax_t(b, lam)).astype(BF16) for a, b in zip(s0, s1)]
    out_t = [_dot(vt[hh], a) for (hh, _), a in zip(work, a_t)]
    for (hh, rows), o in zip(work, out_t):
        o = o * lax.rsqrt(jnp.mean(o * o, axis=0, keepdims=True) + EPS)
        g_row = go_ref[:, head_sl[hh]] * (1.0 - lam_init)
        for c in range(qb_rows // LANES):
            r0 = rows.start + c * LANES
            o_ref[r0:r0 + LANES, head_sl[hh]] = (o[:, c * LANES:(c + 1) * LANES].T * g_row).astype(o_ref.dtype)


def _diff_attn(proj, gq, gk, g_out, lam_vec, lam_init, batch, seq, heads, cache=None, rope=None,
               emit_ctx=False, cast=None):
    t = proj.shape[0]
    nh = H_C // heads
    w = heads * DH
    kblk = W_CQK // w
    past = 0 if cache is None else cache[0].shape[3]
    in_specs = [
        pl.BlockSpec((seq, w), lambda b, j: (b, j)),
        pl.BlockSpec((seq, w), lambda b, j: (b, kblk + j)),
        pl.BlockSpec((seq, w), lambda b, j: (b, 2 * kblk + j)),
        pl.BlockSpec((1, DH), lambda b, j: (0, 0)),
        pl.BlockSpec((1, DH), lambda b, j: (0, 0)),
        pl.BlockSpec((1, w), lambda b, j: (0, j)),
        pl.BlockSpec((4, DQK), lambda b, j: (0, 0)),
    ]
    args = [proj, proj, proj, jnp.tile(gq, 2).reshape(1, DH), jnp.tile(gk, 2).reshape(1, DH),
            g_out.reshape(1, W_CV), lam_vec]
    if cache is not None:
        ctx_spec = pl.BlockSpec((None, None, heads, past, DH), lambda b, j: (b, 0, j, 0, 0))
        in_specs += [ctx_spec, ctx_spec]
        args += [cache[0].reshape(cache[0].shape[:4] + (DH,)), cache[1]]
    if rope is not None:
        tab = pl.BlockSpec((seq, DH), lambda b, j: (0, 0))
        in_specs += [tab, tab, tab]
        args += list(rope)
    out_specs = [pl.BlockSpec((seq, w), lambda b, j: (b, j))]
    out_shape = [jax.ShapeDtypeStruct((t, W_CV), BF16)]
    if emit_ctx:
        kv_spec = pl.BlockSpec((None, None, heads, seq, DH), lambda b, j: (b, 0, j, 0, 0))
        kv_shape = jax.ShapeDtypeStruct((batch, 1, H_C, seq, DH), F32)
        out_specs += [kv_spec, kv_spec]
        out_shape += [kv_shape, kv_shape]
    if cast is not None:
        c_in, c_out, c_shape = _cast_job(*cast, lambda b, j: b * nh + j, batch * nh)
        in_specs.append(c_in)
        args.append(cast[0])
        out_specs.append(c_out)
        out_shape.append(c_shape)
    return pl.pallas_call(
        functools.partial(_diff_kernel, heads=heads, seq=seq, past=past, rope=rope is not None,
                          emit_ctx=emit_ctx, has_cast=cast is not None, lam_init=lam_init),
        grid=(batch, nh),
        in_specs=in_specs,
        out_specs=out_specs,
        out_shape=out_shape,
        compiler_params=_cparams("arbitrary", "arbitrary"),
        name="diff_attn",
    )(*args)


def _rope_tables(seq):
    tpos = jnp.arange(seq)
    pos = jnp.stack([tpos // GRID_W, tpos % GRID_W], axis=-1).astype(F32)
    half = DQK // 2
    inv = ROPE_THETA ** (-jnp.arange(0, half, 2, dtype=F32) / half)
    ang = pos[:, :, None] * inv
    cos = jnp.cos(ang)
    sin = jnp.sin(ang)
    zero = jnp.zeros_like(sin)
    lay = lambda first, second: jnp.tile(jnp.concatenate([first, second], axis=-1).reshape(seq, DQK), (1, 2))
    return lay(cos, cos), lay(-sin, zero), lay(zero, sin)


SGU_TM = 512


def _sgu_kernel(*refs, has_cast):
    if has_cast:
        u_ref, vd_ref, sgg_ref, sgw_ref, sgb_ref, wc_ref, o_ref, wo_ref = refs
        wo_ref[...] = wc_ref[...].astype(BF16)
    else:
        u_ref, vd_ref, sgg_ref, sgw_ref, sgb_ref, o_ref = refs
    tm = u_ref.shape[0]
    for c in range(tm // SG_CHUNK):
        rows = slice(c * SG_CHUNK, (c + 1) * SG_CHUNK)
        vn = _rms_lanes(_gelu_tanh(vd_ref[rows, :].astype(F32)), sgg_ref[...]).astype(BF16)
        for g in range(G_D):
            sl = slice(g * DH, (g + 1) * DH)
            gate = _dot(sgw_ref[g].astype(BF16), vn[:, sl]) + sgb_ref[:, g:g + 1]
            o_ref[rows, sl] = (_gelu_tanh(u_ref[rows, sl].astype(F32)) * gate).astype(o_ref.dtype)


def _sgu(proj, sg_g, sg_w, sg_b, cast=None):
    t = proj.shape[0]
    tm = SGU_TM
    ublk = (2 * W_CQK + W_CV) // W_D
    in_specs = [
        pl.BlockSpec((tm, W_D), lambda i: (i, ublk)),
        pl.BlockSpec((tm, W_D), lambda i: (i, ublk + 1)),
        pl.BlockSpec((1, W_D), lambda i: (0, 0)),
        pl.BlockSpec((G_D, SG_CHUNK, SG_CHUNK), lambda i: (0, 0, 0)),
        pl.BlockSpec((SG_CHUNK, G_D), lambda i: (0, 0)),
    ]
    args = [proj, proj, sg_g.reshape(1, W_D), sg_w, sg_b.T]
    out_specs = [pl.BlockSpec((tm, W_D), lambda i: (i, 0))]
    out_shape = [jax.ShapeDtypeStruct((t, W_D), BF16)]
    if cast is not None:
        c_in, c_out, c_shape = _cast_job(*cast, lambda i: i, t // tm)
        in_specs.append(c_in)
        args.append(cast[0])
        out_specs.append(c_out)
        out_shape.append(c_shape)
    return pl.pallas_call(
        functools.partial(_sgu_kernel, has_cast=cast is not None),
        grid=(t // tm,),
        in_specs=in_specs,
        out_specs=out_specs,
        out_shape=out_shape,
        compiler_params=_cparams("arbitrary"),
        name="sgu",
    )(*args)


def _lambda_init(layer):
    return 0.8 - 0.6 * math.exp(-0.3 * layer)


def _trunk(x, mods, p, batch, seq, caches, ffn_w):
    ctx_out = {}
    for l in range(DEPTH):
        mod = mods[l]
        if l % 2 == 0:
            proj, gates = _inproj(x, p["g_mix"][l], mod, 1, 0, p["w_even"], N_EVEN_MAIN, N_EVEN_MAIN // LANES)
            if caches is None:
                mix_a, ctx_out["na_k"], ctx_out["na_v"], w_up = _attn_ctx(
                    proj, p["na_gq"], p["na_gk"], batch, seq, cast=(p["ffn_w_up"], l, 2))
                mix_b, c1, n1, m1, w_down = _mlstm(proj, gates, p["ml_b"], p["ml_g"], batch, seq, emit_state=True,
                                                   cast=(p["ffn_w_down"], l, 1))
                ffn_w[l] = (w_up, w_down)
                ctx_out.update(mlstm_C=c1, mlstm_n=n1, mlstm_m=m1[..., 0])
            else:
                mix_a = _natten(proj, caches["na_k"], caches["na_v"], p["na_gq"], p["na_gk"], p["na_rpb"], batch, seq)
                (mix_b,) = _mlstm(proj, gates, p["ml_b"], p["ml_g"], batch, seq,
                                  state=(caches["C"], caches["n"], caches["m"]))
        else:
            proj = _inproj(x, p["g_mix"][l], mod, 1, 0, p["w_odd"], p["w_odd"].shape[1])
            lam_init = _lambda_init(l)
            if caches is None:
                mix_a, dk, dv, w_up = _diff_attn(proj, p["diff_gq"], p["diff_gk"], p["diff_g_out"], p["diff_lam"],
                                                 lam_init, batch, seq, heads=H_C, emit_ctx=True,
                                                 cast=(p["ffn_w_up"], l, 2))
                ctx_out["diff_k"] = dk.reshape(batch, 1, H_C, seq, 2, DQK)
                ctx_out["diff_v"] = dv
            else:
                (mix_a,) = _diff_attn(proj, p["diff_gq"], p["diff_gk"], p["diff_g_out"], p["diff_lam"],
                                      lam_init, batch, seq, heads=1,
                                      cache=(caches["diff_k"], caches["diff_v"]), rope=_rope_tables(seq))
            if caches is None:
                mix_b, w_down = _sgu(proj, p["sg_g"], p["sg_w"], p["sg_b"], cast=(p["ffn_w_down"], l, 1))
                ffn_w[l] = (w_up, w_down)
            else:
                (mix_b,) = _sgu(proj, p["sg_g"], p["sg_w"], p["sg_b"])
        x = _outproj(mix_a, mix_b, p["w_out"], l, x, mod, 2)
        x = _ffn(x, p["g_ffn"][l], mod, ffn_w[l][0], p["conv_w"][l], p["conv_b"][l], ffn_w[l][1], seq)
    return x, ctx_out


def kernel(x_prompt, x_sample, cache_na_k, cache_na_v, state_mlstm_C, state_mlstm_n, state_mlstm_m,
           cache_diff_k, cache_diff_v, c, c_ctx, w_mod, b_mod, g_mix, g_ffn, w_out, w_in_even,
           na_gq, na_gk, na_rpb, ml_b_gates, ml_g_out, w_in_odd, diff_gq, diff_gk, diff_lam,
           diff_g_out, sg_g_v, sg_w, sg_b, ffn_w_up, ffn_conv_w, ffn_conv_b, ffn_w_down):
    batch, seq, _ = x_prompt.shape
    dbatch, dseq, _ = x_sample.shape

    cond8 = jnp.zeros((8, D_MODEL), F32).at[0].set(c_ctx).at[1:1 + dbatch].set(c)
    mod = _modulation(cond8, w_mod, b_mod)
    mods_ctx = [mod[l, 0:1].reshape(1, 6, 1, D_MODEL) for l in range(DEPTH)]
    mods_lat = [mod[l, 1:1 + dbatch].reshape(dbatch, 6, 1, D_MODEL) for l in range(DEPTH)]

    p = dict(
        g_mix=g_mix, g_ffn=g_ffn,
        w_even=jnp.pad(w_in_even[0], ((0, 0), (0, LANES - N_GATES))).astype(BF16),
        w_odd=w_in_odd[0].astype(BF16),
        w_out=w_out.astype(BF16), ffn_w_up=ffn_w_up, ffn_w_down=ffn_w_down,
        conv_w=ffn_conv_w, conv_b=ffn_conv_b,
        na_gq=na_gq[0], na_gk=na_gk[0], na_rpb=na_rpb[0],
        ml_b=jnp.pad(ml_b_gates[0], (0, LANES - N_GATES)).reshape(1, LANES), ml_g=ml_g_out[0],
        diff_gq=diff_gq[0], diff_gk=diff_gk[0], diff_lam=diff_lam[0], diff_g_out=diff_g_out[0],
        sg_g=sg_g_v[0], sg_w=sg_w[0], sg_b=sg_b[0],
    )

    ffn_w = {}
    y_prompt, ctx = _trunk(x_prompt.reshape(batch * seq, D_MODEL), mods_ctx, p, batch, seq, None, ffn_w)
    caches = dict(
        na_k=cache_na_k, na_v=cache_na_v, C=state_mlstm_C, n=state_mlstm_n,
        m=jnp.broadcast_to(state_mlstm_m[..., None], state_mlstm_m.shape + (LANES,)),
        diff_k=cache_diff_k, diff_v=cache_diff_v,
    )
    y_sample, _ = _trunk(x_sample.reshape(dbatch * dseq, D_MODEL), mods_lat, p, dbatch, dseq, caches, ffn_w)
    return (y_prompt.reshape(batch, seq, D_MODEL), y_sample.reshape(dbatch, dseq, D_MODEL),
            ctx["na_k"], ctx["na_v"], ctx["mlstm_C"], ctx["mlstm_n"], ctx["mlstm_m"],
            ctx["diff_k"], ctx["diff_v"])
```

```python
import functools
import math

import numpy as np
import jax
import jax.numpy as jnp
from jax import lax
from jax.experimental import pallas as pl
from jax.experimental.pallas import tpu as pltpu

F32 = jnp.float32
BF16 = jnp.bfloat16

D_MODEL = 2048
DEPTH = 2
GRID_W = 64
DH = 128
H_A = 8
W_A = H_A * DH
WIN_H = 8
WIN_W = 16
H_B = 8
W_B = H_B * DH
ML_CHUNK = 128
H_C = 8
DQK = 64
W_CQK = H_C * 2 * DQK
W_CV = H_C * DH
ROPE_THETA = 10000.0
G_D = 8
SG_CHUNK = 128
W_D = 1024
D_FF = 5632
Q_BLOCK = 128
EPS = 1e-6
N_GATES = 4 * H_B
N_EVEN_MAIN = 3 * W_A + 4 * W_B

LANES = 128
V7X_VMEM_BYTES = 64 * 1024 * 1024
VMEM_LIMIT = V7X_VMEM_BYTES - 8 * 1024 * 1024

NEG_INF = float("-inf")


def _cparams(*sem):
    return pltpu.CompilerParams(dimension_semantics=sem, vmem_limit_bytes=VMEM_LIMIT)


def _dot(a, b):
    return jnp.dot(a, b, preferred_element_type=F32)


def _dot_nt(a, b):
    return lax.dot_general(a, b, (((1,), (1,)), ((), ())), preferred_element_type=F32)


def _rms_lanes(x, g):
    ms = jnp.mean(x * x, axis=-1, keepdims=True)
    return x * lax.rsqrt(ms + EPS) * g


def _rms_lane_groups(x, g, group_ones, width):
    y = x * x
    hi = y.astype(BF16)
    mid = (y - hi.astype(F32)).astype(BF16)
    ss = _dot(hi, group_ones) + _dot(mid, group_ones)
    return x * lax.rsqrt(ss * (1.0 / width) + EPS) * g


def _norm_mod(x, g, scale, shift):
    return _rms_lanes(x, g) * (1.0 + scale) + shift


def _sigmoid(x):
    return 1.0 / (1.0 + jnp.exp(-x))


def _gelu_tanh(x):
    c = math.sqrt(2.0 / math.pi)
    return 0.5 * x * (1.0 + jnp.tanh(c * (x + 0.044715 * (x * x * x))))


CAST_BLK = 512


def _cast_job(w, layer, axis, step_of, n_steps):
    nblk = w.shape[axis] // CAST_BLK
    assert w.shape[axis] % CAST_BLK == 0 and n_steps >= nblk
    blk_of = lambda *g: jnp.minimum(step_of(*g), nblk - 1)
    if axis == 2:
        in_spec = pl.BlockSpec((None, w.shape[1], CAST_BLK), lambda *g: (layer, 0, blk_of(*g)))
        out_spec = pl.BlockSpec((w.shape[1], CAST_BLK), lambda *g: (0, blk_of(*g)))
    else:
        in_spec = pl.BlockSpec((None, CAST_BLK, w.shape[2]), lambda *g: (layer, blk_of(*g), 0))
        out_spec = pl.BlockSpec((CAST_BLK, w.shape[2]), lambda *g: (blk_of(*g), 0))
    return in_spec, out_spec, jax.ShapeDtypeStruct(w.shape[1:], BF16)


MOD_TN = 1024


def _mod_kernel(c_ref, w_ref, b_ref, o_ref):
    c = c_ref[...]
    s = c * _sigmoid(c)
    o_ref[...] = _dot(s.astype(BF16), w_ref[...].astype(BF16)) + b_ref[...]


def _modulation(cond8, w_mod, b_mod):
    n = 6 * D_MODEL
    return pl.pallas_call(
        _mod_kernel,
        grid=(DEPTH, n // MOD_TN),
        in_specs=[
            pl.BlockSpec((8, D_MODEL), lambda l, j: (0, 0)),
            pl.BlockSpec((None, D_MODEL, MOD_TN), lambda l, j: (l, 0, j)),
            pl.BlockSpec((None, 1, MOD_TN), lambda l, j: (l, 0, j)),
        ],
        out_specs=pl.BlockSpec((None, 8, MOD_TN), lambda l, j: (l, 0, j)),
        out_shape=jax.ShapeDtypeStruct((DEPTH, 8, n), F32),
        compiler_params=_cparams("parallel", "arbitrary"),
        name="modulation",
    )(cond8, w_mod, b_mod.reshape(DEPTH, 1, n))


DENSE_TM = 1024
NORM_ROWS = 256
PROJ_DTYPE = BF16


def _inproj_body(x_ref, g_ref, sc_ref, sh_ref, w_ref, o_ref, hb_ref, wg_ref=None, og_ref=None, cast_refs=None):
    j = pl.program_id(1)
    tm = x_ref.shape[0]
    if cast_refs is not None:
        cast_refs[1][...] = cast_refs[0][...].astype(BF16)

    @pl.when(j == 0)
    def _():
        for c in range(tm // NORM_ROWS):
            r = slice(c * NORM_ROWS, (c + 1) * NORM_ROWS)
            h = _norm_mod(x_ref[r, :], g_ref[...], sc_ref[...], sh_ref[...]).astype(BF16)
            hb_ref[r, :] = h
            o_ref[r, :] = _dot(h, w_ref[...]).astype(o_ref.dtype)
            if og_ref is not None:
                og_ref[r, :] = _dot(h, wg_ref[...])

    @pl.when(j > 0)
    def _():
        o_ref[...] = _dot(hb_ref[...], w_ref[...]).astype(o_ref.dtype)


def _inproj_kernel(x_ref, g_ref, sc_ref, sh_ref, w_ref, o_ref, hb_ref):
    _inproj_body(x_ref, g_ref, sc_ref, sh_ref, w_ref, o_ref, hb_ref)


def _inproj_gates_kernel(x_ref, g_ref, sc_ref, sh_ref, w_ref, wg_ref, o_ref, og_ref, hb_ref):
    _inproj_body(x_ref, g_ref, sc_ref, sh_ref, w_ref, o_ref, hb_ref, wg_ref, og_ref)


def _inproj_gates_cast_kernel(x_ref, g_ref, sc_ref, sh_ref, w_ref, wg_ref, wc_ref, o_ref, og_ref, wo_ref, hb_ref):
    _inproj_body(x_ref, g_ref, sc_ref, sh_ref, w_ref, o_ref, hb_ref, wg_ref, og_ref, (wc_ref, wo_ref))


def _mod_spec(idx, tm, rows_per_mod):
    return pl.BlockSpec((None, None, 1, D_MODEL), lambda i, n: ((i * tm) // rows_per_mod, idx, 0, 0))


def _inproj(x, g, mod, scale_idx, shift_idx, w, n, gates_block=None, tn=1024, cast=None):
    t = x.shape[0]
    tm = DENSE_TM
    rows_per_mod = t // mod.shape[0]
    in_specs = [
        pl.BlockSpec((tm, D_MODEL), lambda i, j: (i, 0)),
        pl.BlockSpec((1, D_MODEL), lambda i, j: (0, 0)),
        _mod_spec(scale_idx, tm, rows_per_mod),
        _mod_spec(shift_idx, tm, rows_per_mod),
        pl.BlockSpec((D_MODEL, tn), lambda i, j: (0, j)),
    ]
    out_specs = pl.BlockSpec((tm, tn), lambda i, j: (i, j))
    out_shape = jax.ShapeDtypeStruct((t, n), PROJ_DTYPE)
    args = [x, g.reshape(1, D_MODEL), mod, mod, w]
    kern = _inproj_kernel
    if gates_block is not None:
        in_specs.append(pl.BlockSpec((D_MODEL, LANES), lambda i, j: (0, gates_block)))
        out_specs = [out_specs, pl.BlockSpec((tm, LANES), lambda i, j: (i, 0))]
        out_shape = [out_shape, jax.ShapeDtypeStruct((t, LANES), F32)]
        args.append(w)
        kern = _inproj_gates_kernel
    if cast is not None:
        assert gates_block is not None
        nn = n // tn
        c_in, c_out, c_shape = _cast_job(*cast, lambda i, j: i * nn + j, (t // tm) * nn)
        in_specs.append(c_in)
        args.append(cast[0])
        out_specs.append(c_out)
        out_shape.append(c_shape)
        kern = _inproj_gates_cast_kernel
    return pl.pallas_call(
        kern,
        grid=(t // tm, n // tn),
        in_specs=in_specs,
        out_specs=out_specs,
        out_shape=out_shape,
        scratch_shapes=[pltpu.VMEM((tm, D_MODEL), BF16)],
        compiler_params=_cparams("arbitrary", "arbitrary"),
        name="inproj",
    )(*args)


OUT_TM = 512


def _outproj_kernel(a_ref, b_ref, wa_ref, wb_ref, x_ref, gate_ref, o_ref):
    acc = _dot(a_ref[...], wa_ref[...]) + _dot(b_ref[...], wb_ref[...])
    o_ref[...] = x_ref[...] + gate_ref[...] * acc


def _outproj_cast_kernel(a_ref, b_ref, wa_ref, wb_ref, x_ref, gate_ref, wc_ref, o_ref, wo_ref):
    wo_ref[...] = wc_ref[...].astype(BF16)
    _outproj_kernel(a_ref, b_ref, wa_ref, wb_ref, x_ref, gate_ref, o_ref)


def _outproj(mix_a, mix_b, w, layer, x, mod, gate_idx, cast=None):
    t = x.shape[0]
    tm = OUT_TM
    half = mix_a.shape[1]
    rows_per_mod = t // mod.shape[0]
    extra_in, extra_args, out_specs, out_shape = [], [], pl.BlockSpec((tm, D_MODEL), lambda i: (i, 0)), \
        jax.ShapeDtypeStruct((t, D_MODEL), F32)
    if cast is not None:
        c_in, c_out, c_shape = _cast_job(*cast, lambda i: i, t // tm)
        extra_in, extra_args = [c_in], [cast[0]]
        out_specs, out_shape = [out_specs, c_out], [out_shape, c_shape]
    return pl.pallas_call(
        _outproj_kernel if cast is None else _outproj_cast_kernel,
        grid=(t // tm,),
        in_specs=[
            pl.BlockSpec((tm, half), lambda i: (i, 0)),
            pl.BlockSpec((tm, half), lambda i: (i, 0)),
            pl.BlockSpec((None, half, D_MODEL), lambda i: (layer, 0, 0)),
            pl.BlockSpec((None, half, D_MODEL), lambda i: (layer, 1, 0)),
            pl.BlockSpec((tm, D_MODEL), lambda i: (i, 0)),
            pl.BlockSpec((None, None, 1, D_MODEL), lambda i: ((i * tm) // rows_per_mod, gate_idx, 0, 0)),
        ] + extra_in,
        out_specs=out_specs,
        out_shape=out_shape,
        compiler_params=_cparams("arbitrary"),
        name="outproj",
    )(mix_a, mix_b, w, w, x, mod, *extra_args)


FFN_TF = 512
FFN_ROWS = 512


def _ffn_kernel(x_ref, g_ref, sc_ref, sh_ref, gate_ref, wg_ref, wv_ref, cwg_ref, cwv_ref, cbg_ref, cbv_ref,
                wd_ref, o_ref, hb_ref, *, seq_len):
    f = pl.program_id(1)
    tm = x_ref.shape[0]

    rp = max(seq_len, FFN_ROWS)
    pos = lax.broadcasted_iota(jnp.int32, (rp, 1), 0) & (seq_len - 1)
    first = pos == 0
    last = pos == seq_len - 1

    def conv(a, cw_ref, cb_ref):
        prev = jnp.where(first, 0.0, pltpu.roll(a, 1, 0))
        nxt = jnp.where(last, 0.0, pltpu.roll(a, rp - 1, 0))
        return prev * cw_ref[0:1, :] + a * cw_ref[1:2, :] + nxt * cw_ref[2:3, :] + cb_ref[...]

    parts = [slice(r, r + rp) for r in range(0, tm, rp)]

    def hidden_block(first):
        acts = []
        for rows in parts:
            if first:
                for c in range(rows.start, rows.stop, NORM_ROWS):
                    r = slice(c, c + NORM_ROWS)
                    h = _norm_mod(x_ref[r, :], g_ref[...], sc_ref[...], sh_ref[...])
                    hb_ref[r, :] = h.astype(BF16)
            hb = hb_ref[rows, :]
            cg = conv(_dot(hb, wg_ref[...]), cwg_ref, cbg_ref)
            cv = conv(_dot(hb, wv_ref[...]), cwv_ref, cbv_ref)
            acts.append((cg * _sigmoid(cg) * cv).astype(BF16))
        for rows, act in zip(parts, acts):
            if first:
                o_ref[rows, :] = _dot(act, wd_ref[...])
            else:
                o_ref[rows, :] += _dot(act, wd_ref[...])

    @pl.when(f == 0)
    def _():
        hidden_block(True)

    @pl.when(f > 0)
    def _():
        hidden_block(False)

    @pl.when(f == pl.num_programs(1) - 1)
    def _():
        o_ref[...] = x_ref[...] + gate_ref[...] * o_ref[...]


def _ffn(x, g, mod, w_up, conv_w, conv_b, w_down, seq_len):
    t = x.shape[0]
    tm, tf = DENSE_TM, FFN_TF
    nf = D_FF // tf
    rows_per_mod = t // mod.shape[0]
    conv_b = conv_b.reshape(1, 2 * D_FF)
    assert seq_len & (seq_len - 1) == 0 and tm % seq_len == 0
    return pl.pallas_call(
        functools.partial(_ffn_kernel, seq_len=seq_len),
        grid=(t // tm, nf),
        in_specs=[
            pl.BlockSpec((tm, D_MODEL), lambda i, f: (i, 0), pipeline_mode=pl.Buffered(1)),
            pl.BlockSpec((1, D_MODEL), lambda i, f: (0, 0)),
            _mod_spec(4, tm, rows_per_mod),
            _mod_spec(3, tm, rows_per_mod),
            _mod_spec(5, tm, rows_per_mod),
            pl.BlockSpec((D_MODEL, tf), lambda i, f: (0, f)),
            pl.BlockSpec((D_MODEL, tf), lambda i, f: (0, nf + f)),
            pl.BlockSpec((3, tf), lambda i, f: (0, f)),
            pl.BlockSpec((3, tf), lambda i, f: (0, nf + f)),
            pl.BlockSpec((1, tf), lambda i, f: (0, f)),
            pl.BlockSpec((1, tf), lambda i, f: (0, nf + f)),
            pl.BlockSpec((tf, D_MODEL), lambda i, f: (f, 0)),
        ],
        out_specs=pl.BlockSpec((tm, D_MODEL), lambda i, f: (i, 0)),
        out_shape=jax.ShapeDtypeStruct((t, D_MODEL), F32),
        scratch_shapes=[pltpu.VMEM((tm, D_MODEL), BF16)],
        compiler_params=_cparams("parallel", "arbitrary"),
        name="convffn",
    )(x, g.reshape(1, D_MODEL), mod, mod, mod, w_up, w_up, conv_w, conv_w, conv_b, conv_b, w_down)


def _attn_ctx_kernel(q_ref, k_ref, v_ref, gq_ref, gk_ref, wc_ref, o_ref, ko_ref, vo_ref, wo_ref):
    wo_ref[...] = wc_ref[...].astype(BF16)
    scale = DH ** -0.5
    for h in range(H_A):
        sl = slice(h * DH, (h + 1) * DH)
        q = _rms_lanes(q_ref[:, sl].astype(F32), gq_ref[...])
        k = _rms_lanes(k_ref[:, sl].astype(F32), gk_ref[...])
        v = v_ref[:, sl].astype(F32)
        ko_ref[h] = k
        vo_ref[h] = v
        s = _dot_nt(q.astype(BF16), k.astype(BF16)) * scale
        p = jnp.exp(s - jnp.max(s, axis=-1, keepdims=True))
        p = p / jnp.sum(p, axis=-1, keepdims=True)
        o_ref[:, sl] = _dot(p.astype(BF16), v.astype(BF16)).astype(o_ref.dtype)


def _attn_ctx(proj, gq, gk, batch, seq, cast):
    t = proj.shape[0]
    c_in, c_out, c_shape = _cast_job(*cast, lambda b: b, batch)
    kv_shape = jax.ShapeDtypeStruct((batch, 1, H_A, seq, DH), F32)
    kv_spec = pl.BlockSpec((None, None, H_A, seq, DH), lambda b: (b, 0, 0, 0, 0))
    return pl.pallas_call(
        _attn_ctx_kernel,
        grid=(batch,),
        in_specs=[
            pl.BlockSpec((seq, W_A), lambda b: (b, 0)),
            pl.BlockSpec((seq, W_A), lambda b: (b, 1)),
            pl.BlockSpec((seq, W_A), lambda b: (b, 2)),
            pl.BlockSpec((1, DH), lambda b: (0, 0)),
            pl.BlockSpec((1, DH), lambda b: (0, 0)),
            c_in,
        ],
        out_specs=[pl.BlockSpec((seq, W_A), lambda b: (b, 0)), kv_spec, kv_spec, c_out],
        out_shape=[jax.ShapeDtypeStruct((t, W_A), BF16), kv_shape, kv_shape, c_shape],
        compiler_params=_cparams("arbitrary"),
        name="attn_ctx",
    )(proj, proj, proj, gq.reshape(1, DH), gk.reshape(1, DH), cast[0])


NA_ROWS = 16
NA_PAIRS = 2 * WIN_H - 2
RPB_H = 2 * WIN_H - 1
RPB_W = 2 * WIN_W - 1


def _natten_kernel(rpb_ref, q_ref, k_ref, v_ref, kc_ref, vc_ref, gq_ref, gk_ref, o_ref,
                   qn_s, kn_s, v_s, kc_s, vc_s, bias_s):
    h = pl.program_id(1)
    scale = DH ** -0.5
    qn_s[...] = _rms_lanes(q_ref[...].astype(F32), gq_ref[...]).astype(BF16)
    kn_s[...] = _rms_lanes(k_ref[...].astype(F32), gk_ref[...]).astype(BF16)
    v_s[...] = v_ref[...].astype(BF16)
    kc_s[...] = kc_ref[...].astype(BF16)
    vc_s[...] = vc_ref[...].astype(BF16)

    lane = lax.broadcasted_iota(jnp.int32, (GRID_W, LANES), 1)
    qc = lax.broadcasted_iota(jnp.int32, (GRID_W, LANES), 0)
    kcol = lane % GRID_W
    upper = lane >= GRID_W
    dcol = kcol - qc + (WIN_W - 1)
    col0 = jnp.clip(qc - WIN_W // 2, 0, GRID_W - WIN_W)
    col_ok = (kcol >= col0) & (kcol < col0 + WIN_W)
    base = h * (RPB_H * RPB_W)

    def build(p, carry):
        acc = jnp.zeros((GRID_W, LANES), F32)
        for d in range(RPB_W):
            lo = rpb_ref[base + p * RPB_W + d]
            hi = rpb_ref[base + (p + 1) * RPB_W + d]
            acc = jnp.where(dcol == d, jnp.where(upper, hi, lo), acc)
        bias_s[p] = jnp.where(col_ok, acc, NEG_INF)
        return carry

    lax.fori_loop(0, NA_PAIRS, build, 0)

    kh = WIN_H
    n_loc = kh * GRID_W
    row0 = [min(max(r - kh // 2, 0), NA_ROWS - kh) for r in range(NA_ROWS)]
    q = [qn_s[r * GRID_W:(r + 1) * GRID_W, :] for r in range(NA_ROWS)]
    kc = kc_s[...]
    s_loc, s_ctx = [], []
    for r in range(NA_ROWS):
        dr0 = row0[r] - r + WIN_H - 1
        k_loc = kn_s[row0[r] * GRID_W:row0[r] * GRID_W + n_loc, :]
        bias = jnp.concatenate([bias_s[dr0 + 2 * i] for i in range(kh // 2)], axis=-1)
        s_loc.append(_dot_nt(q[r], k_loc) * scale + bias)
        s_ctx.append(_dot_nt(q[r], kc) * scale)
    p_loc, p_ctx = [], []
    for r in range(NA_ROWS):
        m = jnp.maximum(jnp.max(s_loc[r], axis=-1, keepdims=True), jnp.max(s_ctx[r], axis=-1, keepdims=True))
        e_loc = jnp.exp(s_loc[r] - m)
        e_ctx = jnp.exp(s_ctx[r] - m)
        l = jnp.sum(e_loc, axis=-1, keepdims=True) + jnp.sum(e_ctx, axis=-1, keepdims=True)
        p_loc.append((e_loc / l).astype(BF16))
        p_ctx.append((e_ctx / l).astype(BF16))
    vc = vc_s[...]
    for r in range(NA_ROWS):
        v_loc = v_s[row0[r] * GRID_W:row0[r] * GRID_W + n_loc, :]
        out = _dot(p_loc[r], v_loc) + _dot(p_ctx[r], vc)
        o_ref[r * GRID_W:(r + 1) * GRID_W, :] = out.astype(o_ref.dtype)


def _natten(proj, cache_k, cache_v, gq, gk, rpb, batch, seq):
    t = proj.shape[0]
    past = cache_k.shape[3]
    ctx_spec = pl.BlockSpec((None, None, None, past, DH), lambda b, h: (b, 0, h, 0, 0))
    return pl.pallas_call(
        _natten_kernel,
        grid=(batch, H_A),
        in_specs=[
            pl.BlockSpec(memory_space=pltpu.SMEM),
            pl.BlockSpec((seq, DH), lambda b, h: (b, h)),
            pl.BlockSpec((seq, DH), lambda b, h: (b, H_A + h)),
            pl.BlockSpec((seq, DH), lambda b, h: (b, 2 * H_A + h)),
            ctx_spec,
            ctx_spec,
            pl.BlockSpec((1, DH), lambda b, h: (0, 0)),
            pl.BlockSpec((1, DH), lambda b, h: (0, 0)),
        ],
        out_specs=pl.BlockSpec((seq, DH), lambda b, h: (b, h)),
        out_shape=jax.ShapeDtypeStruct((t, W_A), BF16),
        scratch_shapes=[
            pltpu.VMEM((seq, DH), BF16),
            pltpu.VMEM((seq, DH), BF16),
            pltpu.VMEM((seq, DH), BF16),
            pltpu.VMEM((past, DH), BF16),
            pltpu.VMEM((past, DH), BF16),
            pltpu.VMEM((NA_PAIRS, GRID_W, LANES), F32),
        ],
        compiler_params=_cparams("parallel", "arbitrary"),
        name="natten",
    )(rpb.reshape(-1), proj, proj, proj, cache_k, cache_v, gq.reshape(1, DH), gk.reshape(1, DH))


def _split3(x):
    hi = x.astype(BF16)
    r1 = x - hi.astype(F32)
    mid = r1.astype(BF16)
    lo = (r1 - mid.astype(F32)).astype(BF16)
    return hi, mid, lo


def _mlstm_kernel(*refs, has_state, emit_state, has_cast, seq):
    it = iter(refs)
    q_ref, k_ref, v_ref, ob_ref, gt_ref, mlb_ref, mlg_ref = (next(it) for _ in range(7))
    if has_state:
        c0_ref, n0_ref, m0_ref = (next(it) for _ in range(3))
    if has_cast:
        wc_ref = next(it)
    o_ref = next(it)
    if emit_state:
        co_ref, no_ref, mo_ref = (next(it) for _ in range(3))
    if has_cast:
        wo_ref = next(it)
        wo_ref[...] = wc_ref[...].astype(BF16)
    r_s, rt_s, bt_s, hf_s, hb_s, ct_s, n_s, m_s = (next(it) for _ in range(8))

    L = ML_CHUNK
    nc = seq // L
    kscale = DH ** -0.5

    ri = lax.broadcasted_iota(jnp.int32, (L, L), 0)
    ci = lax.broadcasted_iota(jnp.int32, (L, L), 1)
    tri_pre = (ci <= ri).astype(BF16)
    tri_suf = (ci >= ri).astype(BF16)
    mask_t = (ri <= ci, ri >= ci)
    fwd_lane = lax.broadcasted_iota(jnp.int32, (L, LANES), 1) < 2 * H_B

    for c in range(nc):
        rows = slice(c * L, (c + 1) * L)
        g = gt_ref[rows, :] + mlb_ref[...]
        gf = jnp.minimum(g, 0.0) - jnp.log1p(jnp.exp(-jnp.abs(g)))
        hi, mid, lo = _split3(gf)
        pre = _dot(tri_pre, hi) + _dot(tri_pre, mid) + _dot(tri_pre, lo)
        suf = _dot(tri_suf, hi) + _dot(tri_suf, mid) + _dot(tri_suf, lo)
        b = pltpu.roll(jnp.where(fwd_lane, pre, suf), LANES - H_B, 1)
        r = g - b
        r_s[rows, :] = r
        rt_s[c] = r.T
        bt_s[c] = b.T

    for d in range(2):
        for h in range(H_B):
            ct_s[d, h] = c0_ref[d, h].T if has_state else jnp.zeros((DH, DH), F32)
    if has_state:
        n_s[...] = n0_ref[...]
        m_s[...] = m0_ref[...]
    else:
        n_s[...] = jnp.zeros_like(n_s)
        m_s[...] = jnp.zeros_like(m_s)

    pairs = [(d, h) for d in range(2) for h in range(H_B)]

    def step(c, carry):
        rows, r_all, rt_all, bt_all = [], [], [], []
        for d in range(2):
            cc = c if d == 0 else nc - 1 - c
            rows.append(pl.ds(pl.multiple_of(cc * L, L), L))
            r_all.append(r_s[rows[d], :])
            rt_all.append(rt_s[cc])
            bt_all.append(bt_s[cc])

        qb, kb, v_t, rm, r_row, b_row, a_row, qk = {}, {}, {}, {}, {}, {}, {}, {}
        for p in pairs:
            d, h = p
            col = 2 * H_B * d + h
            sl = slice(h * DH, (h + 1) * DH)
            qb[p] = q_ref[rows[d], sl].astype(BF16)
            kb[p] = (k_ref[rows[d], sl].astype(F32) * kscale).astype(BF16)
            v_t[p] = v_ref[rows[d], sl].astype(F32).T
            r_row[p] = rt_all[d][col:col + 1, :]
            b_row[p] = bt_all[d][col:col + 1, :]
            rm[p] = jnp.where(mask_t[d], jnp.broadcast_to(r_all[d][:, col:col + 1], (L, L)), NEG_INF)
            a_row[p] = jnp.max(rm[p], axis=0, keepdims=True)
            qk[p] = _dot_nt(kb[p], qb[p])

        m_row = {(d, h): m_s[d, h:h + 1, :] for d, h in pairs}
        n_row = {(d, h): n_s[d, h:h + 1, :] for d, h in pairs}
        ct = {(d, h): ct_s[d, h] for d, h in pairs}
        big_m = {p: jnp.maximum(m_row[p], a_row[p]) for p in pairs}
        w_in = {p: jnp.exp(m_row[p] - big_m[p]) for p in pairs}
        cq = {p: _dot_nt(ct[p].astype(BF16), qb[p]) for p in pairs}
        qn = {p: _dot_nt(jnp.broadcast_to(n_row[p], (8, DH)).astype(BF16), qb[p])[0:1, :] for p in pairs}
        w_t = {p: jnp.exp(rm[p] - big_m[p]) * qk[p] for p in pairs}
        pv = {p: _dot(v_t[p].astype(BF16), w_t[p].astype(BF16)) for p in pairs}
        for p in pairs:
            d, h = p
            den = w_in[p] * qn[p] + jnp.sum(w_t[p], axis=0, keepdims=True)
            h_t = (w_in[p] * cq[p] + pv[p]) / jnp.maximum(jnp.abs(den), jnp.exp(-(b_row[p] + big_m[p])))
            (hf_s if d == 0 else hb_s)[rows[d], slice(h * DH, (h + 1) * DH)] = h_t.T

        m_last, ws = {}, {}
        for p in pairs:
            m_last[p] = jnp.maximum(m_row[p], jnp.max(r_row[p], axis=-1, keepdims=True))
            ws[p] = jnp.exp(r_row[p] - m_last[p])
        kv = {p: _dot((v_t[p] * ws[p]).astype(BF16), kb[p]) for p in pairs}
        nk = {p: _dot(jnp.broadcast_to(ws[p], (8, L)).astype(BF16), kb[p])[0:1, :] for p in pairs}
        for p in pairs:
            d, h = p
            a_prev = jnp.exp(m_row[p] - m_last[p])
            bl = b_row[p][:, L - 1:L] if d == 0 else b_row[p][:, 0:1]
            ct_s[d, h] = a_prev * ct[p] + kv[p]
            n_s[d, h:h + 1, :] = a_prev * n_row[p] + nk[p]
            m_s[d, h:h + 1, :] = bl + m_last[p]
        return carry

    lax.fori_loop(0, nc, step, 0)

    if emit_state:
        for d in range(2):
            for h in range(H_B):
                co_ref[d, h] = ct_s[d, h].T
        no_ref[...] = n_s[...]
        mo_ref[...] = m_s[...]

    rows_ep = 256
    for r in range(seq // rows_ep):
        rr = slice(r * rows_ep, (r + 1) * rows_ep)
        for h in range(H_B):
            sl = slice(h * DH, (h + 1) * DH)
            hm = hf_s[rr, sl] + hb_s[rr, sl]
            ob = ob_ref[rr, sl].astype(F32)
            o_ref[rr, sl] = (_sigmoid(ob) * _rms_lanes(hm, mlg_ref[:, sl])).astype(o_ref.dtype)


def _mlstm(proj, gates, ml_b, ml_g, batch, seq, state=None, emit_state=False, cast=None):
    t = proj.shape[0]
    col0 = 3 * W_A // W_B
    big = lambda j: pl.BlockSpec((seq, W_B), lambda b: (b, col0 + j))
    in_specs = [big(0), big(1), big(2), big(3),
                pl.BlockSpec((seq, LANES), lambda b: (b, 0)),
                pl.BlockSpec((1, LANES), lambda b: (0, 0)),
                pl.BlockSpec((1, W_B), lambda b: (0, 0))]
    args = [proj, proj, proj, proj, gates, ml_b, ml_g.reshape(1, W_B)]
    c_spec = pl.BlockSpec((None, None, 2, H_B, DH, DH), lambda b: (b, 0, 0, 0, 0, 0))
    n_spec = pl.BlockSpec((None, None, 2, H_B, DH), lambda b: (b, 0, 0, 0, 0))
    if state is not None:
        in_specs += [c_spec, n_spec, n_spec]
        args += list(state)
    out_specs = [pl.BlockSpec((seq, W_B), lambda b: (b, 0))]
    out_shape = [jax.ShapeDtypeStruct((t, W_B), BF16)]
    if emit_state:
        out_specs += [c_spec, n_spec, n_spec]
        out_shape += [jax.ShapeDtypeStruct((batch, 1, 2, H_B, DH, DH), F32),
                      jax.ShapeDtypeStruct((batch, 1, 2, H_B, DH), F32),
                      jax.ShapeDtypeStruct((batch, 1, 2, H_B, LANES), F32)]
    if cast is not None:
        c_in, c_out, c_shape = _cast_job(*cast, lambda b: b, batch)
        in_specs.append(c_in)
        args.append(cast[0])
        out_specs.append(c_out)
        out_shape.append(c_shape)
    nc = seq // ML_CHUNK
    return pl.pallas_call(
        functools.partial(_mlstm_kernel, has_state=state is not None, emit_state=emit_state,
                          has_cast=cast is not None, seq=seq),
        grid=(batch,),
        in_specs=in_specs,
        out_specs=out_specs,
        out_shape=out_shape,
        scratch_shapes=[
            pltpu.VMEM((seq, LANES), F32),
            pltpu.VMEM((nc, ML_CHUNK, LANES), F32),
            pltpu.VMEM((nc, ML_CHUNK, LANES), F32),
            pltpu.VMEM((seq, W_B), F32),
            pltpu.VMEM((seq, W_B), F32),
            pltpu.VMEM((2, H_B, DH, DH), F32),
            pltpu.VMEM((2, H_B, DH), F32),
            pltpu.VMEM((2, H_B, LANES), F32),
        ],
        compiler_params=_cparams("arbitrary"),
        name="mlstm",
    )(*args)


DIFF_QB = 256


def _diff_kernel(*refs, heads, seq, past, rope, emit_ctx, has_cast, lam_init):
    it = iter(refs)
    q_ref, k_ref, v_ref, gq_ref, gk_ref, go_ref, lam_ref = (next(it) for _ in range(7))
    if past:
        kc_ref, vc_ref = next(it), next(it)
    if rope:
        cos_ref, s1_ref, s2_ref = next(it), next(it), next(it)
    if has_cast:
        wc_ref = next(it)
    o_ref = next(it)
    if emit_ctx:
        ko_ref, vo_ref = next(it), next(it)
    if has_cast:
        wo_ref = next(it)
        wo_ref[...] = wc_ref[...].astype(BF16)

    scale = DQK ** -0.5
    lv = lam_ref[...]
    lam = (jnp.exp(jnp.sum(lv[0:1, :] * lv[1:2, :], axis=-1, keepdims=True))
           - jnp.exp(jnp.sum(lv[2:3, :] * lv[3:4, :], axis=-1, keepdims=True)) + lam_init)
    low = lax.broadcasted_iota(jnp.int32, (seq, LANES), 1) < DQK
    hr = lax.broadcasted_iota(jnp.int32, (LANES, LANES), 0) < DQK
    hc = lax.broadcasted_iota(jnp.int32, (LANES, LANES), 1) < DQK
    halves = (hr == hc).astype(BF16)

    def apply_rope(x):
        return (x * cos_ref[...] + pltpu.roll(x, LANES - DQK // 4, 1) * s1_ref[...]
                + pltpu.roll(x, DQK // 4, 1) * s2_ref[...])

    qb_rows = min(DIFF_QB, seq)
    blocks = [slice(i * qb_rows, (i + 1) * qb_rows) for i in range(seq // qb_rows)]
    head_sl = [slice(hh * DH, (hh + 1) * DH) for hh in range(heads)]
    qn = [_rms_lane_groups(q_ref[:, sl].astype(F32), gq_ref[...], halves, DQK) for sl in head_sl]
    kn = [_rms_lane_groups(k_ref[:, sl].astype(F32), gk_ref[...], halves, DQK) for sl in head_sl]
    v = [v_ref[:, sl].astype(F32) for sl in head_sl]
    if emit_ctx:
        for hh in range(heads):
            ko_ref[hh] = kn[hh]
            vo_ref[hh] = v[hh]
    if rope:
        qn = [apply_rope(x) for x in qn]
        kn = [apply_rope(x) for x in kn]
    assert math.frexp(scale)[0] == 0.5
    q0 = [(jnp.where(low, x, 0.0) * scale).astype(BF16) for x in qn]
    q1 = [(jnp.where(low, 0.0, x) * scale).astype(BF16) for x in qn]
    ka, vt = [], []
    for hh in range(heads):
        k_parts = [kn[hh].astype(BF16)]
        vt_parts = [v[hh][c * LANES:(c + 1) * LANES, :].T.astype(BF16) for c in range(seq // LANES)]
        if past:
            k_parts.append(kc_ref[hh].astype(BF16))
            vt_parts += [vc_ref[hh, c * LANES:(c + 1) * LANES, :].T.astype(BF16) for c in range(past // LANES)]
        ka.append(jnp.concatenate(k_parts, axis=0))
        vt.append(jnp.concatenate(vt_parts, axis=1))

    work = [(hh, rows) for hh in range(heads) for rows in blocks]
    s0 = [_dot_nt(ka[hh], q0[hh][rows, :]) for hh, rows in work]
    s1 = [_dot_nt(ka[hh], q1[hh][rows, :]) for hh, rows in work]

    def softmax_t(s, weight):
        p = jnp.exp(s - jnp.max(s, axis=0, keepdims=True))
        return p * (weight / jnp.sum(p, axis=0, keepdims=True))

    a_t = [(softmax_t(a, 1.0) - softmax_t(b, lam)).astype(BF16) for a, b in zip(s0, s1)]
    out_t = [_dot(vt[hh], a) for (hh, _), a in zip(work, a_t)]
    for (hh, rows), o in zip(work, out_t):
        o = o * lax.rsqrt(jnp.mean(o * o, axis=0, keepdims=True) + EPS)
        g_row = go_ref[:, head_sl[hh]] * (1.0 - lam_init)
        for c in range(qb_rows // LANES):
            r0 = rows.start + c * LANES
            o_ref[r0:r0 + LANES, head_sl[hh]] = (o[:, c * LANES:(c + 1) * LANES].T * g_row).astype(o_ref.dtype)


def _diff_attn(proj, gq, gk, g_out, lam_vec, lam_init, batch, seq, heads, cache=None, rope=None,
               emit_ctx=False, cast=None):
    t = proj.shape[0]
    nh = H_C // heads
    w = heads * DH
    kblk = W_CQK // w
    past = 0 if cache is None else cache[0].shape[3]
    in_specs = [
        pl.BlockSpec((seq, w), lambda b, j: (b, j)),
        pl.BlockSpec((seq, w), lambda b, j: (b, kblk + j)),
        pl.BlockSpec((seq, w), lambda b, j: (b, 2 * kblk + j)),
        pl.BlockSpec((1, DH), lambda b, j: (0, 0)),
        pl.BlockSpec((1, DH), lambda b, j: (0, 0)),
        pl.BlockSpec((1, w), lambda b, j: (0, j)),
        pl.BlockSpec((4, DQK), lambda b, j: (0, 0)),
    ]
    args = [proj, proj, proj, jnp.tile(gq, 2).reshape(1, DH), jnp.tile(gk, 2).reshape(1, DH),
            g_out.reshape(1, W_CV), lam_vec]
    if cache is not None:
        ctx_spec = pl.BlockSpec((None, None, heads, past, DH), lambda b, j: (b, 0, j, 0, 0))
        in_specs += [ctx_spec, ctx_spec]
        args += [cache[0].reshape(cache[0].shape[:4] + (DH,)), cache[1]]
    if rope is not None:
        tab = pl.BlockSpec((seq, DH), lambda b, j: (0, 0))
        in_specs += [tab, tab, tab]
        args += list(rope)
    out_specs = [pl.BlockSpec((seq, w), lambda b, j: (b, j))]
    out_shape = [jax.ShapeDtypeStruct((t, W_CV), BF16)]
    if emit_ctx:
        kv_spec = pl.BlockSpec((None, None, heads, seq, DH), lambda b, j: (b, 0, j, 0, 0))
        kv_shape = jax.ShapeDtypeStruct((batch, 1, H_C, seq, DH), F32)
        out_specs += [kv_spec, kv_spec]
        out_shape += [kv_shape, kv_shape]
    if cast is not None:
        c_in, c_out, c_shape = _cast_job(*cast, lambda b, j: b * nh + j, batch * nh)
        in_specs.append(c_in)
        args.append(cast[0])
        out_specs.append(c_out)
        out_shape.append(c_shape)
    return pl.pallas_call(
        functools.partial(_diff_kernel, heads=heads, seq=seq, past=past, rope=rope is not None,
                          emit_ctx=emit_ctx, has_cast=cast is not None, lam_init=lam_init),
        grid=(batch, nh),
        in_specs=in_specs,
        out_specs=out_specs,
        out_shape=out_shape,
        compiler_params=_cparams("arbitrary", "arbitrary"),
        name="diff_attn",
    )(*args)


def _rope_tables(seq):
    tpos = jnp.arange(seq)
    pos = jnp.stack([tpos // GRID_W, tpos % GRID_W], axis=-1).astype(F32)
    half = DQK // 2
    inv = ROPE_THETA ** (-jnp.arange(0, half, 2, dtype=F32) / half)
    ang = pos[:, :, None] * inv
    cos = jnp.cos(ang)
    sin = jnp.sin(ang)
    zero = jnp.zeros_like(sin)
    lay = lambda first, second: jnp.tile(jnp.concatenate([first, second], axis=-1).reshape(seq, DQK), (1, 2))
    return lay(cos, cos), lay(-sin, zero), lay(zero, sin)


SGU_TM = 512


def _sgu_kernel(*refs, has_cast):
    if has_cast:
        u_ref, vd_ref, sgg_ref, sgw_ref, sgb_ref, wc_ref, o_ref, wo_ref = refs
        wo_ref[...] = wc_ref[...].astype(BF16)
    else:
        u_ref, vd_ref, sgg_ref, sgw_ref, sgb_ref, o_ref = refs
    tm = u_ref.shape[0]
    for c in range(tm // SG_CHUNK):
        rows = slice(c * SG_CHUNK, (c + 1) * SG_CHUNK)
        vn = _rms_lanes(_gelu_tanh(vd_ref[rows, :].astype(F32)), sgg_ref[...]).astype(BF16)
        for g in range(G_D):
            sl = slice(g * DH, (g + 1) * DH)
            gate = _dot(sgw_ref[g].astype(BF16), vn[:, sl]) + sgb_ref[:, g:g + 1]
            o_ref[rows, sl] = (_gelu_tanh(u_ref[rows, sl].astype(F32)) * gate).astype(o_ref.dtype)


def _sgu(proj, sg_g, sg_w, sg_b, cast=None):
    t = proj.shape[0]
    tm = SGU_TM
    ublk = (2 * W_CQK + W_CV) // W_D
    in_specs = [
        pl.BlockSpec((tm, W_D), lambda i: (i, ublk)),
        pl.BlockSpec((tm, W_D), lambda i: (i, ublk + 1)),
        pl.BlockSpec((1, W_D), lambda i: (0, 0)),
        pl.BlockSpec((G_D, SG_CHUNK, SG_CHUNK), lambda i: (0, 0, 0)),
        pl.BlockSpec((SG_CHUNK, G_D), lambda i: (0, 0)),
    ]
    args = [proj, proj, sg_g.reshape(1, W_D), sg_w, sg_b.T]
    out_specs = [pl.BlockSpec((tm, W_D), lambda i: (i, 0))]
    out_shape = [jax.ShapeDtypeStruct((t, W_D), BF16)]
    if cast is not None:
        c_in, c_out, c_shape = _cast_job(*cast, lambda i: i, t // tm)
        in_specs.append(c_in)
        args.append(cast[0])
        out_specs.append(c_out)
        out_shape.append(c_shape)
    return pl.pallas_call(
        functools.partial(_sgu_kernel, has_cast=cast is not None),
        grid=(t // tm,),
        in_specs=in_specs,
        out_specs=out_specs,
        out_shape=out_shape,
        compiler_params=_cparams("arbitrary"),
        name="sgu",
    )(*args)


def _lambda_init(layer):
    return 0.8 - 0.6 * math.exp(-0.3 * layer)


def _trunk(x, mods, p, batch, seq, caches, ffn_w):
    ctx_out = {}
    for l in range(DEPTH):
        mod = mods[l]
        if l % 2 == 0:
            if caches is None:
                proj, gates, w_up1 = _inproj(x, p["g_mix"][l], mod, 1, 0, p["w_even"], N_EVEN_MAIN,
                                             N_EVEN_MAIN // LANES, cast=(p["ffn_w_up"], 1, 2))
                mix_a, ctx_out["na_k"], ctx_out["na_v"], w_down0 = _attn_ctx(
                    proj, p["na_gq"], p["na_gk"], batch, seq, cast=(p["ffn_w_down"], 0, 1))
                mix_b, c1, n1, m1, w_up0 = _mlstm(proj, gates, p["ml_b"], p["ml_g"], batch, seq, emit_state=True,
                                                  cast=(p["ffn_w_up"], 0, 2))
                ffn_w[0] = [w_up0, w_down0]
                ffn_w[1] = [w_up1, None]
                ctx_out.update(mlstm_C=c1, mlstm_n=n1, mlstm_m=m1[..., 0])
            else:
                proj, gates = _inproj(x, p["g_mix"][l], mod, 1, 0, p["w_even"], N_EVEN_MAIN, N_EVEN_MAIN // LANES)
                mix_a = _natten(proj, caches["na_k"], caches["na_v"], p["na_gq"], p["na_gk"], p["na_rpb"], batch, seq)
                (mix_b,) = _mlstm(proj, gates, p["ml_b"], p["ml_g"], batch, seq,
                                  state=(caches["C"], caches["n"], caches["m"]))
        else:
            proj = _inproj(x, p["g_mix"][l], mod, 1, 0, p["w_odd"], p["w_odd"].shape[1])
            lam_init = _lambda_init(l)
            if caches is None:
                mix_a, dk, dv = _diff_attn(proj, p["diff_gq"], p["diff_gk"], p["diff_g_out"], p["diff_lam"],
                                           lam_init, batch, seq, heads=H_C, emit_ctx=True)
                ctx_out["diff_k"] = dk.reshape(batch, 1, H_C, seq, 2, DQK)
                ctx_out["diff_v"] = dv
            else:
                (mix_a,) = _diff_attn(proj, p["diff_gq"], p["diff_gk"], p["diff_g_out"], p["diff_lam"],
                                      lam_init, batch, seq, heads=1,
                                      cache=(caches["diff_k"], caches["diff_v"]), rope=_rope_tables(seq))
            (mix_b,) = _sgu(proj, p["sg_g"], p["sg_w"], p["sg_b"])
        if caches is None and l == 0:
            x, ffn_w[1][1] = _outproj(mix_a, mix_b, p["w_out"], l, x, mod, 2, cast=(p["ffn_w_down"], 1, 1))
        else:
            x = _outproj(mix_a, mix_b, p["w_out"], l, x, mod, 2)
        x = _ffn(x, p["g_ffn"][l], mod, ffn_w[l][0], p["conv_w"][l], p["conv_b"][l], ffn_w[l][1], seq)
    return x, ctx_out


def kernel(x_prompt, x_sample, cache_na_k, cache_na_v, state_mlstm_C, state_mlstm_n, state_mlstm_m,
           cache_diff_k, cache_diff_v, c, c_ctx, w_mod, b_mod, g_mix, g_ffn, w_out, w_in_even,
           na_gq, na_gk, na_rpb, ml_b_gates, ml_g_out, w_in_odd, diff_gq, diff_gk, diff_lam,
           diff_g_out, sg_g_v, sg_w, sg_b, ffn_w_up, ffn_conv_w, ffn_conv_b, ffn_w_down):
    batch, seq, _ = x_prompt.shape
    dbatch, dseq, _ = x_sample.shape

    cond8 = jnp.zeros((8, D_MODEL), F32).at[0].set(c_ctx).at[1:1 + dbatch].set(c)
    mod = _modulation(cond8, w_mod, b_mod)
    mods_ctx = [mod[l, 0:1].reshape(1, 6, 1, D_MODEL) for l in range(DEPTH)]
    mods_lat = [mod[l, 1:1 + dbatch].reshape(dbatch, 6, 1, D_MODEL) for l in range(DEPTH)]

    p = dict(
        g_mix=g_mix, g_ffn=g_ffn,
        w_even=jnp.pad(w_in_even[0], ((0, 0), (0, LANES - N_GATES))).astype(BF16),
        w_odd=w_in_odd[0].astype(BF16),
        w_out=w_out.astype(BF16), ffn_w_up=ffn_w_up, ffn_w_down=ffn_w_down,
        conv_w=ffn_conv_w, conv_b=ffn_conv_b,
        na_gq=na_gq[0], na_gk=na_gk[0], na_rpb=na_rpb[0],
        ml_b=jnp.pad(ml_b_gates[0], (0, LANES - N_GATES)).reshape(1, LANES), ml_g=ml_g_out[0],
        diff_gq=diff_gq[0], diff_gk=diff_gk[0], diff_lam=diff_lam[0], diff_g_out=diff_g_out[0],
        sg_g=sg_g_v[0], sg_w=sg_w[0], sg_b=sg_b[0],
    )

    ffn_w = {}
    y_prompt, ctx = _trunk(x_prompt.reshape(batch * seq, D_MODEL), mods_ctx, p, batch, seq, None, ffn_w)
    caches = dict(
        na_k=cache_na_k, na_v=cache_na_v, C=state_mlstm_C, n=state_mlstm_n,
        m=jnp.broadcast_to(state_mlstm_m[..., None], state_mlstm_m.shape + (LANES,)),
        diff_k=cache_diff_k, diff_v=cache_diff_v,
    )
    y_sample, _ = _trunk(x_sample.reshape(dbatch * dseq, D_MODEL), mods_lat, p, dbatch, dseq, caches, ffn_w)
    return (y_prompt.reshape(batch, seq, D_MODEL), y_sample.reshape(dbatch, dseq, D_MODEL),
            ctx["na_k"], ctx["na_v"], ctx["mlstm_C"], ctx["mlstm_n"], ctx["mlstm_m"],
            ctx["diff_k"], ctx["diff_v"])
```

```python
import functools
import math

import numpy as np
import jax
import jax.numpy as jnp
from jax import lax
from jax.experimental import pallas as pl
from jax.experimental.pallas import tpu as pltpu

F32 = jnp.float32
BF16 = jnp.bfloat16

D_MODEL = 2048
DEPTH = 2
GRID_W = 64
DH = 128
H_A = 8
W_A = H_A * DH
WIN_H = 8
WIN_W = 16
H_B = 8
W_B = H_B * DH
ML_CHUNK = 128
H_C = 8
DQK = 64
W_CQK = H_C * 2 * DQK
W_CV = H_C * DH
ROPE_THETA = 10000.0
G_D = 8
SG_CHUNK = 128
W_D = 1024
D_FF = 5632
Q_BLOCK = 128
EPS = 1e-6
N_GATES = 4 * H_B
N_EVEN_MAIN = 3 * W_A + 4 * W_B

LANES = 128
SUBLANES = 8
V7X_VMEM_BYTES = 64 * 1024 * 1024
VMEM_LIMIT = V7X_VMEM_BYTES - 8 * 1024 * 1024

NEG_INF = float("-inf")


def _cparams(*sem):
    return pltpu.CompilerParams(dimension_semantics=sem, vmem_limit_bytes=VMEM_LIMIT)


def _dot(a, b):
    return jnp.dot(a, b, preferred_element_type=F32)


def _dot_nt(a, b):
    return lax.dot_general(a, b, (((1,), (1,)), ((), ())), preferred_element_type=F32)


def _rms_lanes(x, g):
    ms = jnp.mean(x * x, axis=-1, keepdims=True)
    return x * lax.rsqrt(ms + EPS) * g


def _rms_lane_groups(x, g, group_ones, width):
    y = x * x
    hi = y.astype(BF16)
    mid = (y - hi.astype(F32)).astype(BF16)
    ss = _dot(hi, group_ones) + _dot(mid, group_ones)
    return x * lax.rsqrt(ss * (1.0 / width) + EPS) * g


def _norm_mod(x, g, scale, shift):
    return _rms_lanes(x, g) * (1.0 + scale) + shift


def _sigmoid(x):
    return 1.0 / (1.0 + jnp.exp(-x))


def _gelu_tanh(x):
    c = math.sqrt(2.0 / math.pi)
    return 0.5 * x * (1.0 + jnp.tanh(c * (x + 0.044715 * (x * x * x))))


CAST_BLK = 512


def _cast_job(w, layer, axis, step_of, n_steps, start=0):
    nblk = w.shape[axis] // CAST_BLK
    assert w.shape[axis] % CAST_BLK == 0 and n_steps >= start + nblk
    blk_of = lambda *g: jnp.clip(step_of(*g) - start, 0, nblk - 1)
    if axis == 2:
        in_spec = pl.BlockSpec((None, w.shape[1], CAST_BLK), lambda *g: (layer, 0, blk_of(*g)))
        out_spec = pl.BlockSpec((w.shape[1], CAST_BLK), lambda *g: (0, blk_of(*g)))
    else:
        in_spec = pl.BlockSpec((None, CAST_BLK, w.shape[2]), lambda *g: (layer, blk_of(*g), 0))
        out_spec = pl.BlockSpec((CAST_BLK, w.shape[2]), lambda *g: (blk_of(*g), 0))
    return in_spec, out_spec, jax.ShapeDtypeStruct(w.shape[1:], BF16)


MOD_TN = 1024


def _mod_kernel(c_ref, w_ref, b_ref, o_ref):
    c = c_ref[...]
    s = c * _sigmoid(c)
    o_ref[...] = _dot(s.astype(BF16), w_ref[...].astype(BF16)) + b_ref[...]


def _modulation(cond8, w_mod, b_mod):
    n = 6 * D_MODEL
    return pl.pallas_call(
        _mod_kernel,
        grid=(DEPTH, n // MOD_TN),
        in_specs=[
            pl.BlockSpec((SUBLANES, D_MODEL), lambda l, j: (0, 0)),
            pl.BlockSpec((None, D_MODEL, MOD_TN), lambda l, j: (l, 0, j)),
            pl.BlockSpec((None, 1, MOD_TN), lambda l, j: (l, 0, j)),
        ],
        out_specs=pl.BlockSpec((None, SUBLANES, MOD_TN), lambda l, j: (l, 0, j)),
        out_shape=jax.ShapeDtypeStruct((DEPTH, SUBLANES, n), F32),
        compiler_params=_cparams("parallel", "arbitrary"),
        name="modulation",
    )(cond8, w_mod, b_mod.reshape(DEPTH, 1, n))


DENSE_TM = 1024
NORM_ROWS = 256
PROJ_DTYPE = BF16


def _inproj_body(x_ref, g_ref, sc_ref, sh_ref, w_ref, o_ref, hb_ref, wg_ref=None, og_ref=None, cast_refs=None):
    j = pl.program_id(1)
    tm = x_ref.shape[0]
    if cast_refs is not None:
        cast_refs[1][...] = cast_refs[0][...].astype(BF16)

    @pl.when(j == 0)
    def _():
        for c in range(tm // NORM_ROWS):
            r = slice(c * NORM_ROWS, (c + 1) * NORM_ROWS)
            h = _norm_mod(x_ref[r, :], g_ref[...], sc_ref[...], sh_ref[...]).astype(BF16)
            hb_ref[r, :] = h
            o_ref[r, :] = _dot(h, w_ref[...]).astype(o_ref.dtype)
            if og_ref is not None:
                og_ref[r, :] = _dot(h, wg_ref[...])

    @pl.when(j > 0)
    def _():
        o_ref[...] = _dot(hb_ref[...], w_ref[...]).astype(o_ref.dtype)


def _inproj_kernel(x_ref, g_ref, sc_ref, sh_ref, w_ref, o_ref, hb_ref):
    _inproj_body(x_ref, g_ref, sc_ref, sh_ref, w_ref, o_ref, hb_ref)


def _inproj_gates_kernel(x_ref, g_ref, sc_ref, sh_ref, w_ref, wg_ref, o_ref, og_ref, hb_ref):
    _inproj_body(x_ref, g_ref, sc_ref, sh_ref, w_ref, o_ref, hb_ref, wg_ref, og_ref)


def _inproj_gates_cast_kernel(x_ref, g_ref, sc_ref, sh_ref, w_ref, wg_ref, wc_ref, o_ref, og_ref, wo_ref, hb_ref):
    _inproj_body(x_ref, g_ref, sc_ref, sh_ref, w_ref, o_ref, hb_ref, wg_ref, og_ref, (wc_ref, wo_ref))


def _mod_spec(idx, tm, rows_per_mod):
    return pl.BlockSpec((None, None, 1, D_MODEL), lambda i, n: ((i * tm) // rows_per_mod, idx, 0, 0))


def _inproj(x, g, mod, scale_idx, shift_idx, w, n, gates_block=None, tn=1024, cast=None):
    t = x.shape[0]
    tm = DENSE_TM
    rows_per_mod = t // mod.shape[0]
    in_specs = [
        pl.BlockSpec((tm, D_MODEL), lambda i, j: (i, 0)),
        pl.BlockSpec((1, D_MODEL), lambda i, j: (0, 0)),
        _mod_spec(scale_idx, tm, rows_per_mod),
        _mod_spec(shift_idx, tm, rows_per_mod),
        pl.BlockSpec((D_MODEL, tn), lambda i, j: (0, j)),
    ]
    out_specs = pl.BlockSpec((tm, tn), lambda i, j: (i, j))
    out_shape = jax.ShapeDtypeStruct((t, n), PROJ_DTYPE)
    args = [x, g.reshape(1, D_MODEL), mod, mod, w]
    kern = _inproj_kernel
    if gates_block is not None:
        in_specs.append(pl.BlockSpec((D_MODEL, LANES), lambda i, j: (0, gates_block)))
        out_specs = [out_specs, pl.BlockSpec((tm, LANES), lambda i, j: (i, 0))]
        out_shape = [out_shape, jax.ShapeDtypeStruct((t, LANES), F32)]
        args.append(w)
        kern = _inproj_gates_kernel
    if cast is not None:
        assert gates_block is not None
        nn = n // tn
        c_in, c_out, c_shape = _cast_job(*cast, lambda i, j: i * nn + j, (t // tm) * nn)
        in_specs.append(c_in)
        args.append(cast[0])
        out_specs.append(c_out)
        out_shape.append(c_shape)
        kern = _inproj_gates_cast_kernel
    return pl.pallas_call(
        kern,
        grid=(t // tm, n // tn),
        in_specs=in_specs,
        out_specs=out_specs,
        out_shape=out_shape,
        scratch_shapes=[pltpu.VMEM((tm, D_MODEL), BF16)],
        compiler_params=_cparams("arbitrary", "arbitrary"),
        name="inproj",
    )(*args)


OUT_TM = 512


def _outproj_kernel(a_ref, b_ref, wa_ref, wb_ref, x_ref, gate_ref, o_ref):
    acc = _dot(a_ref[...], wa_ref[...]) + _dot(b_ref[...], wb_ref[...])
    o_ref[...] = x_ref[...] + gate_ref[...] * acc


def _outproj_cast_kernel(a_ref, b_ref, wa_ref, wb_ref, x_ref, gate_ref, wc_ref, o_ref, wo_ref):
    wo_ref[...] = wc_ref[...].astype(BF16)
    _outproj_kernel(a_ref, b_ref, wa_ref, wb_ref, x_ref, gate_ref, o_ref)


def _outproj(mix_a, mix_b, w, layer, x, mod, gate_idx, cast=None):
    t = x.shape[0]
    tm = OUT_TM
    half = mix_a.shape[1]
    rows_per_mod = t // mod.shape[0]
    extra_in, extra_args, out_specs, out_shape = [], [], pl.BlockSpec((tm, D_MODEL), lambda i: (i, 0)), \
        jax.ShapeDtypeStruct((t, D_MODEL), F32)
    if cast is not None:
        c_in, c_out, c_shape = _cast_job(*cast, lambda i: i, t // tm)
        extra_in, extra_args = [c_in], [cast[0]]
        out_specs, out_shape = [out_specs, c_out], [out_shape, c_shape]
    return pl.pallas_call(
        _outproj_kernel if cast is None else _outproj_cast_kernel,
        grid=(t // tm,),
        in_specs=[
            pl.BlockSpec((tm, half), lambda i: (i, 0)),
            pl.BlockSpec((tm, half), lambda i: (i, 0)),
            pl.BlockSpec((None, half, D_MODEL), lambda i: (layer, 0, 0)),
            pl.BlockSpec((None, half, D_MODEL), lambda i: (layer, 1, 0)),
            pl.BlockSpec((tm, D_MODEL), lambda i: (i, 0)),
            pl.BlockSpec((None, None, 1, D_MODEL), lambda i: ((i * tm) // rows_per_mod, gate_idx, 0, 0)),
        ] + extra_in,
        out_specs=out_specs,
        out_shape=out_shape,
        compiler_params=_cparams("arbitrary"),
        name="outproj",
    )(mix_a, mix_b, w, w, x, mod, *extra_args)


FFN_TF = 512
FFN_ROWS = 512


def _ffn_kernel(x_ref, g_ref, sc_ref, sh_ref, gate_ref, wg_ref, wv_ref, cwg_ref, cwv_ref, cbg_ref, cbv_ref,
                wd_ref, o_ref, hb_ref, *, seq_len):
    f = pl.program_id(1)
    tm = x_ref.shape[0]

    rp = max(seq_len, FFN_ROWS)
    pos = lax.broadcasted_iota(jnp.int32, (rp, 1), 0) & (seq_len - 1)
    first = pos == 0
    last = pos == seq_len - 1

    def conv(a, cw_ref, cb_ref):
        prev = jnp.where(first, 0.0, pltpu.roll(a, 1, 0))
        nxt = jnp.where(last, 0.0, pltpu.roll(a, rp - 1, 0))
        return prev * cw_ref[0:1, :] + a * cw_ref[1:2, :] + nxt * cw_ref[2:3, :] + cb_ref[...]

    parts = [slice(r, r + rp) for r in range(0, tm, rp)]

    def hidden_block(first):
        acts = []
        for rows in parts:
            if first:
                for c in range(rows.start, rows.stop, NORM_ROWS):
                    r = slice(c, c + NORM_ROWS)
                    h = _norm_mod(x_ref[r, :], g_ref[...], sc_ref[...], sh_ref[...])
                    hb_ref[r, :] = h.astype(BF16)
            hb = hb_ref[rows, :]
            cg = conv(_dot(hb, wg_ref[...]), cwg_ref, cbg_ref)
            cv = conv(_dot(hb, wv_ref[...]), cwv_ref, cbv_ref)
            acts.append((cg * _sigmoid(cg) * cv).astype(BF16))
        for rows, act in zip(parts, acts):
            if first:
                o_ref[rows, :] = _dot(act, wd_ref[...])
            else:
                o_ref[rows, :] += _dot(act, wd_ref[...])

    @pl.when(f == 0)
    def _():
        hidden_block(True)

    @pl.when(f > 0)
    def _():
        hidden_block(False)

    @pl.when(f == pl.num_programs(1) - 1)
    def _():
        o_ref[...] = x_ref[...] + gate_ref[...] * o_ref[...]


def _ffn(x, g, mod, w_up, conv_w, conv_b, w_down, seq_len):
    t = x.shape[0]
    tm, tf = DENSE_TM, FFN_TF
    nf = D_FF // tf
    rows_per_mod = t // mod.shape[0]
    conv_b = conv_b.reshape(1, 2 * D_FF)
    assert seq_len & (seq_len - 1) == 0 and tm % seq_len == 0
    return pl.pallas_call(
        functools.partial(_ffn_kernel, seq_len=seq_len),
        grid=(t // tm, nf),
        in_specs=[
            pl.BlockSpec((tm, D_MODEL), lambda i, f: (i, 0), pipeline_mode=pl.Buffered(1)),
            pl.BlockSpec((1, D_MODEL), lambda i, f: (0, 0)),
            _mod_spec(4, tm, rows_per_mod),
            _mod_spec(3, tm, rows_per_mod),
            _mod_spec(5, tm, rows_per_mod),
            pl.BlockSpec((D_MODEL, tf), lambda i, f: (0, f)),
            pl.BlockSpec((D_MODEL, tf), lambda i, f: (0, nf + f)),
            pl.BlockSpec((3, tf), lambda i, f: (0, f)),
            pl.BlockSpec((3, tf), lambda i, f: (0, nf + f)),
            pl.BlockSpec((1, tf), lambda i, f: (0, f)),
            pl.BlockSpec((1, tf), lambda i, f: (0, nf + f)),
            pl.BlockSpec((tf, D_MODEL), lambda i, f: (f, 0)),
        ],
        out_specs=pl.BlockSpec((tm, D_MODEL), lambda i, f: (i, 0)),
        out_shape=jax.ShapeDtypeStruct((t, D_MODEL), F32),
        scratch_shapes=[pltpu.VMEM((tm, D_MODEL), BF16)],
        compiler_params=_cparams("parallel", "arbitrary"),
        name="convffn",
    )(x, g.reshape(1, D_MODEL), mod, mod, mod, w_up, w_up, conv_w, conv_w, conv_b, conv_b, w_down)


def _attn_ctx_kernel(q_ref, k_ref, v_ref, gq_ref, gk_ref, wc_ref, o_ref, ko_ref, vo_ref, wo_ref):
    wo_ref[...] = wc_ref[...].astype(BF16)
    scale = DH ** -0.5
    for h in range(H_A):
        sl = slice(h * DH, (h + 1) * DH)
        q = _rms_lanes(q_ref[:, sl].astype(F32), gq_ref[...])
        k = _rms_lanes(k_ref[:, sl].astype(F32), gk_ref[...])
        v = v_ref[:, sl].astype(F32)
        ko_ref[h] = k
        vo_ref[h] = v
        s = _dot_nt(q.astype(BF16), k.astype(BF16)) * scale
        p = jnp.exp(s - jnp.max(s, axis=-1, keepdims=True))
        p = p / jnp.sum(p, axis=-1, keepdims=True)
        o_ref[:, sl] = _dot(p.astype(BF16), v.astype(BF16)).astype(o_ref.dtype)


def _attn_ctx(proj, gq, gk, batch, seq, cast):
    t = proj.shape[0]
    c_in, c_out, c_shape = _cast_job(*cast, lambda b: b, batch)
    kv_shape = jax.ShapeDtypeStruct((batch, 1, H_A, seq, DH), F32)
    kv_spec = pl.BlockSpec((None, None, H_A, seq, DH), lambda b: (b, 0, 0, 0, 0))
    return pl.pallas_call(
        _attn_ctx_kernel,
        grid=(batch,),
        in_specs=[
            pl.BlockSpec((seq, W_A), lambda b: (b, 0)),
            pl.BlockSpec((seq, W_A), lambda b: (b, 1)),
            pl.BlockSpec((seq, W_A), lambda b: (b, 2)),
            pl.BlockSpec((1, DH), lambda b: (0, 0)),
            pl.BlockSpec((1, DH), lambda b: (0, 0)),
            c_in,
        ],
        out_specs=[pl.BlockSpec((seq, W_A), lambda b: (b, 0)), kv_spec, kv_spec, c_out],
        out_shape=[jax.ShapeDtypeStruct((t, W_A), BF16), kv_shape, kv_shape, c_shape],
        compiler_params=_cparams("arbitrary"),
        name="attn_ctx",
    )(proj, proj, proj, gq.reshape(1, DH), gk.reshape(1, DH), cast[0])


NA_ROWS = 16
NA_PAIRS = 2 * WIN_H - 2
RPB_H = 2 * WIN_H - 1
RPB_W = 2 * WIN_W - 1


def _natten_kernel(rpb_ref, q_ref, k_ref, v_ref, kc_ref, vc_ref, gq_ref, gk_ref, o_ref,
                   qn_s, kn_s, v_s, kc_s, vc_s, bias_s):
    h = pl.program_id(1)
    scale = DH ** -0.5
    qn_s[...] = _rms_lanes(q_ref[...].astype(F32), gq_ref[...]).astype(BF16)
    kn_s[...] = _rms_lanes(k_ref[...].astype(F32), gk_ref[...]).astype(BF16)
    v_s[...] = v_ref[...].astype(BF16)
    kc_s[...] = kc_ref[...].astype(BF16)
    vc_s[...] = vc_ref[...].astype(BF16)

    lane = lax.broadcasted_iota(jnp.int32, (GRID_W, LANES), 1)
    qc = lax.broadcasted_iota(jnp.int32, (GRID_W, LANES), 0)
    kcol = lane % GRID_W
    upper = lane >= GRID_W
    dcol = kcol - qc + (WIN_W - 1)
    col0 = jnp.clip(qc - WIN_W // 2, 0, GRID_W - WIN_W)
    col_ok = (kcol >= col0) & (kcol < col0 + WIN_W)
    base = h * (RPB_H * RPB_W)

    def build(p, carry):
        acc = jnp.zeros((GRID_W, LANES), F32)
        for d in range(RPB_W):
            lo = rpb_ref[base + p * RPB_W + d]
            hi = rpb_ref[base + (p + 1) * RPB_W + d]
            acc = jnp.where(dcol == d, jnp.where(upper, hi, lo), acc)
        bias_s[p] = jnp.where(col_ok, acc, NEG_INF)
        return carry

    lax.fori_loop(0, NA_PAIRS, build, 0)

    kh = WIN_H
    n_loc = kh * GRID_W
    row0 = [min(max(r - kh // 2, 0), NA_ROWS - kh) for r in range(NA_ROWS)]
    q = [qn_s[r * GRID_W:(r + 1) * GRID_W, :] for r in range(NA_ROWS)]
    kc = kc_s[...]
    s_loc, s_ctx = [], []
    for r in range(NA_ROWS):
        dr0 = row0[r] - r + WIN_H - 1
        k_loc = kn_s[row0[r] * GRID_W:row0[r] * GRID_W + n_loc, :]
        bias = jnp.concatenate([bias_s[dr0 + 2 * i] for i in range(kh // 2)], axis=-1)
        s_loc.append(_dot_nt(q[r], k_loc) * scale + bias)
        s_ctx.append(_dot_nt(q[r], kc) * scale)
    p_loc, p_ctx = [], []
    for r in range(NA_ROWS):
        m = jnp.maximum(jnp.max(s_loc[r], axis=-1, keepdims=True), jnp.max(s_ctx[r], axis=-1, keepdims=True))
        e_loc = jnp.exp(s_loc[r] - m)
        e_ctx = jnp.exp(s_ctx[r] - m)
        l = jnp.sum(e_loc, axis=-1, keepdims=True) + jnp.sum(e_ctx, axis=-1, keepdims=True)
        p_loc.append((e_loc / l).astype(BF16))
        p_ctx.append((e_ctx / l).astype(BF16))
    vc = vc_s[...]
    for r in range(NA_ROWS):
        v_loc = v_s[row0[r] * GRID_W:row0[r] * GRID_W + n_loc, :]
        out = _dot(p_loc[r], v_loc) + _dot(p_ctx[r], vc)
        o_ref[r * GRID_W:(r + 1) * GRID_W, :] = out.astype(o_ref.dtype)


def _natten(proj, cache_k, cache_v, gq, gk, rpb, batch, seq):
    t = proj.shape[0]
    past = cache_k.shape[3]
    ctx_spec = pl.BlockSpec((None, None, None, past, DH), lambda b, h: (b, 0, h, 0, 0))
    return pl.pallas_call(
        _natten_kernel,
        grid=(batch, H_A),
        in_specs=[
            pl.BlockSpec(memory_space=pltpu.SMEM),
            pl.BlockSpec((seq, DH), lambda b, h: (b, h)),
            pl.BlockSpec((seq, DH), lambda b, h: (b, H_A + h)),
            pl.BlockSpec((seq, DH), lambda b, h: (b, 2 * H_A + h)),
            ctx_spec,
            ctx_spec,
            pl.BlockSpec((1, DH), lambda b, h: (0, 0)),
            pl.BlockSpec((1, DH), lambda b, h: (0, 0)),
        ],
        out_specs=pl.BlockSpec((seq, DH), lambda b, h: (b, h)),
        out_shape=jax.ShapeDtypeStruct((t, W_A), BF16),
        scratch_shapes=[
            pltpu.VMEM((seq, DH), BF16),
            pltpu.VMEM((seq, DH), BF16),
            pltpu.VMEM((seq, DH), BF16),
            pltpu.VMEM((past, DH), BF16),
            pltpu.VMEM((past, DH), BF16),
            pltpu.VMEM((NA_PAIRS, GRID_W, LANES), F32),
        ],
        compiler_params=_cparams("parallel", "arbitrary"),
        name="natten",
    )(rpb.reshape(-1), proj, proj, proj, cache_k, cache_v, gq.reshape(1, DH), gk.reshape(1, DH))


ML_EPILOGUE_ROWS = 256


def _split3(x):
    hi = x.astype(BF16)
    r1 = x - hi.astype(F32)
    mid = r1.astype(BF16)
    lo = (r1 - mid.astype(F32)).astype(BF16)
    return hi, mid, lo


def _mlstm_kernel(*refs, has_state, emit_state, n_cast, seq):
    it = iter(refs)
    q_ref, k_ref, v_ref, ob_ref, gt_ref, mlb_ref, mlg_ref = (next(it) for _ in range(7))
    if has_state:
        c0_ref, n0_ref, m0_ref = (next(it) for _ in range(3))
    wc_refs = [next(it) for _ in range(n_cast)]
    o_ref = next(it)
    if emit_state:
        co_ref, no_ref, mo_ref = (next(it) for _ in range(3))
    for wc_ref in wc_refs:
        next(it)[...] = wc_ref[...].astype(BF16)
    r_s, rt_s, bt_s, hf_s, hb_s, ct_s, n_s, m_s = (next(it) for _ in range(8))

    L = ML_CHUNK
    nc = seq // L
    kscale = DH ** -0.5

    ri = lax.broadcasted_iota(jnp.int32, (L, L), 0)
    ci = lax.broadcasted_iota(jnp.int32, (L, L), 1)
    tri_pre = (ci <= ri).astype(BF16)
    tri_suf = (ci >= ri).astype(BF16)
    mask_t = (ri <= ci, ri >= ci)
    fwd_lane = lax.broadcasted_iota(jnp.int32, (L, LANES), 1) < 2 * H_B

    for c in range(nc):
        rows = slice(c * L, (c + 1) * L)
        g = gt_ref[rows, :] + mlb_ref[...]
        gf = jnp.minimum(g, 0.0) - jnp.log1p(jnp.exp(-jnp.abs(g)))
        hi, mid, lo = _split3(gf)
        pre = _dot(tri_pre, hi) + _dot(tri_pre, mid) + _dot(tri_pre, lo)
        suf = _dot(tri_suf, hi) + _dot(tri_suf, mid) + _dot(tri_suf, lo)
        b = pltpu.roll(jnp.where(fwd_lane, pre, suf), LANES - H_B, 1)
        r = g - b
        r_s[rows, :] = r
        rt_s[c] = r.T
        bt_s[c] = b.T

    for d in range(2):
        for h in range(H_B):
            ct_s[d, h] = c0_ref[d, h].T if has_state else jnp.zeros((DH, DH), F32)
    if has_state:
        n_s[...] = n0_ref[...]
        m_s[...] = m0_ref[...]
    else:
        n_s[...] = jnp.zeros_like(n_s)
        m_s[...] = jnp.zeros_like(m_s)

    pairs = [(d, h) for d in range(2) for h in range(H_B)]

    def step(c, carry):
        rows, r_all, rt_all, bt_all = [], [], [], []
        for d in range(2):
            cc = c if d == 0 else nc - 1 - c
            rows.append(pl.ds(pl.multiple_of(cc * L, L), L))
            r_all.append(r_s[rows[d], :])
            rt_all.append(rt_s[cc])
            bt_all.append(bt_s[cc])

        qb, kb, v_t, rm, r_row, b_row, a_row, qk = {}, {}, {}, {}, {}, {}, {}, {}
        for p in pairs:
            d, h = p
            col = 2 * H_B * d + h
            sl = slice(h * DH, (h + 1) * DH)
            qb[p] = q_ref[rows[d], sl].astype(BF16)
            kb[p] = (k_ref[rows[d], sl].astype(F32) * kscale).astype(BF16)
            v_t[p] = v_ref[rows[d], sl].astype(F32).T
            r_row[p] = rt_all[d][col:col + 1, :]
            b_row[p] = bt_all[d][col:col + 1, :]
            rm[p] = jnp.where(mask_t[d], jnp.broadcast_to(r_all[d][:, col:col + 1], (L, L)), NEG_INF)
            a_row[p] = jnp.max(rm[p], axis=0, keepdims=True)
            qk[p] = _dot_nt(kb[p], qb[p])

        m_row = {(d, h): m_s[d, h:h + 1, :] for d, h in pairs}
        n_row = {(d, h): n_s[d, h:h + 1, :] for d, h in pairs}
        ct = {(d, h): ct_s[d, h] for d, h in pairs}
        big_m = {p: jnp.maximum(m_row[p], a_row[p]) for p in pairs}
        w_in = {p: jnp.exp(m_row[p] - big_m[p]) for p in pairs}
        cq = {p: _dot_nt(ct[p].astype(BF16), qb[p]) for p in pairs}
        qn = {p: _dot_nt(jnp.broadcast_to(n_row[p], (SUBLANES, DH)).astype(BF16), qb[p])[0:1, :] for p in pairs}
        w_t = {p: jnp.exp(rm[p] - big_m[p]) * qk[p] for p in pairs}
        pv = {p: _dot(v_t[p].astype(BF16), w_t[p].astype(BF16)) for p in pairs}
        for p in pairs:
            d, h = p
            den = w_in[p] * qn[p] + jnp.sum(w_t[p], axis=0, keepdims=True)
            h_t = (w_in[p] * cq[p] + pv[p]) / jnp.maximum(jnp.abs(den), jnp.exp(-(b_row[p] + big_m[p])))
            (hf_s if d == 0 else hb_s)[rows[d], slice(h * DH, (h + 1) * DH)] = h_t.T

        m_last, ws = {}, {}
        for p in pairs:
            m_last[p] = jnp.maximum(m_row[p], jnp.max(r_row[p], axis=-1, keepdims=True))
            ws[p] = jnp.exp(r_row[p] - m_last[p])
        kv = {p: _dot((v_t[p] * ws[p]).astype(BF16), kb[p]) for p in pairs}
        nk = {p: _dot(jnp.broadcast_to(ws[p], (SUBLANES, L)).astype(BF16), kb[p])[0:1, :] for p in pairs}
        for p in pairs:
            d, h = p
            a_prev = jnp.exp(m_row[p] - m_last[p])
            bl = b_row[p][:, L - 1:L] if d == 0 else b_row[p][:, 0:1]
            ct_s[d, h] = a_prev * ct[p] + kv[p]
            n_s[d, h:h + 1, :] = a_prev * n_row[p] + nk[p]
            m_s[d, h:h + 1, :] = bl + m_last[p]
        return carry

    lax.fori_loop(0, nc, step, 0)

    if emit_state:
        for d in range(2):
            for h in range(H_B):
                co_ref[d, h] = ct_s[d, h].T
        no_ref[...] = n_s[...]
        mo_ref[...] = m_s[...]

    for r in range(seq // ML_EPILOGUE_ROWS):
        rr = slice(r * ML_EPILOGUE_ROWS, (r + 1) * ML_EPILOGUE_ROWS)
        for h in range(H_B):
            sl = slice(h * DH, (h + 1) * DH)
            hm = hf_s[rr, sl] + hb_s[rr, sl]
            ob = ob_ref[rr, sl].astype(F32)
            o_ref[rr, sl] = (_sigmoid(ob) * _rms_lanes(hm, mlg_ref[:, sl])).astype(o_ref.dtype)


def _mlstm(proj, gates, ml_b, ml_g, batch, seq, state=None, emit_state=False, casts=()):
    t = proj.shape[0]
    col0 = 3 * W_A // W_B
    big = lambda j: pl.BlockSpec((seq, W_B), lambda b: (b, col0 + j))
    in_specs = [big(0), big(1), big(2), big(3),
                pl.BlockSpec((seq, LANES), lambda b: (b, 0)),
                pl.BlockSpec((1, LANES), lambda b: (0, 0)),
                pl.BlockSpec((1, W_B), lambda b: (0, 0))]
    args = [proj, proj, proj, proj, gates, ml_b, ml_g.reshape(1, W_B)]
    c_spec = pl.BlockSpec((None, None, 2, H_B, DH, DH), lambda b: (b, 0, 0, 0, 0, 0))
    n_spec = pl.BlockSpec((None, None, 2, H_B, DH), lambda b: (b, 0, 0, 0, 0))
    if state is not None:
        in_specs += [c_spec, n_spec, n_spec]
        args += list(state)
    out_specs = [pl.BlockSpec((seq, W_B), lambda b: (b, 0))]
    out_shape = [jax.ShapeDtypeStruct((t, W_B), BF16)]
    if emit_state:
        out_specs += [c_spec, n_spec, n_spec]
        out_shape += [jax.ShapeDtypeStruct((batch, 1, 2, H_B, DH, DH), F32),
                      jax.ShapeDtypeStruct((batch, 1, 2, H_B, DH), F32),
                      jax.ShapeDtypeStruct((batch, 1, 2, H_B, LANES), F32)]
    start = 0
    for cast in casts:
        c_in, c_out, c_shape = _cast_job(*cast, lambda b: b, batch, start)
        start += cast[0].shape[cast[2]] // CAST_BLK
        in_specs.append(c_in)
        args.append(cast[0])
        out_specs.append(c_out)
        out_shape.append(c_shape)
    nc = seq // ML_CHUNK
    return pl.pallas_call(
        functools.partial(_mlstm_kernel, has_state=state is not None, emit_state=emit_state,
                          n_cast=len(casts), seq=seq),
        grid=(batch,),
        in_specs=in_specs,
        out_specs=out_specs,
        out_shape=out_shape,
        scratch_shapes=[
            pltpu.VMEM((seq, LANES), F32),
            pltpu.VMEM((nc, ML_CHUNK, LANES), F32),
            pltpu.VMEM((nc, ML_CHUNK, LANES), F32),
            pltpu.VMEM((seq, W_B), F32),
            pltpu.VMEM((seq, W_B), F32),
            pltpu.VMEM((2, H_B, DH, DH), F32),
            pltpu.VMEM((2, H_B, DH), F32),
            pltpu.VMEM((2, H_B, LANES), F32),
        ],
        compiler_params=_cparams("arbitrary"),
        name="mlstm",
    )(*args)


DIFF_QB = 256


def _diff_kernel(*refs, heads, seq, past, rope, emit_ctx, has_cast, lam_init):
    it = iter(refs)
    q_ref, k_ref, v_ref, gq_ref, gk_ref, go_ref, lam_ref = (next(it) for _ in range(7))
    if past:
        kc_ref, vc_ref = next(it), next(it)
    if rope:
        cos_ref, s1_ref, s2_ref = next(it), next(it), next(it)
    if has_cast:
        wc_ref = next(it)
    o_ref = next(it)
    if emit_ctx:
        ko_ref, vo_ref = next(it), next(it)
    if has_cast:
        wo_ref = next(it)
        wo_ref[...] = wc_ref[...].astype(BF16)

    scale = DQK ** -0.5
    lv = lam_ref[...]
    lam = (jnp.exp(jnp.sum(lv[0:1, :] * lv[1:2, :], axis=-1, keepdims=True))
           - jnp.exp(jnp.sum(lv[2:3, :] * lv[3:4, :], axis=-1, keepdims=True)) + lam_init)
    low = lax.broadcasted_iota(jnp.int32, (seq, LANES), 1) < DQK
    hr = lax.broadcasted_iota(jnp.int32, (LANES, LANES), 0) < DQK
    hc = lax.broadcasted_iota(jnp.int32, (LANES, LANES), 1) < DQK
    halves = (hr == hc).astype(BF16)

    def apply_rope(x):
        return (x * cos_ref[...] + pltpu.roll(x, LANES - DQK // 4, 1) * s1_ref[...]
                + pltpu.roll(x, DQK // 4, 1) * s2_ref[...])

    qb_rows = min(DIFF_QB, seq)
    blocks = [slice(i * qb_rows, (i + 1) * qb_rows) for i in range(seq // qb_rows)]
    head_sl = [slice(hh * DH, (hh + 1) * DH) for hh in range(heads)]
    qn = [_rms_lane_groups(q_ref[:, sl].astype(F32), gq_ref[...], halves, DQK) for sl in head_sl]
    kn = [_rms_lane_groups(k_ref[:, sl].astype(F32), gk_ref[...], halves, DQK) for sl in head_sl]
    v = [v_ref[:, sl].astype(F32) for sl in head_sl]
    if emit_ctx:
        for hh in range(heads):
            ko_ref[hh] = kn[hh]
            vo_ref[hh] = v[hh]
    if rope:
        qn = [apply_rope(x) for x in qn]
        kn = [apply_rope(x) for x in kn]
    assert math.frexp(scale)[0] == 0.5
    q0 = [(jnp.where(low, x, 0.0) * scale).astype(BF16) for x in qn]
    q1 = [(jnp.where(low, 0.0, x) * scale).astype(BF16) for x in qn]
    ka, vt = [], []
    for hh in range(heads):
        k_parts = [kn[hh].astype(BF16)]
        vt_parts = [v[hh][c * LANES:(c + 1) * LANES, :].T.astype(BF16) for c in range(seq // LANES)]
        if past:
            k_parts.append(kc_ref[hh].astype(BF16))
            vt_parts += [vc_ref[hh, c * LANES:(c + 1) * LANES, :].T.astype(BF16) for c in range(past // LANES)]
        ka.append(jnp.concatenate(k_parts, axis=0))
        vt.append(jnp.concatenate(vt_parts, axis=1))

    work = [(hh, rows) for hh in range(heads) for rows in blocks]
    s0 = [_dot_nt(ka[hh], q0[hh][rows, :]) for hh, rows in work]
    s1 = [_dot_nt(ka[hh], q1[hh][rows, :]) for hh, rows in work]

    def softmax_t(s, weight):
        p = jnp.exp(s - jnp.max(s, axis=0, keepdims=True))
        return p * (weight / jnp.sum(p, axis=0, keepdims=True))

    a_t = [(softmax_t(a, 1.0) - softmax_t(b, lam)).astype(BF16) for a, b in zip(s0, s1)]
    out_t = [_dot(vt[hh], a) for (hh, _), a in zip(work, a_t)]
    for (hh, rows), o in zip(work, out_t):
        o = o * lax.rsqrt(jnp.mean(o * o, axis=0, keepdims=True) + EPS)
        g_row = go_ref[:, head_sl[hh]] * (1.0 - lam_init)
        for c in range(qb_rows // LANES):
            r0 = rows.start + c * LANES
            o_ref[r0:r0 + LANES, head_sl[hh]] = (o[:, c * LANES:(c + 1) * LANES].T * g_row).astype(o_ref.dtype)


def _diff_attn(proj, gq, gk, g_out, lam_vec, lam_init, batch, seq, heads, cache=None, rope=None,
               emit_ctx=False, cast=None):
    t = proj.shape[0]
    nh = H_C // heads
    w = heads * DH
    kblk = W_CQK // w
    past = 0 if cache is None else cache[0].shape[3]
    in_specs = [
        pl.BlockSpec((seq, w), lambda b, j: (b, j)),
        pl.BlockSpec((seq, w), lambda b, j: (b, kblk + j)),
        pl.BlockSpec((seq, w), lambda b, j: (b, 2 * kblk + j)),
        pl.BlockSpec((1, DH), lambda b, j: (0, 0)),
        pl.BlockSpec((1, DH), lambda b, j: (0, 0)),
        pl.BlockSpec((1, w), lambda b, j: (0, j)),
        pl.BlockSpec((4, DQK), lambda b, j: (0, 0)),
    ]
    args = [proj, proj, proj, jnp.tile(gq, 2).reshape(1, DH), jnp.tile(gk, 2).reshape(1, DH),
            g_out.reshape(1, W_CV), lam_vec]
    if cache is not None:
        ctx_spec = pl.BlockSpec((None, None, heads, past, DH), lambda b, j: (b, 0, j, 0, 0))
        in_specs += [ctx_spec, ctx_spec]
        args += [cache[0].reshape(cache[0].shape[:4] + (DH,)), cache[1]]
    if rope is not None:
        tab = pl.BlockSpec((seq, DH), lambda b, j: (0, 0))
        in_specs += [tab, tab, tab]
        args += list(rope)
    out_specs = [pl.BlockSpec((seq, w), lambda b, j: (b, j))]
    out_shape = [jax.ShapeDtypeStruct((t, W_CV), BF16)]
    if emit_ctx:
        kv_spec = pl.BlockSpec((None, None, heads, seq, DH), lambda b, j: (b, 0, j, 0, 0))
        kv_shape = jax.ShapeDtypeStruct((batch, 1, H_C, seq, DH), F32)
        out_specs += [kv_spec, kv_spec]
        out_shape += [kv_shape, kv_shape]
    if cast is not None:
        c_in, c_out, c_shape = _cast_job(*cast, lambda b, j: b * nh + j, batch * nh)
        in_specs.append(c_in)
        args.append(cast[0])
        out_specs.append(c_out)
        out_shape.append(c_shape)
    return pl.pallas_call(
        functools.partial(_diff_kernel, heads=heads, seq=seq, past=past, rope=rope is not None,
                          emit_ctx=emit_ctx, has_cast=cast is not None, lam_init=lam_init),
        grid=(batch, nh),
        in_specs=in_specs,
        out_specs=out_specs,
        out_shape=out_shape,
        compiler_params=_cparams("arbitrary", "arbitrary"),
        name="diff_attn",
    )(*args)


def _rope_tables(seq):
    tpos = jnp.arange(seq)
    pos = jnp.stack([tpos // GRID_W, tpos % GRID_W], axis=-1).astype(F32)
    half = DQK // 2
    inv = ROPE_THETA ** (-jnp.arange(0, half, 2, dtype=F32) / half)
    ang = pos[:, :, None] * inv
    cos = jnp.cos(ang)
    sin = jnp.sin(ang)
    zero = jnp.zeros_like(sin)
    lay = lambda first, second: jnp.tile(jnp.concatenate([first, second], axis=-1).reshape(seq, DQK), (1, 2))
    return lay(cos, cos), lay(-sin, zero), lay(zero, sin)


SGU_TM = 512


def _sgu_kernel(*refs, has_cast):
    if has_cast:
        u_ref, vd_ref, sgg_ref, sgw_ref, sgb_ref, wc_ref, o_ref, wo_ref = refs
        wo_ref[...] = wc_ref[...].astype(BF16)
    else:
        u_ref, vd_ref, sgg_ref, sgw_ref, sgb_ref, o_ref = refs
    tm = u_ref.shape[0]
    for c in range(tm // SG_CHUNK):
        rows = slice(c * SG_CHUNK, (c + 1) * SG_CHUNK)
        vn = _rms_lanes(_gelu_tanh(vd_ref[rows, :].astype(F32)), sgg_ref[...]).astype(BF16)
        for g in range(G_D):
            sl = slice(g * DH, (g + 1) * DH)
            gate = _dot(sgw_ref[g].astype(BF16), vn[:, sl]) + sgb_ref[:, g:g + 1]
            o_ref[rows, sl] = (_gelu_tanh(u_ref[rows, sl].astype(F32)) * gate).astype(o_ref.dtype)


def _sgu(proj, sg_g, sg_w, sg_b, cast=None):
    t = proj.shape[0]
    tm = SGU_TM
    ublk = (2 * W_CQK + W_CV) // W_D
    in_specs = [
        pl.BlockSpec((tm, W_D), lambda i: (i, ublk)),
        pl.BlockSpec((tm, W_D), lambda i: (i, ublk + 1)),
        pl.BlockSpec((1, W_D), lambda i: (0, 0)),
        pl.BlockSpec((G_D, SG_CHUNK, SG_CHUNK), lambda i: (0, 0, 0)),
        pl.BlockSpec((SG_CHUNK, G_D), lambda i: (0, 0)),
    ]
    args = [proj, proj, sg_g.reshape(1, W_D), sg_w, sg_b.T]
    out_specs = [pl.BlockSpec((tm, W_D), lambda i: (i, 0))]
    out_shape = [jax.ShapeDtypeStruct((t, W_D), BF16)]
    if cast is not None:
        c_in, c_out, c_shape = _cast_job(*cast, lambda i: i, t // tm)
        in_specs.append(c_in)
        args.append(cast[0])
        out_specs.append(c_out)
        out_shape.append(c_shape)
    return pl.pallas_call(
        functools.partial(_sgu_kernel, has_cast=cast is not None),
        grid=(t // tm,),
        in_specs=in_specs,
        out_specs=out_specs,
        out_shape=out_shape,
        compiler_params=_cparams("arbitrary"),
        name="sgu",
    )(*args)


def _lambda_init(layer):
    return 0.8 - 0.6 * math.exp(-0.3 * layer)


def _trunk(x, mods, p, batch, seq, caches, ffn_w):
    ctx_out = {}
    for l in range(DEPTH):
        mod = mods[l]
        if l % 2 == 0:
            if caches is None:
                proj, gates, w_up1 = _inproj(x, p["g_mix"][l], mod, 1, 0, p["w_even"], N_EVEN_MAIN,
                                             N_EVEN_MAIN // LANES, cast=(p["ffn_w_up"], 1, 2))
                mix_a, ctx_out["na_k"], ctx_out["na_v"], w_down0 = _attn_ctx(
                    proj, p["na_gq"], p["na_gk"], batch, seq, cast=(p["ffn_w_down"], 0, 1))
                mix_b, c1, n1, m1, w_up0, ffn_w["w_odd"] = _mlstm(
                    proj, gates, p["ml_b"], p["ml_g"], batch, seq, emit_state=True,
                    casts=((p["ffn_w_up"], 0, 2), (p["w_in_odd"], 0, 2)))
                ffn_w[0] = [w_up0, w_down0]
                ffn_w[1] = [w_up1, None]
                ctx_out.update(mlstm_C=c1, mlstm_n=n1, mlstm_m=m1[..., 0])
            else:
                proj, gates = _inproj(x, p["g_mix"][l], mod, 1, 0, p["w_even"], N_EVEN_MAIN, N_EVEN_MAIN // LANES)
                mix_a = _natten(proj, caches["na_k"], caches["na_v"], p["na_gq"], p["na_gk"], p["na_rpb"], batch, seq)
                (mix_b,) = _mlstm(proj, gates, p["ml_b"], p["ml_g"], batch, seq,
                                  state=(caches["C"], caches["n"], caches["m"]))
        else:
            proj = _inproj(x, p["g_mix"][l], mod, 1, 0, ffn_w["w_odd"], ffn_w["w_odd"].shape[1])
            lam_init = _lambda_init(l)
            if caches is None:
                mix_a, dk, dv = _diff_attn(proj, p["diff_gq"], p["diff_gk"], p["diff_g_out"], p["diff_lam"],
                                           lam_init, batch, seq, heads=H_C, emit_ctx=True)
                ctx_out["diff_k"] = dk.reshape(batch, 1, H_C, seq, 2, DQK)
                ctx_out["diff_v"] = dv
            else:
                (mix_a,) = _diff_attn(proj, p["diff_gq"], p["diff_gk"], p["diff_g_out"], p["diff_lam"],
                                      lam_init, batch, seq, heads=1,
                                      cache=(caches["diff_k"], caches["diff_v"]), rope=_rope_tables(seq))
            (mix_b,) = _sgu(proj, p["sg_g"], p["sg_w"], p["sg_b"])
        if caches is None and l == 0:
            x, ffn_w[1][1] = _outproj(mix_a, mix_b, p["w_out"], l, x, mod, 2, cast=(p["ffn_w_down"], 1, 1))
        else:
            x = _outproj(mix_a, mix_b, p["w_out"], l, x, mod, 2)
        x = _ffn(x, p["g_ffn"][l], mod, ffn_w[l][0], p["conv_w"][l], p["conv_b"][l], ffn_w[l][1], seq)
    return x, ctx_out


def kernel(x_prompt, x_sample, cache_na_k, cache_na_v, state_mlstm_C, state_mlstm_n, state_mlstm_m,
           cache_diff_k, cache_diff_v, c, c_ctx, w_mod, b_mod, g_mix, g_ffn, w_out, w_in_even,
           na_gq, na_gk, na_rpb, ml_b_gates, ml_g_out, w_in_odd, diff_gq, diff_gk, diff_lam,
           diff_g_out, sg_g_v, sg_w, sg_b, ffn_w_up, ffn_conv_w, ffn_conv_b, ffn_w_down):
    batch, seq, _ = x_prompt.shape
    dbatch, dseq, _ = x_sample.shape

    cond8 = jnp.zeros((SUBLANES, D_MODEL), F32).at[0].set(c_ctx).at[1:1 + dbatch].set(c)
    mod = _modulation(cond8, w_mod, b_mod)
    mods_ctx = [mod[l, 0:1].reshape(1, 6, 1, D_MODEL) for l in range(DEPTH)]
    mods_lat = [mod[l, 1:1 + dbatch].reshape(dbatch, 6, 1, D_MODEL) for l in range(DEPTH)]

    p = dict(
        g_mix=g_mix, g_ffn=g_ffn,
        w_even=jnp.pad(w_in_even[0], ((0, 0), (0, LANES - N_GATES))).astype(BF16),
        w_in_odd=w_in_odd,
        w_out=w_out.astype(BF16), ffn_w_up=ffn_w_up, ffn_w_down=ffn_w_down,
        conv_w=ffn_conv_w, conv_b=ffn_conv_b,
        na_gq=na_gq[0], na_gk=na_gk[0], na_rpb=na_rpb[0],
        ml_b=jnp.pad(ml_b_gates[0], (0, LANES - N_GATES)).reshape(1, LANES), ml_g=ml_g_out[0],
        diff_gq=diff_gq[0], diff_gk=diff_gk[0], diff_lam=diff_lam[0], diff_g_out=diff_g_out[0],
        sg_g=sg_g_v[0], sg_w=sg_w[0], sg_b=sg_b[0],
    )

    ffn_w = {}
    y_prompt, ctx = _trunk(x_prompt.reshape(batch * seq, D_MODEL), mods_ctx, p, batch, seq, None, ffn_w)
    caches = dict(
        na_k=cache_na_k, na_v=cache_na_v, C=state_mlstm_C, n=state_mlstm_n,
        m=jnp.broadcast_to(state_mlstm_m[..., None], state_mlstm_m.shape + (LANES,)),
        diff_k=cache_diff_k, diff_v=cache_diff_v,
    )
    y_sample, _ = _trunk(x_sample.reshape(dbatch * dseq, D_MODEL), mods_lat, p, dbatch, dseq, caches, ffn_w)
    return (y_prompt.reshape(batch, seq, D_MODEL), y_sample.reshape(dbatch, dseq, D_MODEL),
            ctx["na_k"], ctx["na_v"], ctx["mlstm_C"], ctx["mlstm_n"], ctx["mlstm_m"],
            ctx["diff_k"], ctx["diff_v"])
```

```python
import functools
import math

import numpy as np
import jax
import jax.numpy as jnp
from jax import lax
from jax.experimental import pallas as pl
from jax.experimental.pallas import tpu as pltpu

F32 = jnp.float32
BF16 = jnp.bfloat16

D_MODEL = 2048
DEPTH = 2
GRID_W = 64
DH = 128
H_A = 8
W_A = H_A * DH
WIN_H = 8
WIN_W = 16
H_B = 8
W_B = H_B * DH
ML_CHUNK = 128
H_C = 8
DQK = 64
W_CQK = H_C * 2 * DQK
W_CV = H_C * DH
ROPE_THETA = 10000.0
G_D = 8
SG_CHUNK = 128
W_D = 1024
D_FF = 5632
Q_BLOCK = 128
EPS = 1e-6
N_GATES = 4 * H_B
N_EVEN_MAIN = 3 * W_A + 4 * W_B

LANES = 128
SUBLANES = 8
V7X_VMEM_BYTES = 64 * 1024 * 1024
VMEM_LIMIT = V7X_VMEM_BYTES - 8 * 1024 * 1024

NEG_INF = float("-inf")


def _cparams(*sem):
    return pltpu.CompilerParams(dimension_semantics=sem, vmem_limit_bytes=VMEM_LIMIT)


def _dot(a, b):
    return jnp.dot(a, b, preferred_element_type=F32)


def _dot_nt(a, b):
    return lax.dot_general(a, b, (((1,), (1,)), ((), ())), preferred_element_type=F32)


def _rms_lanes(x, g):
    ms = jnp.mean(x * x, axis=-1, keepdims=True)
    return x * lax.rsqrt(ms + EPS) * g


def _rms_lane_groups(x, g, group_ones, width):
    y = x * x
    hi = y.astype(BF16)
    mid = (y - hi.astype(F32)).astype(BF16)
    ss = _dot(hi, group_ones) + _dot(mid, group_ones)
    return x * lax.rsqrt(ss * (1.0 / width) + EPS) * g


def _norm_mod(x, g, scale, shift):
    return _rms_lanes(x, g) * (1.0 + scale) + shift


def _sigmoid(x):
    return 1.0 / (1.0 + jnp.exp(-x))


def _gelu_tanh(x):
    c = math.sqrt(2.0 / math.pi)
    return 0.5 * x * (1.0 + jnp.tanh(c * (x + 0.044715 * (x * x * x))))


CAST_BLK = 512


def _cast_job(w, layer, axis, step_of, n_steps, start=0):
    nblk = w.shape[axis] // CAST_BLK
    assert w.shape[axis] % CAST_BLK == 0 and n_steps >= start + nblk
    blk_of = lambda *g: jnp.clip(step_of(*g) - start, 0, nblk - 1)
    if axis == 2:
        in_spec = pl.BlockSpec((None, w.shape[1], CAST_BLK), lambda *g: (layer, 0, blk_of(*g)))
        out_spec = pl.BlockSpec((w.shape[1], CAST_BLK), lambda *g: (0, blk_of(*g)))
    else:
        in_spec = pl.BlockSpec((None, CAST_BLK, w.shape[2]), lambda *g: (layer, blk_of(*g), 0))
        out_spec = pl.BlockSpec((CAST_BLK, w.shape[2]), lambda *g: (blk_of(*g), 0))
    return in_spec, out_spec, jax.ShapeDtypeStruct(w.shape[1:], BF16)


MOD_TN = 1024


def _mod_kernel(c_ref, w_ref, b_ref, o_ref):
    c = c_ref[...]
    s = c * _sigmoid(c)
    o_ref[...] = _dot(s.astype(BF16), w_ref[...].astype(BF16)) + b_ref[...]


def _modulation(cond8, w_mod, b_mod):
    n = 6 * D_MODEL
    return pl.pallas_call(
        _mod_kernel,
        grid=(DEPTH, n // MOD_TN),
        in_specs=[
            pl.BlockSpec((SUBLANES, D_MODEL), lambda l, j: (0, 0)),
            pl.BlockSpec((None, D_MODEL, MOD_TN), lambda l, j: (l, 0, j)),
            pl.BlockSpec((None, 1, MOD_TN), lambda l, j: (l, 0, j)),
        ],
        out_specs=pl.BlockSpec((None, SUBLANES, MOD_TN), lambda l, j: (l, 0, j)),
        out_shape=jax.ShapeDtypeStruct((DEPTH, SUBLANES, n), F32),
        compiler_params=_cparams("parallel", "arbitrary"),
        name="modulation",
    )(cond8, w_mod, b_mod.reshape(DEPTH, 1, n))


DENSE_TM = 1024
NORM_ROWS = 256
PROJ_DTYPE = BF16


def _inproj_body(x_ref, g_ref, sc_ref, sh_ref, w_ref, o_ref, hb_ref, wg_ref=None, og_ref=None, cast_refs=None):
    j = pl.program_id(1)
    tm = x_ref.shape[0]
    if cast_refs is not None:
        cast_refs[1][...] = cast_refs[0][...].astype(BF16)

    @pl.when(j == 0)
    def _():
        for c in range(tm // NORM_ROWS):
            r = slice(c * NORM_ROWS, (c + 1) * NORM_ROWS)
            h = _norm_mod(x_ref[r, :], g_ref[...], sc_ref[...], sh_ref[...]).astype(BF16)
            hb_ref[r, :] = h
            o_ref[r, :] = _dot(h, w_ref[...]).astype(o_ref.dtype)
            if og_ref is not None:
                og_ref[r, :] = _dot(h, wg_ref[...])

    @pl.when(j > 0)
    def _():
        o_ref[...] = _dot(hb_ref[...], w_ref[...]).astype(o_ref.dtype)


def _inproj_kernel(x_ref, g_ref, sc_ref, sh_ref, w_ref, o_ref, hb_ref):
    _inproj_body(x_ref, g_ref, sc_ref, sh_ref, w_ref, o_ref, hb_ref)


def _inproj_gates_kernel(x_ref, g_ref, sc_ref, sh_ref, w_ref, wg_ref, o_ref, og_ref, hb_ref):
    _inproj_body(x_ref, g_ref, sc_ref, sh_ref, w_ref, o_ref, hb_ref, wg_ref, og_ref)


def _inproj_gates_cast_kernel(x_ref, g_ref, sc_ref, sh_ref, w_ref, wg_ref, wc_ref, o_ref, og_ref, wo_ref, hb_ref):
    _inproj_body(x_ref, g_ref, sc_ref, sh_ref, w_ref, o_ref, hb_ref, wg_ref, og_ref, (wc_ref, wo_ref))


def _mod_spec(idx, tm, rows_per_mod):
    return pl.BlockSpec((None, None, 1, D_MODEL), lambda i, n: ((i * tm) // rows_per_mod, idx, 0, 0))


def _inproj(x, g, mod, scale_idx, shift_idx, w, n, gates_block=None, tn=1024, cast=None):
    t = x.shape[0]
    tm = DENSE_TM
    rows_per_mod = t // mod.shape[0]
    in_specs = [
        pl.BlockSpec((tm, D_MODEL), lambda i, j: (i, 0)),
        pl.BlockSpec((1, D_MODEL), lambda i, j: (0, 0)),
        _mod_spec(scale_idx, tm, rows_per_mod),
        _mod_spec(shift_idx, tm, rows_per_mod),
        pl.BlockSpec((D_MODEL, tn), lambda i, j: (0, j)),
    ]
    out_specs = pl.BlockSpec((tm, tn), lambda i, j: (i, j))
    out_shape = jax.ShapeDtypeStruct((t, n), PROJ_DTYPE)
    args = [x, g.reshape(1, D_MODEL), mod, mod, w]
    kern = _inproj_kernel
    if gates_block is not None:
        in_specs.append(pl.BlockSpec((D_MODEL, LANES), lambda i, j: (0, gates_block)))
        out_specs = [out_specs, pl.BlockSpec((tm, LANES), lambda i, j: (i, 0))]
        out_shape = [out_shape, jax.ShapeDtypeStruct((t, LANES), F32)]
        args.append(w)
        kern = _inproj_gates_kernel
    if cast is not None:
        assert gates_block is not None
        nn = n // tn
        c_in, c_out, c_shape = _cast_job(*cast, lambda i, j: i * nn + j, (t // tm) * nn)
        in_specs.append(c_in)
        args.append(cast[0])
        out_specs.append(c_out)
        out_shape.append(c_shape)
        kern = _inproj_gates_cast_kernel
    return pl.pallas_call(
        kern,
        grid=(t // tm, n // tn),
        in_specs=in_specs,
        out_specs=out_specs,
        out_shape=out_shape,
        scratch_shapes=[pltpu.VMEM((tm, D_MODEL), BF16)],
        compiler_params=_cparams("arbitrary", "arbitrary"),
        name="inproj",
    )(*args)


OUT_TM = 512


def _outproj_kernel(a_ref, b_ref, wa_ref, wb_ref, x_ref, gate_ref, o_ref):
    acc = _dot(a_ref[...], wa_ref[...]) + _dot(b_ref[...], wb_ref[...])
    o_ref[...] = x_ref[...] + gate_ref[...] * acc


def _outproj_cast_kernel(a_ref, b_ref, wa_ref, wb_ref, x_ref, gate_ref, wc_ref, o_ref, wo_ref):
    wo_ref[...] = wc_ref[...].astype(BF16)
    _outproj_kernel(a_ref, b_ref, wa_ref, wb_ref, x_ref, gate_ref, o_ref)


def _outproj(mix_a, mix_b, w, layer, x, mod, gate_idx, cast=None):
    t = x.shape[0]
    tm = OUT_TM
    half = mix_a.shape[1]
    rows_per_mod = t // mod.shape[0]
    extra_in, extra_args, out_specs, out_shape = [], [], pl.BlockSpec((tm, D_MODEL), lambda i: (i, 0)), \
        jax.ShapeDtypeStruct((t, D_MODEL), F32)
    if cast is not None:
        c_in, c_out, c_shape = _cast_job(*cast, lambda i: i, t // tm)
        extra_in, extra_args = [c_in], [cast[0]]
        out_specs, out_shape = [out_specs, c_out], [out_shape, c_shape]
    return pl.pallas_call(
        _outproj_kernel if cast is None else _outproj_cast_kernel,
        grid=(t // tm,),
        in_specs=[
            pl.BlockSpec((tm, half), lambda i: (i, 0)),
            pl.BlockSpec((tm, half), lambda i: (i, 0)),
            pl.BlockSpec((None, half, D_MODEL), lambda i: (layer, 0, 0)),
            pl.BlockSpec((None, half, D_MODEL), lambda i: (layer, 1, 0)),
            pl.BlockSpec((tm, D_MODEL), lambda i: (i, 0)),
            pl.BlockSpec((None, None, 1, D_MODEL), lambda i: ((i * tm) // rows_per_mod, gate_idx, 0, 0)),
        ] + extra_in,
        out_specs=out_specs,
        out_shape=out_shape,
        compiler_params=_cparams("arbitrary"),
        name="outproj",
    )(mix_a, mix_b, w, w, x, mod, *extra_args)


FFN_TF = 512
FFN_ROWS = 512


def _ffn_kernel(x_ref, g_ref, sc_ref, sh_ref, gate_ref, wg_ref, wv_ref, cwg_ref, cwv_ref, cbg_ref, cbv_ref,
                wd_ref, o_ref, hb_ref, *, seq_len):
    f = pl.program_id(1)
    tm = x_ref.shape[0]

    rp = max(seq_len, FFN_ROWS)
    pos = lax.broadcasted_iota(jnp.int32, (rp, 1), 0) & (seq_len - 1)
    first = pos == 0
    last = pos == seq_len - 1

    def conv(a, cw_ref, cb_ref):
        prev = jnp.where(first, 0.0, pltpu.roll(a, 1, 0))
        nxt = jnp.where(last, 0.0, pltpu.roll(a, rp - 1, 0))
        return prev * cw_ref[0:1, :] + a * cw_ref[1:2, :] + nxt * cw_ref[2:3, :] + cb_ref[...]

    parts = [slice(r, r + rp) for r in range(0, tm, rp)]

    def hidden_block(first):
        acts = []
        for rows in parts:
            if first:
                for c in range(rows.start, rows.stop, NORM_ROWS):
                    r = slice(c, c + NORM_ROWS)
                    h = _norm_mod(x_ref[r, :], g_ref[...], sc_ref[...], sh_ref[...])
                    hb_ref[r, :] = h.astype(BF16)
            hb = hb_ref[rows, :]
            cg = conv(_dot(hb, wg_ref[...]), cwg_ref, cbg_ref)
            cv = conv(_dot(hb, wv_ref[...]), cwv_ref, cbv_ref)
            acts.append((cg * _sigmoid(cg) * cv).astype(BF16))
        for rows, act in zip(parts, acts):
            if first:
                o_ref[rows, :] = _dot(act, wd_ref[...])
            else:
                o_ref[rows, :] += _dot(act, wd_ref[...])

    @pl.when(f == 0)
    def _():
        hidden_block(True)

    @pl.when(f > 0)
    def _():
        hidden_block(False)

    @pl.when(f == pl.num_programs(1) - 1)
    def _():
        o_ref[...] = x_ref[...] + gate_ref[...] * o_ref[...]


def _ffn(x, g, mod, w_up, conv_w, conv_b, w_down, seq_len):
    t = x.shape[0]
    tm, tf = DENSE_TM, FFN_TF
    nf = D_FF // tf
    rows_per_mod = t // mod.shape[0]
    conv_b = conv_b.reshape(1, 2 * D_FF)
    assert seq_len & (seq_len - 1) == 0 and tm % seq_len == 0
    return pl.pallas_call(
        functools.partial(_ffn_kernel, seq_len=seq_len),
        grid=(t // tm, nf),
        in_specs=[
            pl.BlockSpec((tm, D_MODEL), lambda i, f: (i, 0), pipeline_mode=pl.Buffered(1)),
            pl.BlockSpec((1, D_MODEL), lambda i, f: (0, 0)),
            _mod_spec(4, tm, rows_per_mod),
            _mod_spec(3, tm, rows_per_mod),
            _mod_spec(5, tm, rows_per_mod),
            pl.BlockSpec((D_MODEL, tf), lambda i, f: (0, f)),
            pl.BlockSpec((D_MODEL, tf), lambda i, f: (0, nf + f)),
            pl.BlockSpec((3, tf), lambda i, f: (0, f)),
            pl.BlockSpec((3, tf), lambda i, f: (0, nf + f)),
            pl.BlockSpec((1, tf), lambda i, f: (0, f)),
            pl.BlockSpec((1, tf), lambda i, f: (0, nf + f)),
            pl.BlockSpec((tf, D_MODEL), lambda i, f: (f, 0)),
        ],
        out_specs=pl.BlockSpec((tm, D_MODEL), lambda i, f: (i, 0)),
        out_shape=jax.ShapeDtypeStruct((t, D_MODEL), F32),
        scratch_shapes=[pltpu.VMEM((tm, D_MODEL), BF16)],
        compiler_params=_cparams("parallel", "arbitrary"),
        name="convffn",
    )(x, g.reshape(1, D_MODEL), mod, mod, mod, w_up, w_up, conv_w, conv_w, conv_b, conv_b, w_down)


def _attn_ctx_kernel(q_ref, k_ref, v_ref, gq_ref, gk_ref, wc_ref, o_ref, ko_ref, vo_ref, wo_ref):
    wo_ref[...] = wc_ref[...].astype(BF16)
    scale = DH ** -0.5
    for h in range(H_A):
        sl = slice(h * DH, (h + 1) * DH)
        q = _rms_lanes(q_ref[:, sl].astype(F32), gq_ref[...])
        k = _rms_lanes(k_ref[:, sl].astype(F32), gk_ref[...])
        v = v_ref[:, sl].astype(F32)
        ko_ref[h] = k
        vo_ref[h] = v
        s = _dot_nt(q.astype(BF16), k.astype(BF16)) * scale
        p = jnp.exp(s - jnp.max(s, axis=-1, keepdims=True))
        p = p / jnp.sum(p, axis=-1, keepdims=True)
        o_ref[:, sl] = _dot(p.astype(BF16), v.astype(BF16)).astype(o_ref.dtype)


def _attn_ctx(proj, gq, gk, batch, seq, cast):
    t = proj.shape[0]
    c_in, c_out, c_shape = _cast_job(*cast, lambda b: b, batch)
    kv_shape = jax.ShapeDtypeStruct((batch, 1, H_A, seq, DH), F32)
    kv_spec = pl.BlockSpec((None, None, H_A, seq, DH), lambda b: (b, 0, 0, 0, 0))
    return pl.pallas_call(
        _attn_ctx_kernel,
        grid=(batch,),
        in_specs=[
            pl.BlockSpec((seq, W_A), lambda b: (b, 0)),
            pl.BlockSpec((seq, W_A), lambda b: (b, 1)),
            pl.BlockSpec((seq, W_A), lambda b: (b, 2)),
            pl.BlockSpec((1, DH), lambda b: (0, 0)),
            pl.BlockSpec((1, DH), lambda b: (0, 0)),
            c_in,
        ],
        out_specs=[pl.BlockSpec((seq, W_A), lambda b: (b, 0)), kv_spec, kv_spec, c_out],
        out_shape=[jax.ShapeDtypeStruct((t, W_A), BF16), kv_shape, kv_shape, c_shape],
        compiler_params=_cparams("arbitrary"),
        name="attn_ctx",
    )(proj, proj, proj, gq.reshape(1, DH), gk.reshape(1, DH), cast[0])


NA_ROWS = 16
NA_PAIRS = 2 * WIN_H - 2
RPB_H = 2 * WIN_H - 1
RPB_W = 2 * WIN_W - 1


def _natten_kernel(rpb_ref, q_ref, k_ref, v_ref, kc_ref, vc_ref, gq_ref, gk_ref, o_ref,
                   qn_s, kn_s, v_s, kc_s, vc_s, bias_s):
    h = pl.program_id(0)
    scale = DH ** -0.5
    qn_s[...] = _rms_lanes(q_ref[...].astype(F32), gq_ref[...]).astype(BF16)
    kn_s[...] = _rms_lanes(k_ref[...].astype(F32), gk_ref[...]).astype(BF16)
    v_s[...] = v_ref[...].astype(BF16)
    kc_s[...] = kc_ref[...].astype(BF16)
    vc_s[...] = vc_ref[...].astype(BF16)

    lane = lax.broadcasted_iota(jnp.int32, (GRID_W, LANES), 1)
    qc = lax.broadcasted_iota(jnp.int32, (GRID_W, LANES), 0)
    kcol = lane % GRID_W
    upper = lane >= GRID_W
    dcol = kcol - qc + (WIN_W - 1)
    col0 = jnp.clip(qc - WIN_W // 2, 0, GRID_W - WIN_W)
    col_ok = (kcol >= col0) & (kcol < col0 + WIN_W)
    base = h * (RPB_H * RPB_W)

    def build(p, carry):
        acc = jnp.zeros((GRID_W, LANES), F32)
        for d in range(RPB_W):
            lo = rpb_ref[base + p * RPB_W + d]
            hi = rpb_ref[base + (p + 1) * RPB_W + d]
            acc = jnp.where(dcol == d, jnp.where(upper, hi, lo), acc)
        bias_s[p] = jnp.where(col_ok, acc, NEG_INF)
        return carry

    @pl.when(pl.program_id(1) == 0)
    def _():
        lax.fori_loop(0, NA_PAIRS, build, 0)

    kh = WIN_H
    n_loc = kh * GRID_W
    row0 = [min(max(r - kh // 2, 0), NA_ROWS - kh) for r in range(NA_ROWS)]
    q = [qn_s[r * GRID_W:(r + 1) * GRID_W, :] for r in range(NA_ROWS)]
    kc = kc_s[...]
    s_loc, s_ctx = [], []
    for r in range(NA_ROWS):
        dr0 = row0[r] - r + WIN_H - 1
        k_loc = kn_s[row0[r] * GRID_W:row0[r] * GRID_W + n_loc, :]
        bias = jnp.concatenate([bias_s[dr0 + 2 * i] for i in range(kh // 2)], axis=-1)
        s_loc.append(_dot_nt(q[r], k_loc) * scale + bias)
        s_ctx.append(_dot_nt(q[r], kc) * scale)
    p_loc, p_ctx = [], []
    for r in range(NA_ROWS):
        m = jnp.maximum(jnp.max(s_loc[r], axis=-1, keepdims=True), jnp.max(s_ctx[r], axis=-1, keepdims=True))
        e_loc = jnp.exp(s_loc[r] - m)
        e_ctx = jnp.exp(s_ctx[r] - m)
        l = jnp.sum(e_loc, axis=-1, keepdims=True) + jnp.sum(e_ctx, axis=-1, keepdims=True)
        p_loc.append((e_loc / l).astype(BF16))
        p_ctx.append((e_ctx / l).astype(BF16))
    vc = vc_s[...]
    for r in range(NA_ROWS):
        v_loc = v_s[row0[r] * GRID_W:row0[r] * GRID_W + n_loc, :]
        out = _dot(p_loc[r], v_loc) + _dot(p_ctx[r], vc)
        o_ref[r * GRID_W:(r + 1) * GRID_W, :] = out.astype(o_ref.dtype)


def _natten(proj, cache_k, cache_v, gq, gk, rpb, batch, seq):
    t = proj.shape[0]
    past = cache_k.shape[3]
    ctx_spec = pl.BlockSpec((None, None, None, past, DH), lambda h, b: (b, 0, h, 0, 0))
    return pl.pallas_call(
        _natten_kernel,
        grid=(H_A, batch),
        in_specs=[
            pl.BlockSpec(memory_space=pltpu.SMEM),
            pl.BlockSpec((seq, DH), lambda h, b: (b, h)),
            pl.BlockSpec((seq, DH), lambda h, b: (b, H_A + h)),
            pl.BlockSpec((seq, DH), lambda h, b: (b, 2 * H_A + h)),
            ctx_spec,
            ctx_spec,
            pl.BlockSpec((1, DH), lambda h, b: (0, 0)),
            pl.BlockSpec((1, DH), lambda h, b: (0, 0)),
        ],
        out_specs=pl.BlockSpec((seq, DH), lambda h, b: (b, h)),
        out_shape=jax.ShapeDtypeStruct((t, W_A), BF16),
        scratch_shapes=[
            pltpu.VMEM((seq, DH), BF16),
            pltpu.VMEM((seq, DH), BF16),
            pltpu.VMEM((seq, DH), BF16),
            pltpu.VMEM((past, DH), BF16),
            pltpu.VMEM((past, DH), BF16),
            pltpu.VMEM((NA_PAIRS, GRID_W, LANES), F32),
        ],
        compiler_params=_cparams("arbitrary", "arbitrary"),
        name="natten",
    )(rpb.reshape(-1), proj, proj, proj, cache_k, cache_v, gq.reshape(1, DH), gk.reshape(1, DH))


ML_EPILOGUE_ROWS = 256


def _split3(x):
    hi = x.astype(BF16)
    r1 = x - hi.astype(F32)
    mid = r1.astype(BF16)
    lo = (r1 - mid.astype(F32)).astype(BF16)
    return hi, mid, lo


def _mlstm_kernel(*refs, has_state, emit_state, n_cast, seq):
    it = iter(refs)
    q_ref, k_ref, v_ref, ob_ref, gt_ref, mlb_ref, mlg_ref = (next(it) for _ in range(7))
    if has_state:
        c0_ref, n0_ref, m0_ref = (next(it) for _ in range(3))
    wc_refs = [next(it) for _ in range(n_cast)]
    o_ref = next(it)
    if emit_state:
        co_ref, no_ref, mo_ref = (next(it) for _ in range(3))
    for wc_ref in wc_refs:
        next(it)[...] = wc_ref[...].astype(BF16)
    r_s, rt_s, bt_s, hf_s, hb_s, ct_s, n_s, m_s = (next(it) for _ in range(8))

    L = ML_CHUNK
    nc = seq // L
    kscale = DH ** -0.5

    ri = lax.broadcasted_iota(jnp.int32, (L, L), 0)
    ci = lax.broadcasted_iota(jnp.int32, (L, L), 1)
    tri_pre = (ci <= ri).astype(BF16)
    tri_suf = (ci >= ri).astype(BF16)
    mask_t = (ri <= ci, ri >= ci)
    fwd_lane = lax.broadcasted_iota(jnp.int32, (L, LANES), 1) < 2 * H_B

    for c in range(nc):
        rows = slice(c * L, (c + 1) * L)
        g = gt_ref[rows, :] + mlb_ref[...]
        gf = jnp.minimum(g, 0.0) - jnp.log1p(jnp.exp(-jnp.abs(g)))
        hi, mid, lo = _split3(gf)
        pre = _dot(tri_pre, hi) + _dot(tri_pre, mid) + _dot(tri_pre, lo)
        suf = _dot(tri_suf, hi) + _dot(tri_suf, mid) + _dot(tri_suf, lo)
        b = pltpu.roll(jnp.where(fwd_lane, pre, suf), LANES - H_B, 1)
        r = g - b
        r_s[rows, :] = r
        rt_s[c] = r.T
        bt_s[c] = b.T

    for d in range(2):
        for h in range(H_B):
            ct_s[d, h] = c0_ref[d, h].T if has_state else jnp.zeros((DH, DH), F32)
    if has_state:
        n_s[...] = n0_ref[...]
        m_s[...] = m0_ref[...]
    else:
        n_s[...] = jnp.zeros_like(n_s)
        m_s[...] = jnp.zeros_like(m_s)

    pairs = [(d, h) for d in range(2) for h in range(H_B)]

    def step(c, carry):
        rows, r_all, rt_all, bt_all = [], [], [], []
        for d in range(2):
            cc = c if d == 0 else nc - 1 - c
            rows.append(pl.ds(pl.multiple_of(cc * L, L), L))
            r_all.append(r_s[rows[d], :])
            rt_all.append(rt_s[cc])
            bt_all.append(bt_s[cc])

        qb, kb, v_t, rm, r_row, b_row, a_row, qk = {}, {}, {}, {}, {}, {}, {}, {}
        for p in pairs:
            d, h = p
            col = 2 * H_B * d + h
            sl = slice(h * DH, (h + 1) * DH)
            qb[p] = q_ref[rows[d], sl].astype(BF16)
            kb[p] = (k_ref[rows[d], sl].astype(F32) * kscale).astype(BF16)
            v_t[p] = v_ref[rows[d], sl].astype(F32).T
            r_row[p] = rt_all[d][col:col + 1, :]
            b_row[p] = bt_all[d][col:col + 1, :]
            rm[p] = jnp.where(mask_t[d], jnp.broadcast_to(r_all[d][:, col:col + 1], (L, L)), NEG_INF)
            a_row[p] = jnp.max(rm[p], axis=0, keepdims=True)
            qk[p] = _dot_nt(kb[p], qb[p])

        m_row = {(d, h): m_s[d, h:h + 1, :] for d, h in pairs}
        n_row = {(d, h): n_s[d, h:h + 1, :] for d, h in pairs}
        ct = {(d, h): ct_s[d, h] for d, h in pairs}
        big_m = {p: jnp.maximum(m_row[p], a_row[p]) for p in pairs}
        w_in = {p: jnp.exp(m_row[p] - big_m[p]) for p in pairs}
        cq = {p: _dot_nt(ct[p].astype(BF16), qb[p]) for p in pairs}
        qn = {p: _dot_nt(jnp.broadcast_to(n_row[p], (SUBLANES, DH)).astype(BF16), qb[p])[0:1, :] for p in pairs}
        w_t = {p: jnp.exp(rm[p] - big_m[p]) * qk[p] for p in pairs}
        pv = {p: _dot(v_t[p].astype(BF16), w_t[p].astype(BF16)) for p in pairs}
        for p in pairs:
            d, h = p
            den = w_in[p] * qn[p] + jnp.sum(w_t[p], axis=0, keepdims=True)
            h_t = (w_in[p] * cq[p] + pv[p]) / jnp.maximum(jnp.abs(den), jnp.exp(-(b_row[p] + big_m[p])))
            (hf_s if d == 0 else hb_s)[rows[d], slice(h * DH, (h + 1) * DH)] = h_t.T

        m_last, ws = {}, {}
        for p in pairs:
            m_last[p] = jnp.maximum(m_row[p], jnp.max(r_row[p], axis=-1, keepdims=True))
            ws[p] = jnp.exp(r_row[p] - m_last[p])
        kv = {p: _dot((v_t[p] * ws[p]).astype(BF16), kb[p]) for p in pairs}
        nk = {p: _dot(jnp.broadcast_to(ws[p], (SUBLANES, L)).astype(BF16), kb[p])[0:1, :] for p in pairs}
        for p in pairs:
            d, h = p
            a_prev = jnp.exp(m_row[p] - m_last[p])
            bl = b_row[p][:, L - 1:L] if d == 0 else b_row[p][:, 0:1]
            ct_s[d, h] = a_prev * ct[p] + kv[p]
            n_s[d, h:h + 1, :] = a_prev * n_row[p] + nk[p]
            m_s[d, h:h + 1, :] = bl + m_last[p]
        return carry

    lax.fori_loop(0, nc, step, 0)

    if emit_state:
        for d in range(2):
            for h in range(H_B):
                co_ref[d, h] = ct_s[d, h].T
        no_ref[...] = n_s[...]
        mo_ref[...] = m_s[...]

    for r in range(seq // ML_EPILOGUE_ROWS):
        rr = slice(r * ML_EPILOGUE_ROWS, (r + 1) * ML_EPILOGUE_ROWS)
        for h in range(H_B):
            sl = slice(h * DH, (h + 1) * DH)
            hm = hf_s[rr, sl] + hb_s[rr, sl]
            ob = ob_ref[rr, sl].astype(F32)
            o_ref[rr, sl] = (_sigmoid(ob) * _rms_lanes(hm, mlg_ref[:, sl])).astype(o_ref.dtype)


def _mlstm(proj, gates, ml_b, ml_g, batch, seq, state=None, emit_state=False, casts=()):
    t = proj.shape[0]
    col0 = 3 * W_A // W_B
    big = lambda j: pl.BlockSpec((seq, W_B), lambda b: (b, col0 + j))
    in_specs = [big(0), big(1), big(2), big(3),
                pl.BlockSpec((seq, LANES), lambda b: (b, 0)),
                pl.BlockSpec((1, LANES), lambda b: (0, 0)),
                pl.BlockSpec((1, W_B), lambda b: (0, 0))]
    args = [proj, proj, proj, proj, gates, ml_b, ml_g.reshape(1, W_B)]
    c_spec = pl.BlockSpec((None, None, 2, H_B, DH, DH), lambda b: (b, 0, 0, 0, 0, 0))
    n_spec = pl.BlockSpec((None, None, 2, H_B, DH), lambda b: (b, 0, 0, 0, 0))
    if state is not None:
        in_specs += [c_spec, n_spec, n_spec]
        args += list(state)
    out_specs = [pl.BlockSpec((seq, W_B), lambda b: (b, 0))]
    out_shape = [jax.ShapeDtypeStruct((t, W_B), BF16)]
    if emit_state:
        out_specs += [c_spec, n_spec, n_spec]
        out_shape += [jax.ShapeDtypeStruct((batch, 1, 2, H_B, DH, DH), F32),
                      jax.ShapeDtypeStruct((batch, 1, 2, H_B, DH), F32),
                      jax.ShapeDtypeStruct((batch, 1, 2, H_B, LANES), F32)]
    start = 0
    for cast in casts:
        c_in, c_out, c_shape = _cast_job(*cast, lambda b: b, batch, start)
        start += cast[0].shape[cast[2]] // CAST_BLK
        in_specs.append(c_in)
        args.append(cast[0])
        out_specs.append(c_out)
        out_shape.append(c_shape)
    nc = seq // ML_CHUNK
    return pl.pallas_call(
        functools.partial(_mlstm_kernel, has_state=state is not None, emit_state=emit_state,
                          n_cast=len(casts), seq=seq),
        grid=(batch,),
        in_specs=in_specs,
        out_specs=out_specs,
        out_shape=out_shape,
        scratch_shapes=[
            pltpu.VMEM((seq, LANES), F32),
            pltpu.VMEM((nc, ML_CHUNK, LANES), F32),
            pltpu.VMEM((nc, ML_CHUNK, LANES), F32),
            pltpu.VMEM((seq, W_B), F32),
            pltpu.VMEM((seq, W_B), F32),
            pltpu.VMEM((2, H_B, DH, DH), F32),
            pltpu.VMEM((2, H_B, DH), F32),
            pltpu.VMEM((2, H_B, LANES), F32),
        ],
        compiler_params=_cparams("arbitrary"),
        name="mlstm",
    )(*args)


DIFF_QB = 256


def _diff_kernel(*refs, heads, seq, past, rope, emit_ctx, has_cast, lam_init):
    it = iter(refs)
    q_ref, k_ref, v_ref, gq_ref, gk_ref, go_ref, lam_ref = (next(it) for _ in range(7))
    if past:
        kc_ref, vc_ref = next(it), next(it)
    if rope:
        cos_ref, s1_ref, s2_ref = next(it), next(it), next(it)
    if has_cast:
        wc_ref = next(it)
    o_ref = next(it)
    if emit_ctx:
        ko_ref, vo_ref = next(it), next(it)
    if has_cast:
        wo_ref = next(it)
        wo_ref[...] = wc_ref[...].astype(BF16)

    scale = DQK ** -0.5
    lv = lam_ref[...]
    lam = (jnp.exp(jnp.sum(lv[0:1, :] * lv[1:2, :], axis=-1, keepdims=True))
           - jnp.exp(jnp.sum(lv[2:3, :] * lv[3:4, :], axis=-1, keepdims=True)) + lam_init)
    low = lax.broadcasted_iota(jnp.int32, (seq, LANES), 1) < DQK
    hr = lax.broadcasted_iota(jnp.int32, (LANES, LANES), 0) < DQK
    hc = lax.broadcasted_iota(jnp.int32, (LANES, LANES), 1) < DQK
    halves = (hr == hc).astype(BF16)

    def apply_rope(x):
        return (x * cos_ref[...] + pltpu.roll(x, LANES - DQK // 4, 1) * s1_ref[...]
                + pltpu.roll(x, DQK // 4, 1) * s2_ref[...])

    qb_rows = min(DIFF_QB, seq)
    blocks = [slice(i * qb_rows, (i + 1) * qb_rows) for i in range(seq // qb_rows)]
    head_sl = [slice(hh * DH, (hh + 1) * DH) for hh in range(heads)]
    qn = [_rms_lane_groups(q_ref[:, sl].astype(F32), gq_ref[...], halves, DQK) for sl in head_sl]
    kn = [_rms_lane_groups(k_ref[:, sl].astype(F32), gk_ref[...], halves, DQK) for sl in head_sl]
    v = [v_ref[:, sl].astype(F32) for sl in head_sl]
    if emit_ctx:
        for hh in range(heads):
            ko_ref[hh] = kn[hh]
            vo_ref[hh] = v[hh]
    if rope:
        qn = [apply_rope(x) for x in qn]
        kn = [apply_rope(x) for x in kn]
    assert math.frexp(scale)[0] == 0.5
    q0 = [(jnp.where(low, x, 0.0) * scale).astype(BF16) for x in qn]
    q1 = [(jnp.where(low, 0.0, x) * scale).astype(BF16) for x in qn]
    ka, vt = [], []
    for hh in range(heads):
        k_parts = [kn[hh].astype(BF16)]
        vt_parts = [v[hh][c * LANES:(c + 1) * LANES, :].T.astype(BF16) for c in range(seq // LANES)]
        if past:
            k_parts.append(kc_ref[hh].astype(BF16))
            vt_parts += [vc_ref[hh, c * LANES:(c + 1) * LANES, :].T.astype(BF16) for c in range(past // LANES)]
        ka.append(jnp.concatenate(k_parts, axis=0))
        vt.append(jnp.concatenate(vt_parts, axis=1))

    work = [(hh, rows) for hh in range(heads) for rows in blocks]
    s0 = [_dot_nt(ka[hh], q0[hh][rows, :]) for hh, rows in work]
    s1 = [_dot_nt(ka[hh], q1[hh][rows, :]) for hh, rows in work]

    def softmax_t(s, weight):
        p = jnp.exp(s - jnp.max(s, axis=0, keepdims=True))
        return p * (weight / jnp.sum(p, axis=0, keepdims=True))

    a_t = [(softmax_t(a, 1.0) - softmax_t(b, lam)).astype(BF16) for a, b in zip(s0, s1)]
    out_t = [_dot(vt[hh], a) for (hh, _), a in zip(work, a_t)]
    for (hh, rows), o in zip(work, out_t):
        o = o * lax.rsqrt(jnp.mean(o * o, axis=0, keepdims=True) + EPS)
        g_row = go_ref[:, head_sl[hh]] * (1.0 - lam_init)
        for c in range(qb_rows // LANES):
            r0 = rows.start + c * LANES
            o_ref[r0:r0 + LANES, head_sl[hh]] = (o[:, c * LANES:(c + 1) * LANES].T * g_row).astype(o_ref.dtype)


def _diff_attn(proj, gq, gk, g_out, lam_vec, lam_init, batch, seq, heads, cache=None, rope=None,
               emit_ctx=False, cast=None):
    t = proj.shape[0]
    nh = H_C // heads
    w = heads * DH
    kblk = W_CQK // w
    past = 0 if cache is None else cache[0].shape[3]
    in_specs = [
        pl.BlockSpec((seq, w), lambda b, j: (b, j)),
        pl.BlockSpec((seq, w), lambda b, j: (b, kblk + j)),
        pl.BlockSpec((seq, w), lambda b, j: (b, 2 * kblk + j)),
        pl.BlockSpec((1, DH), lambda b, j: (0, 0)),
        pl.BlockSpec((1, DH), lambda b, j: (0, 0)),
        pl.BlockSpec((1, w), lambda b, j: (0, j)),
        pl.BlockSpec((4, DQK), lambda b, j: (0, 0)),
    ]
    args = [proj, proj, proj, jnp.tile(gq, 2).reshape(1, DH), jnp.tile(gk, 2).reshape(1, DH),
            g_out.reshape(1, W_CV), lam_vec]
    if cache is not None:
        ctx_spec = pl.BlockSpec((None, None, heads, past, DH), lambda b, j: (b, 0, j, 0, 0))
        in_specs += [ctx_spec, ctx_spec]
        args += [cache[0].reshape(cache[0].shape[:4] + (DH,)), cache[1]]
    if rope is not None:
        tab = pl.BlockSpec((seq, DH), lambda b, j: (0, 0))
        in_specs += [tab, tab, tab]
        args += list(rope)
    out_specs = [pl.BlockSpec((seq, w), lambda b, j: (b, j))]
    out_shape = [jax.ShapeDtypeStruct((t, W_CV), BF16)]
    if emit_ctx:
        kv_spec = pl.BlockSpec((None, None, heads, seq, DH), lambda b, j: (b, 0, j, 0, 0))
        kv_shape = jax.ShapeDtypeStruct((batch, 1, H_C, seq, DH), F32)
        out_specs += [kv_spec, kv_spec]
        out_shape += [kv_shape, kv_shape]
    if cast is not None:
        c_in, c_out, c_shape = _cast_job(*cast, lambda b, j: b * nh + j, batch * nh)
        in_specs.append(c_in)
        args.append(cast[0])
        out_specs.append(c_out)
        out_shape.append(c_shape)
    return pl.pallas_call(
        functools.partial(_diff_kernel, heads=heads, seq=seq, past=past, rope=rope is not None,
                          emit_ctx=emit_ctx, has_cast=cast is not None, lam_init=lam_init),
        grid=(batch, nh),
        in_specs=in_specs,
        out_specs=out_specs,
        out_shape=out_shape,
        compiler_params=_cparams("arbitrary", "arbitrary"),
        name="diff_attn",
    )(*args)


def _rope_tables(seq):
    tpos = jnp.arange(seq)
    pos = jnp.stack([tpos // GRID_W, tpos % GRID_W], axis=-1).astype(F32)
    half = DQK // 2
    inv = ROPE_THETA ** (-jnp.arange(0, half, 2, dtype=F32) / half)
    ang = pos[:, :, None] * inv
    cos = jnp.cos(ang)
    sin = jnp.sin(ang)
    zero = jnp.zeros_like(sin)
    lay = lambda first, second: jnp.tile(jnp.concatenate([first, second], axis=-1).reshape(seq, DQK), (1, 2))
    return lay(cos, cos), lay(-sin, zero), lay(zero, sin)


SGU_TM = 512


def _sgu_kernel(*refs, has_cast):
    if has_cast:
        u_ref, vd_ref, sgg_ref, sgw_ref, sgb_ref, wc_ref, o_ref, wo_ref = refs
        wo_ref[...] = wc_ref[...].astype(BF16)
    else:
        u_ref, vd_ref, sgg_ref, sgw_ref, sgb_ref, o_ref = refs
    tm = u_ref.shape[0]
    for c in range(tm // SG_CHUNK):
        rows = slice(c * SG_CHUNK, (c + 1) * SG_CHUNK)
        vn = _rms_lanes(_gelu_tanh(vd_ref[rows, :].astype(F32)), sgg_ref[...]).astype(BF16)
        for g in range(G_D):
            sl = slice(g * DH, (g + 1) * DH)
            gate = _dot(sgw_ref[g].astype(BF16), vn[:, sl]) + sgb_ref[:, g:g + 1]
            o_ref[rows, sl] = (_gelu_tanh(u_ref[rows, sl].astype(F32)) * gate).astype(o_ref.dtype)


def _sgu(proj, sg_g, sg_w, sg_b, cast=None):
    t = proj.shape[0]
    tm = SGU_TM
    ublk = (2 * W_CQK + W_CV) // W_D
    in_specs = [
        pl.BlockSpec((tm, W_D), lambda i: (i, ublk)),
        pl.BlockSpec((tm, W_D), lambda i: (i, ublk + 1)),
        pl.BlockSpec((1, W_D), lambda i: (0, 0)),
        pl.BlockSpec((G_D, SG_CHUNK, SG_CHUNK), lambda i: (0, 0, 0)),
        pl.BlockSpec((SG_CHUNK, G_D), lambda i: (0, 0)),
    ]
    args = [proj, proj, sg_g.reshape(1, W_D), sg_w, sg_b.T]
    out_specs = [pl.BlockSpec((tm, W_D), lambda i: (i, 0))]
    out_shape = [jax.ShapeDtypeStruct((t, W_D), BF16)]
    if cast is not None:
        c_in, c_out, c_shape = _cast_job(*cast, lambda i: i, t // tm)
        in_specs.append(c_in)
        args.append(cast[0])
        out_specs.append(c_out)
        out_shape.append(c_shape)
    return pl.pallas_call(
        functools.partial(_sgu_kernel, has_cast=cast is not None),
        grid=(t // tm,),
        in_specs=in_specs,
        out_specs=out_specs,
        out_shape=out_shape,
        compiler_params=_cparams("arbitrary"),
        name="sgu",
    )(*args)


def _lambda_init(layer):
    return 0.8 - 0.6 * math.exp(-0.3 * layer)


def _trunk(x, mods, p, batch, seq, caches, ffn_w):
    ctx_out = {}
    for l in range(DEPTH):
        mod = mods[l]
        if l % 2 == 0:
            if caches is None:
                proj, gates, w_up1 = _inproj(x, p["g_mix"][l], mod, 1, 0, p["w_even"], N_EVEN_MAIN,
                                             N_EVEN_MAIN // LANES, cast=(p["ffn_w_up"], 1, 2))
                mix_a, ctx_out["na_k"], ctx_out["na_v"], w_down0 = _attn_ctx(
                    proj, p["na_gq"], p["na_gk"], batch, seq, cast=(p["ffn_w_down"], 0, 1))
                mix_b, c1, n1, m1, w_up0, ffn_w["w_odd"] = _mlstm(
                    proj, gates, p["ml_b"], p["ml_g"], batch, seq, emit_state=True,
                    casts=((p["ffn_w_up"], 0, 2), (p["w_in_odd"], 0, 2)))
                ffn_w[0] = [w_up0, w_down0]
                ffn_w[1] = [w_up1, None]
                ctx_out.update(mlstm_C=c1, mlstm_n=n1, mlstm_m=m1[..., 0])
            else:
                proj, gates = _inproj(x, p["g_mix"][l], mod, 1, 0, p["w_even"], N_EVEN_MAIN, N_EVEN_MAIN // LANES)
                mix_a = _natten(proj, caches["na_k"], caches["na_v"], p["na_gq"], p["na_gk"], p["na_rpb"], batch, seq)
                (mix_b,) = _mlstm(proj, gates, p["ml_b"], p["ml_g"], batch, seq,
                                  state=(caches["C"], caches["n"], caches["m"]))
        else:
            proj = _inproj(x, p["g_mix"][l], mod, 1, 0, ffn_w["w_odd"], ffn_w["w_odd"].shape[1])
            lam_init = _lambda_init(l)
            if caches is None:
                mix_a, dk, dv = _diff_attn(proj, p["diff_gq"], p["diff_gk"], p["diff_g_out"], p["diff_lam"],
                                           lam_init, batch, seq, heads=H_C, emit_ctx=True)
                ctx_out["diff_k"] = dk.reshape(batch, 1, H_C, seq, 2, DQK)
                ctx_out["diff_v"] = dv
            else:
                (mix_a,) = _diff_attn(proj, p["diff_gq"], p["diff_gk"], p["diff_g_out"], p["diff_lam"],
                                      lam_init, batch, seq, heads=1,
                                      cache=(caches["diff_k"], caches["diff_v"]), rope=_rope_tables(seq))
            (mix_b,) = _sgu(proj, p["sg_g"], p["sg_w"], p["sg_b"])
        if caches is None and l == 0:
            x, ffn_w[1][1] = _outproj(mix_a, mix_b, p["w_out"], l, x, mod, 2, cast=(p["ffn_w_down"], 1, 1))
        else:
            x = _outproj(mix_a, mix_b, p["w_out"], l, x, mod, 2)
        x = _ffn(x, p["g_ffn"][l], mod, ffn_w[l][0], p["conv_w"][l], p["conv_b"][l], ffn_w[l][1], seq)
    return x, ctx_out


def kernel(x_prompt, x_sample, cache_na_k, cache_na_v, state_mlstm_C, state_mlstm_n, state_mlstm_m,
           cache_diff_k, cache_diff_v, c, c_ctx, w_mod, b_mod, g_mix, g_ffn, w_out, w_in_even,
           na_gq, na_gk, na_rpb, ml_b_gates, ml_g_out, w_in_odd, diff_gq, diff_gk, diff_lam,
           diff_g_out, sg_g_v, sg_w, sg_b, ffn_w_up, ffn_conv_w, ffn_conv_b, ffn_w_down):
    batch, seq, _ = x_prompt.shape
    dbatch, dseq, _ = x_sample.shape

    cond8 = jnp.zeros((SUBLANES, D_MODEL), F32).at[0].set(c_ctx).at[1:1 + dbatch].set(c)
    mod = _modulation(cond8, w_mod, b_mod)
    mods_ctx = [mod[l, 0:1].reshape(1, 6, 1, D_MODEL) for l in range(DEPTH)]
    mods_lat = [mod[l, 1:1 + dbatch].reshape(dbatch, 6, 1, D_MODEL) for l in range(DEPTH)]

    p = dict(
        g_mix=g_mix, g_ffn=g_ffn,
        w_even=jnp.pad(w_in_even[0], ((0, 0), (0, LANES - N_GATES))).astype(BF16),
        w_in_odd=w_in_odd,
        w_out=w_out.astype(BF16), ffn_w_up=ffn_w_up, ffn_w_down=ffn_w_down,
        conv_w=ffn_conv_w, conv_b=ffn_conv_b,
        na_gq=na_gq[0], na_gk=na_gk[0], na_rpb=na_rpb[0],
        ml_b=jnp.pad(ml_b_gates[0], (0, LANES - N_GATES)).reshape(1, LANES), ml_g=ml_g_out[0],
        diff_gq=diff_gq[0], diff_gk=diff_gk[0], diff_lam=diff_lam[0], diff_g_out=diff_g_out[0],
        sg_g=sg_g_v[0], sg_w=sg_w[0], sg_b=sg_b[0],
    )

    ffn_w = {}
    y_prompt, ctx = _trunk(x_prompt.reshape(batch * seq, D_MODEL), mods_ctx, p, batch, seq, None, ffn_w)
    caches = dict(
        na_k=cache_na_k, na_v=cache_na_v, C=state_mlstm_C, n=state_mlstm_n,
        m=jnp.broadcast_to(state_mlstm_m[..., None], state_mlstm_m.shape + (LANES,)),
        diff_k=cache_diff_k, diff_v=cache_diff_v,
    )
    y_sample, _ = _trunk(x_sample.reshape(dbatch * dseq, D_MODEL), mods_lat, p, dbatch, dseq, caches, ffn_w)
    return (y_prompt.reshape(batch, seq, D_MODEL), y_sample.reshape(dbatch, dseq, D_MODEL),
            ctx["na_k"], ctx["na_v"], ctx["mlstm_C"], ctx["mlstm_n"], ctx["mlstm_m"],
            ctx["diff_k"], ctx["diff_v"])
```
